```python
import math
import jax, jax.numpy as jnp
from jax import lax
import numpy as np

D_MODEL = 2048
BATCH = 8
SEQ = 2048
DEPTH = 2

HEAD_DIM = 128
A_HEADS = 12
A_WIDTH = A_HEADS * HEAD_DIM
DILATED_CONFIGS = ((128, 1), (512, 4), (2048, 16))
POOL_WINDOWS = (2, 4, 8, 16)
POOL_GROUP = 128
B_WIDTH = len(POOL_WINDOWS) * POOL_GROUP
EVEN_IN_WIDTH = 3 * A_WIDTH + B_WIDTH
MIX_WIDTH = A_WIDTH + B_WIDTH
HGRN_HEADS = 16
HGRN_HEAD_DIM = 128
HGRN_WIDTH = HGRN_HEADS * HGRN_HEAD_DIM
HGRN_CHUNK = 64
D_FF = 5632
N_EXPERTS = 8
TOP_K = 2
D_EXPERT = 7168
MOE_BLOCK_ROWS = 512
N_EVEN = (DEPTH + 1) // 2
N_ODD = DEPTH // 2
EPS = 1e-6

kernel_name = "hybrid_dilated_pool_hgrn2_moe"


def rms_norm(x, g):
    xf = x.astype(jnp.float32)
    y = xf * lax.rsqrt(jnp.mean(xf * xf, axis=-1, keepdims=True) + EPS)
    return (y * g.astype(jnp.float32)).astype(x.dtype)


def dilated_branch(q, k, v, window, dil):
    B, S, H, hd = q.shape
    steps = window // dil
    L = S // dil
    nb = -(-L // steps)
    Lp = nb * steps

    def to_res(t):
        t = t.reshape(B, L, dil, H, hd).transpose(0, 2, 3, 1, 4)
        t = jnp.pad(t, ((0, 0), (0, 0), (0, 0), (0, Lp - L), (0, 0)))
        return t.reshape(B, dil, H, nb, steps, hd)

    def with_prev(t):
        prev = jnp.pad(t, ((0, 0), (0, 0), (0, 0), (1, 0), (0, 0), (0, 0)))[:, :, :, :-1]
        return jnp.concatenate([prev, t], axis=-2)

    qb = to_res(q)
    kk = with_prev(to_res(k))
    vv = with_prev(to_res(v))
    s = jnp.einsum('brhnqc,brhnkc->brhnqk', qb, kk).astype(jnp.float32) / math.sqrt(hd)
    n_idx = jnp.arange(nb)[:, None, None]
    i = jnp.arange(steps)[None, :, None]
    j = jnp.arange(2 * steps)[None, None, :]
    mask = (j >= i) & (j <= i + steps) & ((n_idx > 0) | (j >= steps))
    s = jnp.where(mask, s, -jnp.inf)
    lse = jax.nn.logsumexp(s, axis=-1)
    p = jnp.exp(s - lse[..., None])
    o = jnp.einsum('brhnqk,brhnkc->brhnqc', p.astype(vv.dtype), vv).astype(jnp.float32)
    o = o.reshape(B, dil, H, Lp, hd)[:, :, :, :L].transpose(0, 3, 1, 2, 4).reshape(B, S, H, hd)
    lse = lse.reshape(B, dil, H, Lp)[..., :L].transpose(0, 3, 1, 2).reshape(B, S, H)
    return o, lse


def pool_mixer(p, pool_w, pool_scale):
    B, S, _ = p.shape
    pf = p.astype(jnp.float32)
    c = jnp.cumsum(pf, axis=1)
    pos = jnp.arange(1, S + 1, dtype=jnp.int32)[None, :, None]
    outs = []
    for gi, w in enumerate(POOL_WINDOWS):
        sl = slice(gi * POOL_GROUP, (gi + 1) * POOL_GROUP)
        cg = c[..., sl]
        lagged = jnp.pad(cg, ((0, 0), (w, 0), (0, 0)))[:, :S]
        cnt = jnp.minimum(pos, w).astype(jnp.float32)
        pooled = (cg - lagged) / cnt - pf[..., sl]
        outs.append(pooled @ pool_w[gi].astype(jnp.float32))
    y = jnp.concatenate(outs, axis=-1) * pool_scale.astype(jnp.float32)
    return y.astype(p.dtype)


def even_mixer(h, w_in, pool_w, pool_scale, w_out):
    B, S, _ = h.shape
    proj = h @ w_in
    q, k, v, p = jnp.split(proj, [A_WIDTH, 2 * A_WIDTH, 3 * A_WIDTH], axis=-1)
    q, k, v = (t.reshape(B, S, A_HEADS, HEAD_DIM) for t in (q, k, v))
    outs, lses = [], []
    for window, dil in DILATED_CONFIGS:
        o, l = dilated_branch(q, k, v, window, dil)
        outs.append(o)
        lses.append(l)
    wts = jax.nn.softmax(jnp.stack(lses, axis=0), axis=0)
    a_out = jnp.einsum('gbsh,gbshc->bshc', wts, jnp.stack(outs, axis=0))
    a_out = a_out.astype(h.dtype).reshape(B, S, A_WIDTH)
    b_out = pool_mixer(p, pool_w, pool_scale)
    return (jnp.concatenate([a_out, b_out], axis=-1) @ w_out).astype(h.dtype)


def hgrn2_chunked(q, k, v, log_f):
    B, S, H, dk = q.shape
    dv = v.shape[-1]
    C = HGRN_CHUNK
    N = S // C

    def chunks(t):
        return t.astype(jnp.float32).reshape(B, N, C, H, t.shape[-1]).transpose(0, 3, 1, 2, 4)

    q, k, v, g = (chunks(t) for t in (q, k, v, log_f))
    b = jnp.cumsum(g, axis=3)
    b_mid = b[:, :, :, C // 2 - 1:C // 2]
    b_last = b[:, :, :, -1:]
    att = jnp.einsum('bhnik,bhnjk->bhnij', q * jnp.exp(b - b_mid), k * jnp.exp(b_mid - b))
    causal = jnp.tril(jnp.ones((C, C), dtype=bool))
    att = jnp.where(causal, att, 0.0)
    o_intra = jnp.einsum('bhnij,bhnjv->bhniv', att, v)
    u = jnp.einsum('bhnjk,bhnjv->bhnkv', k * jnp.exp(b_last - b), v)
    d = jnp.exp(b_last[:, :, :, 0])

    def step(state, xs):
        dn, un = xs
        return dn[..., None] * state + un, state

    init = jnp.zeros((B, H, dk, dv), jnp.float32)
    _, s_start = lax.scan(step, init, (jnp.moveaxis(d, 2, 0), jnp.moveaxis(u, 2, 0)))
    s_start = jnp.moveaxis(s_start, 0, 2)
    o_inter = jnp.einsum('bhnik,bhnkv->bhniv', q * jnp.exp(b), s_start)
    o = o_intra + o_inter
    return o.transpose(0, 2, 3, 1, 4).reshape(B, S, H, dv)


def hgrn2_mixer(h, w_in, lb, out_norm, w_out):
    B, S, _ = h.shape
    q, f_pre, i, gate = jnp.split(h @ w_in, 4, axis=-1)
    f = lb + (1.0 - lb) * jax.nn.sigmoid(f_pre.astype(jnp.float32))
    q = jax.nn.silu(q)

    def heads(t):
        return t.reshape(B, S, HGRN_HEADS, HGRN_HEAD_DIM)

    o = hgrn2_chunked(heads(q), heads(1.0 - f), heads(i), heads(jnp.log(f)))
    o = rms_norm(o, out_norm.reshape(HGRN_HEADS, HGRN_HEAD_DIM)).reshape(B, S, HGRN_WIDTH)
    o = o * jax.nn.silu(gate.astype(jnp.float32))
    return (o.astype(h.dtype) @ w_out).astype(h.dtype)


def swiglu(h, w_gate, w_up, w_down):
    return ((jax.nn.silu(h @ w_gate) * (h @ w_up)) @ w_down).astype(h.dtype)


def moe_swiglu(h, router_w, w_g, w_u, w_d):
    B, S, D = h.shape
    T = B * S
    xf = h.reshape(T, D)
    logits = (xf @ router_w).astype(jnp.float32)
    top_val, top_idx = lax.top_k(logits, TOP_K)
    gates = jax.nn.softmax(top_val, axis=-1)
    A = T * TOP_K
    e_flat = top_idx.reshape(A).astype(jnp.int32)
    tok = jnp.arange(A, dtype=jnp.int32) // TOP_K
    gate_flat = gates.reshape(A)
    order = jnp.argsort(e_flat)
    e_sorted = e_flat[order]
    counts = jnp.zeros((N_EXPERTS,), jnp.int32).at[e_flat].add(1)
    starts = jnp.cumsum(counts) - counts
    padded = (counts + MOE_BLOCK_ROWS - 1) // MOE_BLOCK_ROWS * MOE_BLOCK_ROWS
    pends = jnp.cumsum(padded)
    pstarts = pends - padded
    dest = pstarts[e_sorted] + jnp.arange(A, dtype=jnp.int32) - starts[e_sorted]
    NB = -(-A // MOE_BLOCK_ROWS) + N_EXPERTS
    P = NB * MOE_BLOCK_ROWS
    row_tok = jnp.full((P,), T, jnp.int32).at[dest].set(tok[order])
    row_gate = jnp.zeros((P,), jnp.float32).at[dest].set(gate_flat[order])
    xf_pad = jnp.concatenate([xf, jnp.zeros((1, D), xf.dtype)], axis=0)
    x_rows = xf_pad[row_tok].reshape(NB, MOE_BLOCK_ROWS, D)
    block_e = jnp.minimum(
        jnp.searchsorted(pends, jnp.arange(NB, dtype=jnp.int32) * MOE_BLOCK_ROWS, side='right'),
        N_EXPERTS - 1).astype(jnp.int32)

    def expert_block(args):
        xb, e = args
        return (jax.nn.silu(xb @ w_g[e]) * (xb @ w_u[e])) @ w_d[e]

    y_rows = lax.map(expert_block, (x_rows, block_e)).reshape(P, D)
    y_rows = y_rows.astype(jnp.float32) * row_gate[:, None]
    y = jnp.zeros((T + 1, D), jnp.float32).at[row_tok].add(y_rows)[:T]
    return y.reshape(B, S, D).astype(h.dtype)


def setup_inputs(seed: int = 0) -> dict:
    key = jax.random.key(seed)
    ks = jax.random.split(key, 20)

    def nrm(k, shape, scale):
        return jax.random.normal(k, shape, jnp.float32) * scale

    return {
        "x": nrm(ks[0], (BATCH, SEQ, D_MODEL), 1.0),
        "norm_mix": 1.0 + nrm(ks[1], (DEPTH, D_MODEL), 0.02),
        "norm_ffn": 1.0 + nrm(ks[2], (DEPTH, D_MODEL), 0.02),
        "w_in_even": nrm(ks[3], (N_EVEN, D_MODEL, EVEN_IN_WIDTH), D_MODEL ** -0.5),
        "pool_w": nrm(ks[4], (N_EVEN, len(POOL_WINDOWS), POOL_GROUP, POOL_GROUP), POOL_GROUP ** -0.5),
        "pool_scale": 1.0 + nrm(ks[5], (N_EVEN, B_WIDTH), 0.1),
        "w_out_even": nrm(ks[6], (N_EVEN, MIX_WIDTH, D_MODEL), MIX_WIDTH ** -0.5),
        "ffn_w_gate": nrm(ks[7], (N_EVEN, D_MODEL, D_FF), D_MODEL ** -0.5),
        "ffn_w_up": nrm(ks[8], (N_EVEN, D_MODEL, D_FF), D_MODEL ** -0.5),
        "ffn_w_down": nrm(ks[9], (N_EVEN, D_FF, D_MODEL), D_FF ** -0.5),
        "w_in_odd": nrm(ks[10], (N_ODD, D_MODEL, 4 * HGRN_WIDTH), D_MODEL ** -0.5),
        "lower_bound_logits": nrm(ks[11], (DEPTH, HGRN_WIDTH), 0.1),
        "hgrn_out_norm": 1.0 + nrm(ks[12], (N_ODD, HGRN_WIDTH), 0.02),
        "w_out_odd": nrm(ks[13], (N_ODD, HGRN_WIDTH, D_MODEL), HGRN_WIDTH ** -0.5),
        "router_w": nrm(ks[14], (N_ODD, D_MODEL, N_EXPERTS), D_MODEL ** -0.5),
        "moe_w_gate": nrm(ks[15], (N_ODD, N_EXPERTS, D_MODEL, D_EXPERT), D_MODEL ** -0.5),
        "moe_w_up": nrm(ks[16], (N_ODD, N_EXPERTS, D_MODEL, D_EXPERT), D_MODEL ** -0.5),
        "moe_w_down": nrm(ks[17], (N_ODD, N_EXPERTS, D_EXPERT, D_MODEL), D_EXPERT ** -0.5),
        "norm_final": 1.0 + nrm(ks[18], (D_MODEL,), 0.02),
    }


def reference(x, norm_mix, norm_ffn, w_in_even, pool_w, pool_scale, w_out_even,
              ffn_w_gate, ffn_w_up, ffn_w_down, w_in_odd, lower_bound_logits,
              hgrn_out_norm, w_out_odd, router_w, moe_w_gate, moe_w_up, moe_w_down,
              norm_final):
    lb_all = jnp.cumsum(jax.nn.softmax(lower_bound_logits.astype(jnp.float32), axis=0), axis=0)
    lb_all = lb_all - lb_all[:1]
    for layer in range(DEPTH):
        li = layer // 2
        h = rms_norm(x, norm_mix[layer])
        if layer % 2 == 0:
            x = x + even_mixer(h, w_in_even[li], pool_w[li], pool_scale[li], w_out_even[li])
            h = rms_norm(x, norm_ffn[layer])
            x = x + swiglu(h, ffn_w_gate[li], ffn_w_up[li], ffn_w_down[li])
        else:
            x = x + hgrn2_mixer(h, w_in_odd[li], lb_all[layer], hgrn_out_norm[li], w_out_odd[li])
            h = rms_norm(x, norm_ffn[layer])
            x = x + moe_swiglu(h, router_w[li], moe_w_gate[li], moe_w_up[li], moe_w_down[li])
    return rms_norm(x, norm_final)
```

```python
import functools
import math

import jax
import jax.numpy as jnp
from jax import lax
from jax.experimental import pallas as pl
from jax.experimental.pallas import tpu as pltpu

F32 = jnp.float32
BF16 = jnp.bfloat16

D_MODEL = 2048
HEAD_DIM = 128
A_HEADS = 12
A_WIDTH = A_HEADS * HEAD_DIM
N_POOL = 4
POOL_WINDOWS = (2, 4, 8, 16)
HGRN_HEADS = 16
HGRN_CHUNK = 64
N_EXPERTS = 8
EPS = 1e-6
NEG = -1e30

LANES = 128
VMEM_LIMIT = 56 * 1024 * 1024

MOE_ROWS = 512
MOE_TF = 256


def _cparams(sem, vmem=VMEM_LIMIT):
    return pltpu.CompilerParams(dimension_semantics=sem, vmem_limit_bytes=vmem)


def _rms(x, g):
    ms = jnp.mean(x * x, axis=-1, keepdims=True)
    return x * lax.rsqrt(ms + EPS) * g


def _silu(x):
    return x * jax.nn.sigmoid(x)


def _dot(a, b):
    return jnp.dot(a, b, preferred_element_type=F32)


def _dot_nt(a, b):
    return lax.dot_general(a, b, (((1,), (1,)), ((), ())), preferred_element_type=F32)


def _dot_tn(a, b):
    return lax.dot_general(a, b, (((0,), (0,)), ((), ())), preferred_element_type=F32)


def _norm_matmul_kernel(x_ref, g_ref, w_ref, o_ref, h_scr):
    @pl.when(pl.program_id(1) == 0)
    def _():
        h_scr[...] = _rms(x_ref[...], g_ref[...]).astype(BF16)

    o_ref[...] = _dot(h_scr[...], w_ref[...]).astype(o_ref.dtype)


def _norm_matmul(x, g, w, tm=1024, tn=1024):
    m, d = x.shape
    n = w.shape[1]
    return pl.pallas_call(
        _norm_matmul_kernel,
        grid=(m // tm, n // tn),
        in_specs=[
            pl.BlockSpec((tm, d), lambda i, j: (i, 0)),
            pl.BlockSpec((1, d), lambda i, j: (0, 0)),
            pl.BlockSpec((d, tn), lambda i, j: (0, j)),
        ],
        out_specs=pl.BlockSpec((tm, tn), lambda i, j: (i, j)),
        out_shape=jax.ShapeDtypeStruct((m, n), BF16),
        scratch_shapes=[pltpu.VMEM((tm, d), BF16)],
        compiler_params=_cparams(("parallel", "arbitrary")),
        name="norm_matmul",
    )(x, g.reshape(1, d), w)


def _matmul_res_kernel(a_ref, w_ref, r_ref, o_ref):
    o_ref[...] = r_ref[...] + _dot(a_ref[...], w_ref[...])


def _matmul_res(a, w, res, tm=1024, tn=1024):
    m, k = a.shape
    n = w.shape[1]
    return pl.pallas_call(
        _matmul_res_kernel,
        grid=(m // tm, n // tn),
        in_specs=[
            pl.BlockSpec((tm, k), lambda i, j: (i, 0)),
            pl.BlockSpec((k, tn), lambda i, j: (0, j)),
            pl.BlockSpec((tm, tn), lambda i, j: (i, j)),
        ],
        out_specs=pl.BlockSpec((tm, tn), lambda i, j: (i, j)),
        out_shape=jax.ShapeDtypeStruct((m, n), F32),
        compiler_params=_cparams(("parallel", "arbitrary")),
        name="matmul_res",
    )(a, w, res)


def _swiglu_kernel(x_ref, g_ref, wg_ref, wu_ref, wd_ref, o_ref, h_scr, acc_scr):
    j = pl.program_id(1)

    @pl.when(j == 0)
    def _():
        h_scr[...] = _rms(x_ref[...], g_ref[...]).astype(BF16)
        acc_scr[...] = jnp.zeros_like(acc_scr)

    h = h_scr[...]
    a = _dot(h, wg_ref[...])
    b = _dot(h, wu_ref[...])
    hid = (_silu(a) * b).astype(BF16)
    acc_scr[...] += _dot(hid, wd_ref[...])

    @pl.when(j == pl.num_programs(1) - 1)
    def _():
        o_ref[...] = x_ref[...] + acc_scr[...]


def _swiglu_res(x, g, wg, wu, wd, tm=512, tf=512):
    m, d = x.shape
    ff = wg.shape[1]
    return pl.pallas_call(
        _swiglu_kernel,
        grid=(m // tm, ff // tf),
        in_specs=[
            pl.BlockSpec((tm, d), lambda i, j: (i, 0)),
            pl.BlockSpec((1, d), lambda i, j: (0, 0)),
            pl.BlockSpec((d, tf), lambda i, j: (0, j)),
            pl.BlockSpec((d, tf), lambda i, j: (0, j)),
            pl.BlockSpec((tf, d), lambda i, j: (j, 0)),
        ],
        out_specs=pl.BlockSpec((tm, d), lambda i, j: (i, 0)),
        out_shape=jax.ShapeDtypeStruct((m, d), F32),
        scratch_shapes=[pltpu.VMEM((tm, d), BF16), pltpu.VMEM((tm, d), F32)],
        compiler_params=_cparams(("parallel", "arbitrary")),
        name="swiglu_res",
    )(x, g.reshape(1, d), wg, wu, wd)


def _softmax_block(qb, kb, vb, mask, scale):
    s = _dot_nt(qb, kb) * scale
    s = jnp.where(mask, s, NEG)
    m = jnp.max(s, axis=-1, keepdims=True)
    p = jnp.exp(s - m)
    l = jnp.sum(p, axis=-1, keepdims=True)
    acc = _dot(p.astype(BF16), vb)
    return m, l, acc


def _merge(m_a, l_a, a_a, m_b, l_b, a_b):
    m = jnp.maximum(m_a, m_b)
    w_a = jnp.exp(m_a - m)
    w_b = jnp.exp(m_b - m)
    return m, l_a * w_a + l_b * w_b, a_a * w_a + a_b * w_b


def _attn_pool_kernel(q_ref, k_ref, v_ref, pw_ref, ps_ref, o_ref,
                      qf, kf, vf, q16, k16, v16, m1, l1, a1, m2, l2, a2, outf):
    h = pl.program_id(1)
    seq = q_ref.shape[0]
    blk = HEAD_DIM
    n_res = seq // blk
    scale = 1.0 / math.sqrt(HEAD_DIM)

    @pl.when(h < A_HEADS)
    def _attention():
        qf[...] = q_ref[...].astype(F32)
        kf[...] = k_ref[...].astype(F32)
        vf[...] = v_ref[...].astype(F32)
        for r in range(n_res):
            rows = slice(r * blk, (r + 1) * blk)
            q16[rows, :] = qf[pl.ds(r, blk, stride=n_res), :].astype(BF16)
            k16[rows, :] = kf[pl.ds(r, blk, stride=n_res), :].astype(BF16)
            v16[rows, :] = vf[pl.ds(r, blk, stride=n_res), :].astype(BF16)

        iq = lax.broadcasted_iota(jnp.int32, (blk, blk), 0)
        jk = lax.broadcasted_iota(jnp.int32, (blk, blk), 1)
        iq2 = lax.broadcasted_iota(jnp.int32, (blk, 2 * blk), 0)
        jk2 = lax.broadcasted_iota(jnp.int32, (blk, 2 * blk), 1)

        def put(mr, lr, ar, rows, m, l, acc):
            mr[rows, :] = jnp.broadcast_to(m, acc.shape)
            lr[rows, :] = jnp.broadcast_to(l, acc.shape)
            ar[rows, :] = acc

        causal = jk <= iq
        band = (jk2 >= iq2) & (jk2 <= iq2 + blk)
        m, l, acc = _softmax_block(q_ref[0:blk, :], k_ref[0:blk, :], v_ref[0:blk, :],
                                   causal, scale)
        put(m1, l1, a1, slice(0, blk), m, l, acc)

        def b1(n, carry):
            q0 = pl.multiple_of(n * blk, blk)
            k0 = pl.multiple_of((n - 1) * blk, blk)
            m, l, acc = _softmax_block(q_ref[pl.ds(q0, blk), :], k_ref[pl.ds(k0, 2 * blk), :],
                                       v_ref[pl.ds(k0, 2 * blk), :], band, scale)
            put(m1, l1, a1, pl.ds(q0, blk), m, l, acc)
            return carry

        lax.fori_loop(1, seq // blk, b1, 0)

        sub = blk // 4
        lq = 4 * (iq % sub) + iq // sub
        lk0 = 4 * (jk % sub) + jk // sub
        mask_first = lk0 <= lq
        lq2 = 4 * (iq2 % sub) + iq2 // sub
        lk2 = 4 * (jk2 % (2 * sub)) + jk2 // (2 * sub) - blk
        dist = lq2 - lk2
        mask_band4 = (dist >= 0) & (dist <= blk)
        for r4 in range(4):
            bases = [(r4 + 4 * c) * blk for c in range(4)]

            def gather(ref, off, size):
                return jnp.concatenate([ref[pl.ds(b0 + off, size), :] for b0 in bases], axis=0)

            def scatter(mr, lr, ar, off, m, l, acc):
                mb = jnp.broadcast_to(m, acc.shape)
                lb = jnp.broadcast_to(l, acc.shape)
                for c, b0 in enumerate(bases):
                    rows = pl.ds(b0 + off, sub)
                    src = slice(c * sub, (c + 1) * sub)
                    mr[rows, :] = mb[src]
                    lr[rows, :] = lb[src]
                    ar[rows, :] = acc[src]

            qb = gather(q16, 0, sub)
            m, l, acc = _softmax_block(qb, gather(k16, 0, sub), gather(v16, 0, sub),
                                       mask_first, scale)
            scatter(m2, l2, a2, 0, m, l, acc)

            def b2(n, carry):
                qo = pl.multiple_of(n * sub, sub)
                ko = pl.multiple_of((n - 1) * sub, sub)
                m, l, acc = _softmax_block(gather(q16, qo, sub), gather(k16, ko, 2 * sub),
                                           gather(v16, ko, 2 * sub), mask_band4, scale)
                scatter(m2, l2, a2, qo, m, l, acc)
                return carry

            lax.fori_loop(1, 4, b2, 0)

        def b3(r, carry):
            rows = pl.ds(pl.multiple_of(r * blk, blk), blk)
            m, l, acc = _softmax_block(q16[rows, :], k16[rows, :], v16[rows, :], causal, scale)
            mm, ll, aa = _merge(m2[rows, :], l2[rows, :], a2[rows, :], m, l, acc)
            m2[rows, :] = mm
            l2[rows, :] = ll
            a2[rows, :] = aa
            return carry

        lax.fori_loop(0, n_res, b3, 0)

        for r in range(n_res):
            nat = pl.ds(r, blk, stride=n_res)
            rows = slice(r * blk, (r + 1) * blk)
            _, ll, aa = _merge(m1[nat, :], l1[nat, :], a1[nat, :],
                               m2[rows, :], l2[rows, :], a2[rows, :])
            outf[nat, :] = aa / ll
        o_ref[...] = outf[...].astype(o_ref.dtype)

    for gi, w in enumerate(POOL_WINDOWS):
        @pl.when(h == A_HEADS + gi)
        def _pool(gi=gi, w=w):
            pf = q_ref[...].astype(F32)
            row = lax.broadcasted_iota(jnp.int32, pf.shape, 0)
            s = pf
            k = 1
            while k < w:
                s = s + jnp.where(row >= k, pltpu.roll(s, k, axis=0), 0.0)
                k *= 2
            cnt = jnp.minimum(row + 1, w).astype(F32)
            pooled = s / cnt - pf
            y = _dot(pooled.astype(BF16), pw_ref[gi].astype(BF16)) * ps_ref[...]
            o_ref[...] = y.astype(o_ref.dtype)


def _attn_pool(proj, pool_w, pool_scale):
    bsz, seq, _ = proj.shape
    n_steps = A_HEADS + N_POOL
    qmap = lambda b, h: (b, 0, jnp.where(h < A_HEADS, h, h + 2 * A_HEADS))
    kmap = lambda b, h: (b, 0, A_HEADS + jnp.minimum(h, A_HEADS - 1))
    vmap = lambda b, h: (b, 0, 2 * A_HEADS + jnp.minimum(h, A_HEADS - 1))
    f32s = pltpu.VMEM((seq, HEAD_DIM), F32)
    bf16s = pltpu.VMEM((seq, HEAD_DIM), BF16)
    return pl.pallas_call(
        _attn_pool_kernel,
        grid=(bsz, n_steps),
        in_specs=[
            pl.BlockSpec((None, seq, HEAD_DIM), qmap),
            pl.BlockSpec((None, seq, HEAD_DIM), kmap),
            pl.BlockSpec((None, seq, HEAD_DIM), vmap),
            pl.BlockSpec((N_POOL, HEAD_DIM, HEAD_DIM), lambda b, h: (0, 0, 0)),
            pl.BlockSpec((1, HEAD_DIM), lambda b, h: (0, jnp.maximum(h - A_HEADS, 0))),
        ],
        out_specs=pl.BlockSpec((None, seq, HEAD_DIM), lambda b, h: (b, 0, h)),
        out_shape=jax.ShapeDtypeStruct((bsz, seq, n_steps * HEAD_DIM), BF16),
        scratch_shapes=[f32s, f32s, f32s, bf16s, bf16s, bf16s,
                        f32s, f32s, f32s, f32s, f32s, f32s, f32s],
        compiler_params=_cparams(("parallel", "arbitrary")),
        name="attn_pool",
    )(proj, proj, proj, pool_w, pool_scale.reshape(1, N_POOL * HEAD_DIM))


def _hgrn_kernel(q_ref, f_ref, i_ref, gt_ref, lb_ref, on_ref, o_ref,
                 qe_scr, o_scr, u_scr, d_scr):
    seq = q_ref.shape[0]
    ck = HGRN_CHUNK
    grp = 4 * ck
    n_grp = seq // grp
    n_chunk = seq // ck
    lb = lb_ref[...]

    ri = lax.broadcasted_iota(jnp.int32, (grp, grp), 0)
    ci = lax.broadcasted_iota(jnp.int32, (grp, grp), 1)
    tri_mask = (ri // ck == ci // ck) & (ci <= ri)
    tri = tri_mask.astype(BF16)

    def phase_a(gi, carry):
        rows = pl.ds(pl.multiple_of(gi * grp, grp), grp)
        q = _silu(q_ref[rows, :].astype(F32))
        f = lb + (1.0 - lb) * jax.nn.sigmoid(f_ref[rows, :].astype(F32))
        k = 1.0 - f
        g = jnp.log(f)
        v = i_ref[rows, :]
        g_hi = g.astype(BF16)
        g_lo = (g - g_hi.astype(F32)).astype(BF16)
        b = _dot(tri, g_hi) + _dot(tri, g_lo)
        b3 = b.reshape(4, ck, HEAD_DIM)
        b_mid = b3[:, ck // 2 - 1:ck // 2, :]
        b_last = b3[:, ck - 1:ck, :]
        q3 = q.reshape(4, ck, HEAD_DIM)
        k3 = k.reshape(4, ck, HEAD_DIM)
        qs = (q3 * jnp.exp(b3 - b_mid)).reshape(grp, HEAD_DIM).astype(BF16)
        ks = (k3 * jnp.exp(b_mid - b3)).reshape(grp, HEAD_DIM).astype(BF16)
        att = jnp.where(tri_mask, _dot_nt(qs, ks), 0.0)
        o_scr[rows, :] = _dot(att.astype(BF16), v)
        qe_scr[rows, :] = (q * jnp.exp(b)).astype(BF16)
        kd = (k3 * jnp.exp(b_last - b3)).reshape(grp, HEAD_DIM).astype(BF16)
        for c in range(4):
            cr = slice(c * ck, (c + 1) * ck)
            u_scr[gi * 4 + c] = _dot_tn(v[cr, :], kd[cr, :])
        d_scr[pl.ds(pl.multiple_of(gi * 4, 4), 4), :] = jnp.exp(b_last).reshape(4, HEAD_DIM)
        return carry

    lax.fori_loop(0, n_grp, phase_a, 0)

    def phase_b(c, st):
        rows = pl.ds(pl.multiple_of(c * ck, ck), ck)
        o_scr[rows, :] += _dot_nt(qe_scr[rows, :], st.astype(BF16))
        return st * d_scr[pl.ds(c, 1), :] + u_scr[c]

    lax.fori_loop(0, n_chunk, phase_b, jnp.zeros((HEAD_DIM, HEAD_DIM), F32))

    o = _rms(o_scr[...], on_ref[...])
    o_ref[...] = (o * _silu(gt_ref[...].astype(F32))).astype(o_ref.dtype)


def _hgrn(proj, lb, out_norm):
    bsz, seq, _ = proj.shape
    nh = HGRN_HEADS
    spec = lambda off: pl.BlockSpec((None, seq, HEAD_DIM), lambda b, h: (b, 0, off + h))
    vec = pl.BlockSpec((1, HEAD_DIM), lambda b, h: (0, h))
    return pl.pallas_call(
        _hgrn_kernel,
        grid=(bsz, nh),
        in_specs=[spec(0), spec(nh), spec(2 * nh), spec(3 * nh), vec, vec],
        out_specs=pl.BlockSpec((None, seq, HEAD_DIM), lambda b, h: (b, 0, h)),
        out_shape=jax.ShapeDtypeStruct((bsz, seq, nh * HEAD_DIM), BF16),
        scratch_shapes=[
            pltpu.VMEM((seq, HEAD_DIM), BF16),
            pltpu.VMEM((seq, HEAD_DIM), F32),
            pltpu.VMEM((seq // HGRN_CHUNK, HEAD_DIM, HEAD_DIM), F32),
            pltpu.VMEM((seq // HGRN_CHUNK, HEAD_DIM), F32),
        ],
        compiler_params=_cparams(("parallel", "arbitrary")),
        name="hgrn2",
    )(proj, proj, proj, proj, lb.reshape(1, -1), out_norm.reshape(1, -1))


def _router_kernel(x_ref, g_ref, w_ref, h_ref, meta_ref, cnt_ref, carry):
    i = pl.program_id(0)
    tm = x_ref.shape[0]

    @pl.when(i == 0)
    def _():
        carry[...] = jnp.zeros_like(carry)

    h = _rms(x_ref[...], g_ref[...])
    h_ref[...] = h
    h_hi = h.astype(BF16)
    h_lo = (h - h_hi.astype(F32)).astype(BF16)
    w = w_ref[...]
    w_hi = w.astype(BF16)
    w_lo = (w - w_hi.astype(F32)).astype(BF16)
    logits = _dot(h_hi, w_hi) + (_dot(h_hi, w_lo) + _dot(h_lo, w_hi))
    lane = lax.broadcasted_iota(jnp.int32, logits.shape, 1)
    logits = jnp.where(lane < N_EXPERTS, logits, NEG)
    m1 = jnp.max(logits, axis=-1, keepdims=True)
    i1 = jnp.min(jnp.where(logits == m1, lane, LANES), axis=-1, keepdims=True)
    rest = jnp.where(lane == i1, NEG, logits)
    m2 = jnp.max(rest, axis=-1, keepdims=True)
    i2 = jnp.min(jnp.where(rest == m2, lane, LANES), axis=-1, keepdims=True)
    e = jnp.exp(m2 - m1)
    g1 = 1.0 / (1.0 + e)
    g2 = e / (1.0 + e)
    oh1 = (lane == i1).astype(F32)
    oh2 = (lane == i2).astype(F32)
    cnt = oh1 + oh2
    ri = lax.broadcasted_iota(jnp.int32, (tm, tm), 0)
    ci = lax.broadcasted_iota(jnp.int32, (tm, tm), 1)
    before = (ci < ri).astype(BF16)
    prior = _dot(before, cnt.astype(BF16)) + carry[...]
    r1 = jnp.sum(oh1 * prior, axis=-1, keepdims=True)
    r2 = jnp.sum(oh2 * prior, axis=-1, keepdims=True)
    carry[...] += jnp.sum(cnt, axis=0, keepdims=True)
    cnt_ref[...] = carry[...]
    meta = jnp.where(lane == 0, i1.astype(F32), 0.0)
    meta = jnp.where(lane == 1, i2.astype(F32), meta)
    meta = jnp.where(lane == 2, r1, meta)
    meta = jnp.where(lane == 3, r2, meta)
    meta = jnp.where(lane == 4, g1, meta)
    meta = jnp.where(lane == 5, g2, meta)
    meta_ref[...] = meta


def _router(x, g, router_w, tm=512):
    t, d = x.shape
    w_pad = jnp.pad(router_w, ((0, 0), (0, LANES - router_w.shape[1])))
    return pl.pallas_call(
        _router_kernel,
        grid=(t // tm,),
        in_specs=[
            pl.BlockSpec((tm, d), lambda i: (i, 0)),
            pl.BlockSpec((1, d), lambda i: (0, 0)),
            pl.BlockSpec((d, LANES), lambda i: (0, 0)),
        ],
        out_specs=[
            pl.BlockSpec((tm, d), lambda i: (i, 0)),
            pl.BlockSpec((tm, LANES), lambda i: (i, 0)),
            pl.BlockSpec((1, LANES), lambda i: (0, 0)),
        ],
        out_shape=[
            jax.ShapeDtypeStruct((t, d), F32),
            jax.ShapeDtypeStruct((t, LANES), F32),
            jax.ShapeDtypeStruct((1, LANES), F32),
        ],
        scratch_shapes=[pltpu.VMEM((1, LANES), F32)],
        compiler_params=_cparams(("arbitrary",)),
        name="moe_router",
    )(x, g.reshape(1, d), w_pad)


def _row_copy(src, dst, s, d, sem):
    return pltpu.make_async_copy(src.at[pl.ds(s, 1)], dst.at[pl.ds(d, 1)], sem)


def _dispatch_kernel(dest_ref, h_hbm, o_hbm, sem):
    i = pl.program_id(0)
    tm = dest_ref.shape[2] // 2

    def start(t, carry):
        tok = i * tm + t
        _row_copy(h_hbm, o_hbm, tok, dest_ref[0, 0, 2 * t], sem).start()
        _row_copy(h_hbm, o_hbm, tok, dest_ref[0, 0, 2 * t + 1], sem).start()
        return carry

    lax.fori_loop(0, tm, start, 0)

    def wait(t, carry):
        _row_copy(h_hbm, o_hbm, 0, 0, sem).wait()
        _row_copy(h_hbm, o_hbm, 0, 0, sem).wait()
        return carry

    lax.fori_loop(0, tm, wait, 0)


def _dispatch(h, dest, n_rows, tm=256):
    t, d = h.shape
    dest3 = dest.reshape(t // tm, 1, 2 * tm)
    return pl.pallas_call(
        _dispatch_kernel,
        grid=(t // tm,),
        in_specs=[
            pl.BlockSpec((1, 1, 2 * tm), lambda i: (i, 0, 0), memory_space=pltpu.SMEM),
            pl.BlockSpec(memory_space=pl.ANY),
        ],
        out_specs=pl.BlockSpec(memory_space=pl.ANY),
        out_shape=jax.ShapeDtypeStruct((n_rows, d), F32),
        scratch_shapes=[pltpu.SemaphoreType.DMA(())],
        compiler_params=_cparams(("arbitrary",)),
        name="moe_dispatch",
    )(dest3, h)


def _moe_kernel(be_ref, bb_ref, nv_ref, na_ref, x_ref, wg_ref, wu_ref, wd_ref, o_ref, xb_scr):
    i = pl.program_id(0)
    j = pl.program_id(1)

    @pl.when(i < na_ref[0])
    def _():
        @pl.when(j == 0)
        def _():
            row = lax.broadcasted_iota(jnp.int32, x_ref.shape, 0)
            xb_scr[...] = jnp.where(row < nv_ref[i], x_ref[...], 0.0).astype(BF16)
            o_ref[...] = jnp.zeros_like(o_ref)

        xb = xb_scr[...]
        a = _dot(xb, wg_ref[0].astype(BF16))
        b = _dot(xb, wu_ref[0].astype(BF16))
        hid = (_silu(a) * b).astype(BF16)
        o_ref[...] += _dot(hid, wd_ref[0].astype(BF16))


def _moe_experts(x_rows, wg, wu, wd, blk_e, blk_idx, blk_valid, n_live, n_blocks):
    _, d = x_rows.shape
    dff = wg.shape[2]
    n_ff = dff // MOE_TF
    r = MOE_ROWS

    def ffi(i, j, na):
        return jnp.where(i < na[0], j, n_ff - 1)

    grid_spec = pltpu.PrefetchScalarGridSpec(
        num_scalar_prefetch=4,
        grid=(n_blocks, n_ff),
        in_specs=[
            pl.BlockSpec((r, d), lambda i, j, be, bb, nv, na: (bb[i], 0)),
            pl.BlockSpec((1, d, MOE_TF), lambda i, j, be, bb, nv, na: (be[i], 0, ffi(i, j, na))),
            pl.BlockSpec((1, d, MOE_TF), lambda i, j, be, bb, nv, na: (be[i], 0, ffi(i, j, na))),
            pl.BlockSpec((1, MOE_TF, d), lambda i, j, be, bb, nv, na: (be[i], ffi(i, j, na), 0)),
        ],
        out_specs=pl.BlockSpec((r, d), lambda i, j, be, bb, nv, na: (bb[i], 0)),
        scratch_shapes=[pltpu.VMEM((r, d), BF16)],
    )
    return pl.pallas_call(
        _moe_kernel,
        grid_spec=grid_spec,
        out_shape=jax.ShapeDtypeStruct(x_rows.shape, F32),
        compiler_params=_cparams(("arbitrary", "arbitrary")),
        name="moe_experts",
    )(blk_e, blk_idx, blk_valid, n_live, x_rows, wg, wu, wd)


def _combine_kernel(dest_ref, x_ref, meta_ref, nf_ref, y_hbm, o_ref, buf0, buf1, sem):
    tm = x_ref.shape[0]

    def start(t, carry):
        _row_copy(y_hbm, buf0, dest_ref[0, 0, 2 * t], t, sem).start()
        _row_copy(y_hbm, buf1, dest_ref[0, 0, 2 * t + 1], t, sem).start()
        return carry

    lax.fori_loop(0, tm, start, 0)

    def wait(t, carry):
        _row_copy(y_hbm, buf0, 0, 0, sem).wait()
        _row_copy(y_hbm, buf1, 0, 0, sem).wait()
        return carry

    lax.fori_loop(0, tm, wait, 0)

    meta = meta_ref[...]
    g1 = meta[:, 4:5]
    g2 = meta[:, 5:6]
    x = x_ref[...] + (g1 * buf0[...] + g2 * buf1[...])
    o_ref[...] = _rms(x, nf_ref[...])


def _combine(x, meta, norm_final, y_rows, dest, tm=256):
    t, d = x.shape
    dest3 = dest.reshape(t // tm, 1, 2 * tm)
    return pl.pallas_call(
        _combine_kernel,
        grid=(t // tm,),
        in_specs=[
            pl.BlockSpec((1, 1, 2 * tm), lambda i: (i, 0, 0), memory_space=pltpu.SMEM),
            pl.BlockSpec((tm, d), lambda i: (i, 0)),
            pl.BlockSpec((tm, LANES), lambda i: (i, 0)),
            pl.BlockSpec((1, d), lambda i: (0, 0)),
            pl.BlockSpec(memory_space=pl.ANY),
        ],
        out_specs=pl.BlockSpec((tm, d), lambda i: (i, 0)),
        out_shape=jax.ShapeDtypeStruct((t, d), F32),
        scratch_shapes=[pltpu.VMEM((tm, d), F32), pltpu.VMEM((tm, d), F32),
                        pltpu.SemaphoreType.DMA(())],
        compiler_params=_cparams(("arbitrary",)),
        name="moe_combine",
    )(dest3, x, meta, norm_final.reshape(1, d), y_rows)


def _moe_block_table(counts, t):
    r = MOE_ROWS
    n_blocks = (2 * t) // r + N_EXPERTS
    nblk = (counts + r - 1) // r
    ends = jnp.cumsum(nblk)
    starts = ends - nblk
    n_live = ends[-1]
    step = jnp.minimum(jnp.arange(n_blocks, dtype=jnp.int32), n_live - 1)
    blk_e = jnp.searchsorted(ends, step, side="right").astype(jnp.int32)
    local = step - starts[blk_e]
    blk_valid = jnp.clip(counts[blk_e] - local * r, 0, r)
    return (starts * r, blk_e, step, blk_valid.astype(jnp.int32),
            n_live.reshape(1).astype(jnp.int32), n_blocks)


def kernel(x, norm_mix, norm_ffn, w_in_even, pool_w, pool_scale, w_out_even, ffn_w_gate,
           ffn_w_up, ffn_w_down, w_in_odd, lower_bound_logits, hgrn_out_norm, w_out_odd,
           router_w, moe_w_gate, moe_w_up, moe_w_down, norm_final):
    bsz, seq, d = x.shape
    t = bsz * seq
    xt = x.reshape(t, d)

    lb_all = jnp.cumsum(jax.nn.softmax(lower_bound_logits.astype(F32), axis=0), axis=0)
    lb_all = lb_all - lb_all[:1]

    proj = _norm_matmul(xt, norm_mix[0], w_in_even[0].astype(BF16))
    mix = _attn_pool(proj.reshape(bsz, seq, -1), pool_w[0], pool_scale[0])
    xt = _matmul_res(mix.reshape(t, -1), w_out_even[0].astype(BF16), xt)
    xt = _swiglu_res(xt, norm_ffn[0], ffn_w_gate[0].astype(BF16), ffn_w_up[0].astype(BF16),
                     ffn_w_down[0].astype(BF16))

    proj = _norm_matmul(xt, norm_mix[1], w_in_odd[0].astype(BF16))
    o = _hgrn(proj.reshape(bsz, seq, -1), lb_all[1], hgrn_out_norm[0])
    xt = _matmul_res(o.reshape(t, -1), w_out_odd[0].astype(BF16), xt)

    h, meta, cnt = _router(xt, norm_ffn[1], router_w[0])
    counts = cnt[0, :N_EXPERTS].astype(jnp.int32)
    row0, blk_e, blk_idx, blk_valid, n_live, n_blocks = _moe_block_table(counts, t)
    dest = (row0[meta[:, 0:2].astype(jnp.int32)] + meta[:, 2:4].astype(jnp.int32)).reshape(-1)
    x_rows = _dispatch(h, dest, n_blocks * MOE_ROWS)
    y_rows = _moe_experts(x_rows, moe_w_gate[0], moe_w_up[0], moe_w_down[0],
                          blk_e, blk_idx, blk_valid, n_live, n_blocks)
    out = _combine(xt, meta, norm_final, y_rows, dest)
    return out.reshape(bsz, seq, d)
```

```python
import math

import jax
import jax.numpy as jnp
from jax import lax
from jax.experimental import pallas as pl
from jax.experimental.pallas import tpu as pltpu

F32 = jnp.float32
BF16 = jnp.bfloat16

D_MODEL = 2048
HEAD_DIM = 128
A_HEADS = 12
A_WIDTH = A_HEADS * HEAD_DIM
N_POOL = 4
POOL_WINDOWS = (2, 4, 8, 16)
HGRN_HEADS = 16
HGRN_CHUNK = 64
N_EXPERTS = 8
EPS = 1e-6
NEG = -1e30

LANES = 128
VMEM_LIMIT = 56 * 1024 * 1024

MOE_ROWS = 512
MOE_TF = 256

N_RES = 16


def _cparams(sem, vmem=VMEM_LIMIT):
    return pltpu.CompilerParams(dimension_semantics=sem, vmem_limit_bytes=vmem)


def _rms(x, g):
    ms = jnp.mean(x * x, axis=-1, keepdims=True)
    return x * lax.rsqrt(ms + EPS) * g


def _silu(x):
    return x * jax.nn.sigmoid(x)


def _dot(a, b):
    return jnp.dot(a, b, preferred_element_type=F32)


def _dot_nt(a, b):
    return lax.dot_general(a, b, (((1,), (1,)), ((), ())), preferred_element_type=F32)


def _dot_tn(a, b):
    return lax.dot_general(a, b, (((0,), (0,)), ((), ())), preferred_element_type=F32)


def _norm_matmul_kernel(x_ref, g_ref, w_ref, o_ref, h_scr):
    @pl.when(pl.program_id(1) == 0)
    def _():
        h_scr[...] = _rms(x_ref[...], g_ref[...]).astype(BF16)

    o_ref[...] = _dot(h_scr[...], w_ref[...]).astype(o_ref.dtype)


def _norm_matmul(x, g, w, tm=1024, tn=1024):
    m, d = x.shape
    n = w.shape[1]
    return pl.pallas_call(
        _norm_matmul_kernel,
        grid=(m // tm, n // tn),
        in_specs=[
            pl.BlockSpec((tm, d), lambda i, j: (i, 0)),
            pl.BlockSpec((1, d), lambda i, j: (0, 0)),
            pl.BlockSpec((d, tn), lambda i, j: (0, j)),
        ],
        out_specs=pl.BlockSpec((tm, tn), lambda i, j: (i, j)),
        out_shape=jax.ShapeDtypeStruct((m, n), BF16),
        scratch_shapes=[pltpu.VMEM((tm, d), BF16)],
        compiler_params=_cparams(("parallel", "arbitrary")),
        name="norm_matmul",
    )(x, g.reshape(1, d), w)


def _norm_matmul_res_kernel(x_ref, g_ref, w_ref, cs_ref, o_ref, h_scr):
    rows = x_ref.shape[0]
    d = g_ref.shape[1]

    @pl.when(pl.program_id(1) == 0)
    def _():
        for r in range(x_ref.shape[1] // d):
            h_scr[r * rows:(r + 1) * rows, :] = _rms(
                x_ref[:, r * d:(r + 1) * d], g_ref[...]).astype(BF16)

    o_ref[...] = (_dot(h_scr[...], w_ref[...]) * cs_ref[...]).astype(o_ref.dtype)


def _norm_matmul_res(x, g, w, col_scale, n_res_tile=8, tn=1024):
    bsz, seq, d = x.shape
    n = w.shape[1]
    rows = seq // N_RES
    tm = n_res_tile * rows
    halves = N_RES // n_res_tile
    xv = x.reshape(bsz, rows, N_RES * d)
    return pl.pallas_call(
        _norm_matmul_res_kernel,
        grid=(bsz * halves, n // tn),
        in_specs=[
            pl.BlockSpec((None, rows, n_res_tile * d), lambda i, j: (i // halves, 0, i % halves)),
            pl.BlockSpec((1, d), lambda i, j: (0, 0)),
            pl.BlockSpec((d, tn), lambda i, j: (0, j)),
            pl.BlockSpec((1, tn), lambda i, j: (0, j)),
        ],
        out_specs=pl.BlockSpec((tm, tn), lambda i, j: (i, j)),
        out_shape=jax.ShapeDtypeStruct((bsz * seq, n), BF16),
        scratch_shapes=[pltpu.VMEM((tm, d), BF16)],
        compiler_params=_cparams(("parallel", "arbitrary")),
        name="norm_matmul_res",
    )(xv, g.reshape(1, d), w, col_scale.reshape(1, n))


def _matmul_res_kernel(a_ref, w_ref, r_ref, o_ref):
    o_ref[...] = r_ref[...] + _dot(a_ref[...], w_ref[...])


def _matmul_res(a, w, res, tm=1024, tn=1024):
    m, k = a.shape
    n = w.shape[1]
    return pl.pallas_call(
        _matmul_res_kernel,
        grid=(m // tm, n // tn),
        in_specs=[
            pl.BlockSpec((tm, k), lambda i, j: (i, 0)),
            pl.BlockSpec((k, tn), lambda i, j: (0, j)),
            pl.BlockSpec((tm, tn), lambda i, j: (i, j)),
        ],
        out_specs=pl.BlockSpec((tm, tn), lambda i, j: (i, j)),
        out_shape=jax.ShapeDtypeStruct((m, n), F32),
        compiler_params=_cparams(("parallel", "arbitrary")),
        name="matmul_res",
    )(a, w, res)


def _matmul_unres_kernel(a_ref, w_ref, r_ref, o_ref):
    rows = r_ref.shape[0]
    d = w_ref.shape[1]
    y = _dot(a_ref[...], w_ref[...])
    for r in range(r_ref.shape[1] // d):
        o_ref[:, r * d:(r + 1) * d] = r_ref[:, r * d:(r + 1) * d] + y[r * rows:(r + 1) * rows, :]


def _matmul_unres(a, w, res, n_res_tile=4):
    bsz, seq, d = res.shape
    k = a.shape[1]
    rows = seq // N_RES
    tm = n_res_tile * rows
    parts = N_RES // n_res_tile
    resv = res.reshape(bsz, rows, N_RES * d)
    io_spec = pl.BlockSpec((None, rows, n_res_tile * d), lambda i: (i // parts, 0, i % parts))
    out = pl.pallas_call(
        _matmul_unres_kernel,
        grid=(bsz * parts,),
        in_specs=[
            pl.BlockSpec((tm, k), lambda i: (i, 0)),
            pl.BlockSpec((k, d), lambda i: (0, 0)),
            io_spec,
        ],
        out_specs=io_spec,
        out_shape=jax.ShapeDtypeStruct(resv.shape, F32),
        compiler_params=_cparams(("parallel",)),
        name="matmul_unres",
    )(a, w, resv)
    return out.reshape(bsz * seq, d)


def _swiglu_kernel(x_ref, g_ref, wg_ref, wu_ref, wd_ref, o_ref, h_scr, acc_scr):
    j = pl.program_id(1)

    @pl.when(j == 0)
    def _():
        h_scr[...] = _rms(x_ref[...], g_ref[...]).astype(BF16)
        acc_scr[...] = jnp.zeros_like(acc_scr)

    h = h_scr[...]
    a = _dot(h, wg_ref[...])
    b = _dot(h, wu_ref[...])
    hid = (_silu(a) * b).astype(BF16)
    acc_scr[...] += _dot(hid, wd_ref[...])

    @pl.when(j == pl.num_programs(1) - 1)
    def _():
        o_ref[...] = x_ref[...] + acc_scr[...]


def _swiglu_res(x, g, wg, wu, wd, tm=512, tf=512):
    m, d = x.shape
    ff = wg.shape[1]
    return pl.pallas_call(
        _swiglu_kernel,
        grid=(m // tm, ff // tf),
        in_specs=[
            pl.BlockSpec((tm, d), lambda i, j: (i, 0)),
            pl.BlockSpec((1, d), lambda i, j: (0, 0)),
            pl.BlockSpec((d, tf), lambda i, j: (0, j)),
            pl.BlockSpec((d, tf), lambda i, j: (0, j)),
            pl.BlockSpec((tf, d), lambda i, j: (j, 0)),
        ],
        out_specs=pl.BlockSpec((tm, d), lambda i, j: (i, 0)),
        out_shape=jax.ShapeDtypeStruct((m, d), F32),
        scratch_shapes=[pltpu.VMEM((tm, d), BF16), pltpu.VMEM((tm, d), F32)],
        compiler_params=_cparams(("parallel", "arbitrary")),
        name="swiglu_res",
    )(x, g.reshape(1, d), wg, wu, wd)


def _attn_bias(bias):
    blk = HEAD_DIM
    iq = lax.broadcasted_iota(jnp.int32, (blk, 2 * blk), 0)
    jk = lax.broadcasted_iota(jnp.int32, (blk, 2 * blk), 1)

    def offs(idx, n_chunk, size):
        return n_chunk * (idx % size) + idx // size

    def put(i, valid):
        bias[i] = jnp.where(valid, 0.0, NEG)

    for i, n_chunk in enumerate((1, 4, 16)):
        size = blk // n_chunk
        lq = offs(iq, n_chunk, size)
        put(2 * i, (offs(jk, n_chunk, size) <= lq) & (jk < blk))
        dist = lq - (offs(jk, n_chunk, 2 * size) - blk)
        put(2 * i + 1, (dist >= 0) & (dist <= blk))


def _softmax_block(qb, kb, vb, bias):
    s = _dot_nt(qb, kb) + bias
    m = jnp.max(s, axis=-1, keepdims=True)
    p = jnp.exp2(s - m)
    l = jnp.sum(p, axis=-1, keepdims=True)
    acc = _dot(p.astype(BF16), vb)
    return m, l, acc


def _attn_pool_kernel(q_ref, k_ref, v_ref, pw_ref, ps_ref, o_ref,
                      qf, kf, vf, bias, ms, ls, accs):
    h = pl.program_id(1)
    seq = q_ref.shape[0]
    blk = HEAD_DIM
    n_blk = seq // blk

    @pl.when(h < A_HEADS)
    def _attention():
        qf[...] = q_ref[...].astype(F32)
        kf[...] = k_ref[...].astype(F32)
        vf[...] = v_ref[...].astype(F32)
        _attn_bias(bias)

        def gather(ref, starts, size):
            return jnp.concatenate([ref[s0:s0 + size, :] for s0 in starts], axis=0).astype(BF16)

        def scatter(br, starts, size, m, l, acc):
            mb = jnp.broadcast_to(m, acc.shape)
            lb = jnp.broadcast_to(l, acc.shape)
            for c, s0 in enumerate(starts):
                src = slice(c * size, (c + 1) * size)
                ms[br, s0:s0 + size, :] = mb[src]
                ls[br, s0:s0 + size, :] = lb[src]
                accs[br, s0:s0 + size, :] = acc[src]

        for br, n_chunk in enumerate((1, 4, 16)):
            size = blk // n_chunk
            for grp in range(N_RES // n_chunk):
                bases = [(grp + (N_RES // n_chunk) * c) * blk for c in range(n_chunk)]
                for n in range(n_chunk):
                    qs = [b0 + n * size for b0 in bases]
                    if n == 0:
                        ks, ksize, bi = qs, size, bias[2 * br][:, :blk]
                    else:
                        ks, ksize, bi = [s0 - size for s0 in qs], 2 * size, bias[2 * br + 1]
                    m, l, acc = _softmax_block(gather(qf, qs, size), gather(kf, ks, ksize),
                                               gather(vf, ks, ksize), bi)
                    scatter(br, qs, size, m, l, acc)

        slab = 2 * blk
        for s0 in range(0, seq, slab):
            rows = slice(s0, s0 + slab)
            m_all = jnp.maximum(jnp.maximum(ms[0, rows, :], ms[1, rows, :]), ms[2, rows, :])
            num = jnp.zeros((slab, blk), F32)
            den = jnp.zeros((slab, blk), F32)
            for br in range(3):
                w = jnp.exp2(ms[br, rows, :] - m_all)
                num = num + accs[br, rows, :] * w
                den = den + ls[br, rows, :] * w
            o_ref[rows, :] = (num / den).astype(o_ref.dtype)

    for gi, w in enumerate(POOL_WINDOWS):
        @pl.when(h == A_HEADS + gi)
        def _pool(gi=gi, w=w):
            pf = q_ref[...].astype(F32)
            row = lax.broadcasted_iota(jnp.int32, pf.shape, 0)
            l_idx = row % blk
            pos = N_RES * l_idx + row // blk
            s = pf
            k = 1
            while k < w:
                y = pltpu.roll(s, k * blk, axis=0)
                y1 = jnp.where(l_idx == 0, 0.0, pltpu.roll(y, 1, axis=0))
                s = s + jnp.where(row < k * blk, y1, y)
                k *= 2
            cnt = jnp.minimum(pos + 1, w).astype(F32)
            pooled = s / cnt - pf
            y = _dot(pooled.astype(BF16), pw_ref[gi].astype(BF16)) * ps_ref[...]
            o_ref[...] = y.astype(o_ref.dtype)


def _attn_pool(proj, pool_w, pool_scale):
    bsz, seq, _ = proj.shape
    n_steps = A_HEADS + N_POOL
    qmap = lambda b, h: (b, 0, jnp.where(h < A_HEADS, h, h + 2 * A_HEADS))
    kmap = lambda b, h: (b, 0, A_HEADS + jnp.minimum(h, A_HEADS - 1))
    vmap = lambda b, h: (b, 0, 2 * A_HEADS + jnp.minimum(h, A_HEADS - 1))
    f32s = pltpu.VMEM((seq, HEAD_DIM), F32)
    br3 = pltpu.VMEM((3, seq, HEAD_DIM), F32)
    return pl.pallas_call(
        _attn_pool_kernel,
        grid=(bsz, n_steps),
        in_specs=[
            pl.BlockSpec((None, seq, HEAD_DIM), qmap),
            pl.BlockSpec((None, seq, HEAD_DIM), kmap),
            pl.BlockSpec((None, seq, HEAD_DIM), vmap),
            pl.BlockSpec((N_POOL, HEAD_DIM, HEAD_DIM), lambda b, h: (0, 0, 0)),
            pl.BlockSpec((1, HEAD_DIM), lambda b, h: (0, jnp.maximum(h - A_HEADS, 0))),
        ],
        out_specs=pl.BlockSpec((None, seq, HEAD_DIM), lambda b, h: (b, 0, h)),
        out_shape=jax.ShapeDtypeStruct((bsz, seq, n_steps * HEAD_DIM), BF16),
        scratch_shapes=[f32s, f32s, f32s,
                        pltpu.VMEM((6, HEAD_DIM, 2 * HEAD_DIM), F32), br3, br3, br3],
        compiler_params=_cparams(("parallel", "arbitrary")),
        name="attn_pool",
    )(proj, proj, proj, pool_w, pool_scale.reshape(1, N_POOL * HEAD_DIM))


def _hgrn_kernel(q_ref, f_ref, i_ref, gt_ref, lb_ref, on_ref, o_ref,
                 qe_scr, o_scr, u_scr, d_scr, s_scr):
    seq = q_ref.shape[0]
    ck = HGRN_CHUNK
    grp = 4 * ck
    n_grp = seq // grp
    n_chunk = seq // ck
    lb = lb_ref[...]

    ri = lax.broadcasted_iota(jnp.int32, (grp, grp), 0)
    ci = lax.broadcasted_iota(jnp.int32, (grp, grp), 1)
    tri_mask = (ri // ck == ci // ck) & (ci <= ri)
    tri = tri_mask.astype(BF16)

    for gi in range(n_grp):
        rows = slice(gi * grp, (gi + 1) * grp)
        q = _silu(q_ref[rows, :].astype(F32))
        f = lb + (1.0 - lb) * jax.nn.sigmoid(f_ref[rows, :].astype(F32))
        k = 1.0 - f
        g = jnp.log(f)
        v = i_ref[rows, :]
        g_hi = g.astype(BF16)
        g_lo = (g - g_hi.astype(F32)).astype(BF16)
        b = _dot(tri, g_hi) + _dot(tri, g_lo)
        b3 = b.reshape(4, ck, HEAD_DIM)
        b_mid = b3[:, ck // 2 - 1:ck // 2, :]
        b_last = b3[:, ck - 1:ck, :]
        q3 = q.reshape(4, ck, HEAD_DIM)
        k3 = k.reshape(4, ck, HEAD_DIM)
        qs = (q3 * jnp.exp(b3 - b_mid)).reshape(grp, HEAD_DIM).astype(BF16)
        ks = (k3 * jnp.exp(b_mid - b3)).reshape(grp, HEAD_DIM).astype(BF16)
        att = jnp.where(tri_mask, _dot_nt(qs, ks), 0.0)
        o_scr[rows, :] = _dot(att.astype(BF16), v)
        qe_scr[rows, :] = (q * jnp.exp(b)).astype(BF16)
        kd = (k3 * jnp.exp(b_last - b3)).reshape(grp, HEAD_DIM).astype(BF16)
        for c in range(4):
            cr = slice(c * ck, (c + 1) * ck)
            u_scr[gi * 4 + c] = _dot_tn(v[cr, :], kd[cr, :])
        d_scr[gi * 4:(gi + 1) * 4, :] = jnp.exp(b_last).reshape(4, HEAD_DIM)

    st = jnp.zeros((HEAD_DIM, HEAD_DIM), F32)
    for c in range(n_chunk):
        s_scr[c] = st.astype(BF16)
        st = st * d_scr[c:c + 1, :] + u_scr[c]

    for gi in range(n_grp):
        rows = slice(gi * grp, (gi + 1) * grp)
        inter = jnp.concatenate(
            [_dot_nt(qe_scr[gi * grp + c * ck:gi * grp + (c + 1) * ck, :], s_scr[gi * 4 + c])
             for c in range(4)], axis=0)
        o = _rms(o_scr[rows, :] + inter, on_ref[...])
        o_ref[rows, :] = (o * _silu(gt_ref[rows, :].astype(F32))).astype(o_ref.dtype)


def _hgrn(proj, lb, out_norm):
    bsz, seq, _ = proj.shape
    nh = HGRN_HEADS
    n_chunk = seq // HGRN_CHUNK
    spec = lambda off: pl.BlockSpec((None, seq, HEAD_DIM), lambda b, h: (b, 0, off + h))
    vec = pl.BlockSpec((1, HEAD_DIM), lambda b, h: (0, h))
    return pl.pallas_call(
        _hgrn_kernel,
        grid=(bsz, nh),
        in_specs=[spec(0), spec(nh), spec(2 * nh), spec(3 * nh), vec, vec],
        out_specs=pl.BlockSpec((None, seq, HEAD_DIM), lambda b, h: (b, 0, h)),
        out_shape=jax.ShapeDtypeStruct((bsz, seq, nh * HEAD_DIM), BF16),
        scratch_shapes=[
            pltpu.VMEM((seq, HEAD_DIM), BF16),
            pltpu.VMEM((seq, HEAD_DIM), F32),
            pltpu.VMEM((n_chunk, HEAD_DIM, HEAD_DIM), F32),
            pltpu.VMEM((n_chunk, HEAD_DIM), F32),
            pltpu.VMEM((n_chunk, HEAD_DIM, HEAD_DIM), BF16),
        ],
        compiler_params=_cparams(("parallel", "arbitrary")),
        name="hgrn2",
    )(proj, proj, proj, proj, lb.reshape(1, -1), out_norm.reshape(1, -1))


def _router_kernel(x_ref, g_ref, w_ref, h_ref, meta_ref, cnt_ref, carry):
    i = pl.program_id(0)
    tm = x_ref.shape[0]

    @pl.when(i == 0)
    def _():
        carry[...] = jnp.zeros_like(carry)

    h = _rms(x_ref[...], g_ref[...])
    h_ref[...] = h
    h_hi = h.astype(BF16)
    h_lo = (h - h_hi.astype(F32)).astype(BF16)
    w = w_ref[...]
    w_hi = w.astype(BF16)
    w_lo = (w - w_hi.astype(F32)).astype(BF16)
    logits = _dot(h_hi, w_hi) + (_dot(h_hi, w_lo) + _dot(h_lo, w_hi))
    lane = lax.broadcasted_iota(jnp.int32, logits.shape, 1)
    logits = jnp.where(lane < N_EXPERTS, logits, NEG)
    m1 = jnp.max(logits, axis=-1, keepdims=True)
    i1 = jnp.min(jnp.where(logits == m1, lane, LANES), axis=-1, keepdims=True)
    rest = jnp.where(lane == i1, NEG, logits)
    m2 = jnp.max(rest, axis=-1, keepdims=True)
    i2 = jnp.min(jnp.where(rest == m2, lane, LANES), axis=-1, keepdims=True)
    e = jnp.exp(m2 - m1)
    g1 = 1.0 / (1.0 + e)
    g2 = e / (1.0 + e)
    oh1 = (lane == i1).astype(F32)
    oh2 = (lane == i2).astype(F32)
    cnt = oh1 + oh2
    ri = lax.broadcasted_iota(jnp.int32, (tm, tm), 0)
    ci = lax.broadcasted_iota(jnp.int32, (tm, tm), 1)
    before = (ci < ri).astype(BF16)
    prior = _dot(before, cnt.astype(BF16)) + carry[...]
    r1 = jnp.sum(oh1 * prior, axis=-1, keepdims=True)
    r2 = jnp.sum(oh2 * prior, axis=-1, keepdims=True)
    carry[...] += jnp.sum(cnt, axis=0, keepdims=True)
    cnt_ref[...] = carry[...]
    meta = jnp.where(lane == 0, i1.astype(F32), 0.0)
    meta = jnp.where(lane == 1, i2.astype(F32), meta)
    meta = jnp.where(lane == 2, r1, meta)
    meta = jnp.where(lane == 3, r2, meta)
    meta = jnp.where(lane == 4, g1, meta)
    meta = jnp.where(lane == 5, g2, meta)
    meta_ref[...] = meta


def _router(x, g, router_w, tm=512):
    t, d = x.shape
    w_pad = jnp.pad(router_w, ((0, 0), (0, LANES - router_w.shape[1])))
    return pl.pallas_call(
        _router_kernel,
        grid=(t // tm,),
        in_specs=[
            pl.BlockSpec((tm, d), lambda i: (i, 0)),
            pl.BlockSpec((1, d), lambda i: (0, 0)),
            pl.BlockSpec((d, LANES), lambda i: (0, 0)),
        ],
        out_specs=[
            pl.BlockSpec((tm, d), lambda i: (i, 0)),
            pl.BlockSpec((tm, LANES), lambda i: (i, 0)),
            pl.BlockSpec((1, LANES), lambda i: (0, 0)),
        ],
        out_shape=[
            jax.ShapeDtypeStruct((t, d), F32),
            jax.ShapeDtypeStruct((t, LANES), F32),
            jax.ShapeDtypeStruct((1, LANES), F32),
        ],
        scratch_shapes=[pltpu.VMEM((1, LANES), F32)],
        compiler_params=_cparams(("arbitrary",)),
        name="moe_router",
    )(x, g.reshape(1, d), w_pad)


def _row_copy(src, dst, s, d, sem):
    return pltpu.make_async_copy(src.at[pl.ds(s, 1)], dst.at[pl.ds(d, 1)], sem)


def _dispatch_kernel(dest_ref, h_ref, o_hbm, sem):
    tm = h_ref.shape[0]

    def start(t, carry):
        _row_copy(h_ref, o_hbm, t, dest_ref[0, 0, 2 * t], sem).start()
        _row_copy(h_ref, o_hbm, t, dest_ref[0, 0, 2 * t + 1], sem).start()
        return carry

    lax.fori_loop(0, tm, start, 0)

    def wait(t, carry):
        _row_copy(h_ref, o_hbm, 0, 0, sem).wait()
        _row_copy(h_ref, o_hbm, 0, 0, sem).wait()
        return carry

    lax.fori_loop(0, tm, wait, 0)


def _dispatch(h, dest, n_rows, tm=256):
    t, d = h.shape
    dest3 = dest.reshape(t // tm, 1, 2 * tm)
    return pl.pallas_call(
        _dispatch_kernel,
        grid=(t // tm,),
        in_specs=[
            pl.BlockSpec((1, 1, 2 * tm), lambda i: (i, 0, 0), memory_space=pltpu.SMEM),
            pl.BlockSpec((tm, d), lambda i: (i, 0)),
        ],
        out_specs=pl.BlockSpec(memory_space=pl.ANY),
        out_shape=jax.ShapeDtypeStruct((n_rows, d), F32),
        scratch_shapes=[pltpu.SemaphoreType.DMA(())],
        compiler_params=_cparams(("arbitrary",)),
        name="moe_dispatch",
    )(dest3, h)


def _moe_kernel(be_ref, bb_ref, nv_ref, na_ref, x_ref, wg_ref, wu_ref, wd_ref, o_ref, xb_scr):
    i = pl.program_id(0)
    j = pl.program_id(1)

    @pl.when(i < na_ref[0])
    def _():
        @pl.when(j == 0)
        def _():
            row = lax.broadcasted_iota(jnp.int32, x_ref.shape, 0)
            xb_scr[...] = jnp.where(row < nv_ref[i], x_ref[...], 0.0).astype(BF16)
            o_ref[...] = jnp.zeros_like(o_ref)

        xb = xb_scr[...]
        a = _dot(xb, wg_ref[0].astype(BF16))
        b = _dot(xb, wu_ref[0].astype(BF16))
        hid = (_silu(a) * b).astype(BF16)
        o_ref[...] += _dot(hid, wd_ref[0].astype(BF16))


def _moe_experts(x_rows, wg, wu, wd, blk_e, blk_idx, blk_valid, n_live, n_blocks):
    _, d = x_rows.shape
    dff = wg.shape[2]
    n_ff = dff // MOE_TF
    r = MOE_ROWS

    def ffi(i, j, na):
        return jnp.where(i < na[0], j, n_ff - 1)

    grid_spec = pltpu.PrefetchScalarGridSpec(
        num_scalar_prefetch=4,
        grid=(n_blocks, n_ff),
        in_specs=[
            pl.BlockSpec((r, d), lambda i, j, be, bb, nv, na: (bb[i], 0)),
            pl.BlockSpec((1, d, MOE_TF), lambda i, j, be, bb, nv, na: (be[i], 0, ffi(i, j, na))),
            pl.BlockSpec((1, d, MOE_TF), lambda i, j, be, bb, nv, na: (be[i], 0, ffi(i, j, na))),
            pl.BlockSpec((1, MOE_TF, d), lambda i, j, be, bb, nv, na: (be[i], ffi(i, j, na), 0)),
        ],
        out_specs=pl.BlockSpec((r, d), lambda i, j, be, bb, nv, na: (bb[i], 0)),
        scratch_shapes=[pltpu.VMEM((r, d), BF16)],
    )
    return pl.pallas_call(
        _moe_kernel,
        grid_spec=grid_spec,
        out_shape=jax.ShapeDtypeStruct(x_rows.shape, F32),
        compiler_params=_cparams(("arbitrary", "arbitrary")),
        name="moe_experts",
    )(blk_e, blk_idx, blk_valid, n_live, x_rows, wg, wu, wd)


def _combine_kernel(dest_ref, x_ref, meta_ref, nf_ref, y_hbm, o_ref, buf0, buf1, sem):
    tm = x_ref.shape[0]

    def start(t, carry):
        _row_copy(y_hbm, buf0, dest_ref[0, 0, 2 * t], t, sem).start()
        _row_copy(y_hbm, buf1, dest_ref[0, 0, 2 * t + 1], t, sem).start()
        return carry

    lax.fori_loop(0, tm, start, 0)

    def wait(t, carry):
        _row_copy(y_hbm, buf0, 0, 0, sem).wait()
        _row_copy(y_hbm, buf1, 0, 0, sem).wait()
        return carry

    lax.fori_loop(0, tm, wait, 0)

    meta = meta_ref[...]
    g1 = meta[:, 4:5]
    g2 = meta[:, 5:6]
    x = x_ref[...] + (g1 * buf0[...] + g2 * buf1[...])
    o_ref[...] = _rms(x, nf_ref[...])


def _combine(x, meta, norm_final, y_rows, dest, tm=256):
    t, d = x.shape
    dest3 = dest.reshape(t // tm, 1, 2 * tm)
    return pl.pallas_call(
        _combine_kernel,
        grid=(t // tm,),
        in_specs=[
            pl.BlockSpec((1, 1, 2 * tm), lambda i: (i, 0, 0), memory_space=pltpu.SMEM),
            pl.BlockSpec((tm, d), lambda i: (i, 0)),
            pl.BlockSpec((tm, LANES), lambda i: (i, 0)),
            pl.BlockSpec((1, d), lambda i: (0, 0)),
            pl.BlockSpec(memory_space=pl.ANY),
        ],
        out_specs=pl.BlockSpec((tm, d), lambda i: (i, 0)),
        out_shape=jax.ShapeDtypeStruct((t, d), F32),
        scratch_shapes=[pltpu.VMEM((tm, d), F32), pltpu.VMEM((tm, d), F32),
                        pltpu.SemaphoreType.DMA(())],
        compiler_params=_cparams(("arbitrary",)),
        name="moe_combine",
    )(dest3, x, meta, norm_final.reshape(1, d), y_rows)


def _moe_block_table(counts, t):
    r = MOE_ROWS
    n_blocks = (2 * t) // r + N_EXPERTS
    nblk = (counts + r - 1) // r
    ends = jnp.cumsum(nblk)
    starts = ends - nblk
    n_live = ends[-1]
    step = jnp.minimum(jnp.arange(n_blocks, dtype=jnp.int32), n_live - 1)
    blk_e = jnp.searchsorted(ends, step, side="right").astype(jnp.int32)
    local = step - starts[blk_e]
    blk_valid = jnp.clip(counts[blk_e] - local * r, 0, r)
    return (starts * r, blk_e, step, blk_valid.astype(jnp.int32),
            n_live.reshape(1).astype(jnp.int32), n_blocks)


def kernel(x, norm_mix, norm_ffn, w_in_even, pool_w, pool_scale, w_out_even, ffn_w_gate,
           ffn_w_up, ffn_w_down, w_in_odd, lower_bound_logits, hgrn_out_norm, w_out_odd,
           router_w, moe_w_gate, moe_w_up, moe_w_down, norm_final):
    bsz, seq, d = x.shape
    t = bsz * seq

    lb_all = jnp.cumsum(jax.nn.softmax(lower_bound_logits.astype(F32), axis=0), axis=0)
    lb_all = lb_all - lb_all[:1]

    n_in = w_in_even.shape[2]
    q_scale = math.log2(math.e) / math.sqrt(HEAD_DIM)
    col_scale = jnp.where(jnp.arange(n_in) < A_WIDTH, q_scale, 1.0).astype(F32)
    proj = _norm_matmul_res(x, norm_mix[0], w_in_even[0].astype(BF16), col_scale)
    mix = _attn_pool(proj.reshape(bsz, seq, -1), pool_w[0], pool_scale[0])
    xt = _matmul_unres(mix.reshape(t, -1), w_out_even[0].astype(BF16), x)
    xt = _swiglu_res(xt, norm_ffn[0], ffn_w_gate[0].astype(BF16), ffn_w_up[0].astype(BF16),
                     ffn_w_down[0].astype(BF16))

    proj = _norm_matmul(xt, norm_mix[1], w_in_odd[0].astype(BF16))
    o = _hgrn(proj.reshape(bsz, seq, -1), lb_all[1], hgrn_out_norm[0])
    xt = _matmul_res(o.reshape(t, -1), w_out_odd[0].astype(BF16), xt)

    h, meta, cnt = _router(xt, norm_ffn[1], router_w[0])
    counts = cnt[0, :N_EXPERTS].astype(jnp.int32)
    row0, blk_e, blk_idx, blk_valid, n_live, n_blocks = _moe_block_table(counts, t)
    dest = (row0[meta[:, 0:2].astype(jnp.int32)] + meta[:, 2:4].astype(jnp.int32)).reshape(-1)
    x_rows = _dispatch(h, dest, n_blocks * MOE_ROWS)
    y_rows = _moe_experts(x_rows, moe_w_gate[0], moe_w_up[0], moe_w_down[0],
                          blk_e, blk_idx, blk_valid, n_live, n_blocks)
    out = _combine(xt, meta, norm_final, y_rows, dest)
    return out.reshape(bsz, seq, d)
```

```python
import math

import jax
import jax.numpy as jnp
from jax import lax
from jax.experimental import pallas as pl
from jax.experimental.pallas import tpu as pltpu

F32 = jnp.float32
BF16 = jnp.bfloat16

D_MODEL = 2048
HEAD_DIM = 128
A_HEADS = 12
A_WIDTH = A_HEADS * HEAD_DIM
N_POOL = 4
POOL_WINDOWS = (2, 4, 8, 16)
HGRN_HEADS = 16
HGRN_CHUNK = 64
N_EXPERTS = 8
EPS = 1e-6
NEG = -1e30

LANES = 128
VMEM_LIMIT = 56 * 1024 * 1024

MOE_ROWS = 1024
MOE_TF = 256
HI16 = 0xFFFF0000


def _pack_bf16_pair(x):
    n = x.shape[1] // 2
    bits = lax.bitcast_convert_type(x.astype(BF16).astype(F32), jnp.uint32)
    return (bits[:, n:] & jnp.uint32(HI16)) | (bits[:, :n] >> 16)


def _unpack_bf16_pair(p):
    lo = lax.bitcast_convert_type(p << 16, F32).astype(BF16)
    hi = lax.bitcast_convert_type(p & jnp.uint32(HI16), F32).astype(BF16)
    return lo, hi

N_RES = 16


def _cparams(sem, vmem=VMEM_LIMIT):
    return pltpu.CompilerParams(dimension_semantics=sem, vmem_limit_bytes=vmem)


def _rms(x, g):
    ms = jnp.mean(x * x, axis=-1, keepdims=True)
    return x * lax.rsqrt(ms + EPS) * g


def _silu(x):
    return x * jax.nn.sigmoid(x)


def _dot(a, b):
    return jnp.dot(a, b, preferred_element_type=F32)


def _dot_nt(a, b):
    return lax.dot_general(a, b, (((1,), (1,)), ((), ())), preferred_element_type=F32)


def _dot_tn(a, b):
    return lax.dot_general(a, b, (((0,), (0,)), ((), ())), preferred_element_type=F32)


def _norm_matmul_kernel(x_ref, g_ref, w_ref, o_ref, h_scr):
    @pl.when(pl.program_id(1) == 0)
    def _():
        h_scr[...] = _rms(x_ref[...], g_ref[...]).astype(BF16)

    o_ref[...] = _dot(h_scr[...], w_ref[...]).astype(o_ref.dtype)


def _norm_matmul(x, g, w, tm=1024, tn=1024):
    m, d = x.shape
    n = w.shape[1]
    return pl.pallas_call(
        _norm_matmul_kernel,
        grid=(m // tm, n // tn),
        in_specs=[
            pl.BlockSpec((tm, d), lambda i, j: (i, 0)),
            pl.BlockSpec((1, d), lambda i, j: (0, 0)),
            pl.BlockSpec((d, tn), lambda i, j: (0, j)),
        ],
        out_specs=pl.BlockSpec((tm, tn), lambda i, j: (i, j)),
        out_shape=jax.ShapeDtypeStruct((m, n), BF16),
        scratch_shapes=[pltpu.VMEM((tm, d), BF16)],
        compiler_params=_cparams(("parallel", "arbitrary")),
        name="norm_matmul",
    )(x, g.reshape(1, d), w)


def _norm_matmul_res_kernel(x_ref, g_ref, w_ref, cs_ref, o_ref, h_scr):
    rows = x_ref.shape[0]
    d = g_ref.shape[1]

    @pl.when(pl.program_id(1) == 0)
    def _():
        for r in range(x_ref.shape[1] // d):
            h_scr[r * rows:(r + 1) * rows, :] = _rms(
                x_ref[:, r * d:(r + 1) * d], g_ref[...]).astype(BF16)

    o_ref[...] = (_dot(h_scr[...], w_ref[...]) * cs_ref[...]).astype(o_ref.dtype)


def _norm_matmul_res(x, g, w, col_scale, n_res_tile=8, tn=1024):
    bsz, seq, d = x.shape
    n = w.shape[1]
    rows = seq // N_RES
    tm = n_res_tile * rows
    halves = N_RES // n_res_tile
    xv = x.reshape(bsz, rows, N_RES * d)
    return pl.pallas_call(
        _norm_matmul_res_kernel,
        grid=(bsz * halves, n // tn),
        in_specs=[
            pl.BlockSpec((None, rows, n_res_tile * d), lambda i, j: (i // halves, 0, i % halves)),
            pl.BlockSpec((1, d), lambda i, j: (0, 0)),
            pl.BlockSpec((d, tn), lambda i, j: (0, j)),
            pl.BlockSpec((1, tn), lambda i, j: (0, j)),
        ],
        out_specs=pl.BlockSpec((tm, tn), lambda i, j: (i, j)),
        out_shape=jax.ShapeDtypeStruct((bsz * seq, n), BF16),
        scratch_shapes=[pltpu.VMEM((tm, d), BF16)],
        compiler_params=_cparams(("parallel", "arbitrary")),
        name="norm_matmul_res",
    )(xv, g.reshape(1, d), w, col_scale.reshape(1, n))


def _matmul_res_kernel(a_ref, w_ref, r_ref, o_ref):
    o_ref[...] = r_ref[...] + _dot(a_ref[...], w_ref[...])


def _matmul_res(a, w, res, tm=1024, tn=1024):
    m, k = a.shape
    n = w.shape[1]
    return pl.pallas_call(
        _matmul_res_kernel,
        grid=(m // tm, n // tn),
        in_specs=[
            pl.BlockSpec((tm, k), lambda i, j: (i, 0)),
            pl.BlockSpec((k, tn), lambda i, j: (0, j)),
            pl.BlockSpec((tm, tn), lambda i, j: (i, j)),
        ],
        out_specs=pl.BlockSpec((tm, tn), lambda i, j: (i, j)),
        out_shape=jax.ShapeDtypeStruct((m, n), F32),
        compiler_params=_cparams(("parallel", "arbitrary")),
        name="matmul_res",
    )(a, w, res)


def _matmul_unres_kernel(a_ref, w_ref, r_ref, o_ref):
    rows = r_ref.shape[0]
    d = w_ref.shape[1]
    y = _dot(a_ref[...], w_ref[...])
    for r in range(r_ref.shape[1] // d):
        o_ref[:, r * d:(r + 1) * d] = r_ref[:, r * d:(r + 1) * d] + y[r * rows:(r + 1) * rows, :]


def _matmul_unres(a, w, res, n_res_tile=4):
    bsz, seq, d = res.shape
    k = a.shape[1]
    rows = seq // N_RES
    tm = n_res_tile * rows
    parts = N_RES // n_res_tile
    resv = res.reshape(bsz, rows, N_RES * d)
    io_spec = pl.BlockSpec((None, rows, n_res_tile * d), lambda i: (i // parts, 0, i % parts))
    out = pl.pallas_call(
        _matmul_unres_kernel,
        grid=(bsz * parts,),
        in_specs=[
            pl.BlockSpec((tm, k), lambda i: (i, 0)),
            pl.BlockSpec((k, d), lambda i: (0, 0)),
            io_spec,
        ],
        out_specs=io_spec,
        out_shape=jax.ShapeDtypeStruct(resv.shape, F32),
        compiler_params=_cparams(("parallel",)),
        name="matmul_unres",
    )(a, w, resv)
    return out.reshape(bsz * seq, d)


def _swiglu_kernel(x_ref, g_ref, wg_ref, wu_ref, wd_ref, o_ref, h_scr, acc_scr):
    j = pl.program_id(1)

    @pl.when(j == 0)
    def _():
        h_scr[...] = _rms(x_ref[...], g_ref[...]).astype(BF16)
        acc_scr[...] = jnp.zeros_like(acc_scr)

    h = h_scr[...]
    a = _dot(h, wg_ref[...])
    b = _dot(h, wu_ref[...])
    hid = (_silu(a) * b).astype(BF16)
    acc_scr[...] += _dot(hid, wd_ref[...])

    @pl.when(j == pl.num_programs(1) - 1)
    def _():
        o_ref[...] = x_ref[...] + acc_scr[...]


def _swiglu_res(x, g, wg, wu, wd, tm=512, tf=512):
    m, d = x.shape
    ff = wg.shape[1]
    return pl.pallas_call(
        _swiglu_kernel,
        grid=(m // tm, ff // tf),
        in_specs=[
            pl.BlockSpec((tm, d), lambda i, j: (i, 0)),
            pl.BlockSpec((1, d), lambda i, j: (0, 0)),
            pl.BlockSpec((d, tf), lambda i, j: (0, j)),
            pl.BlockSpec((d, tf), lambda i, j: (0, j)),
            pl.BlockSpec((tf, d), lambda i, j: (j, 0)),
        ],
        out_specs=pl.BlockSpec((tm, d), lambda i, j: (i, 0)),
        out_shape=jax.ShapeDtypeStruct((m, d), F32),
        scratch_shapes=[pltpu.VMEM((tm, d), BF16), pltpu.VMEM((tm, d), F32)],
        compiler_params=_cparams(("parallel", "arbitrary")),
        name="swiglu_res",
    )(x, g.reshape(1, d), wg, wu, wd)


def _attn_bias(bias):
    blk = HEAD_DIM
    iq = lax.broadcasted_iota(jnp.int32, (blk, 2 * blk), 0)
    jk = lax.broadcasted_iota(jnp.int32, (blk, 2 * blk), 1)

    def offs(idx, n_chunk, size):
        return n_chunk * (idx % size) + idx // size

    def put(i, valid):
        bias[i] = jnp.where(valid, 0.0, NEG)

    for i, n_chunk in enumerate((1, 4, 16)):
        size = blk // n_chunk
        lq = offs(iq, n_chunk, size)
        put(2 * i, (offs(jk, n_chunk, size) <= lq) & (jk < blk))
        dist = lq - (offs(jk, n_chunk, 2 * size) - blk)
        put(2 * i + 1, (dist >= 0) & (dist <= blk))


def _softmax_block(qb, kb, vb, bias):
    s = _dot_nt(qb, kb) + bias
    m = jnp.max(s, axis=-1, keepdims=True)
    p = jnp.exp2(s - m)
    l = jnp.sum(p, axis=-1, keepdims=True)
    acc = _dot(p.astype(BF16), vb)
    return m, l, acc


def _attn_pool_kernel(q_ref, k_ref, v_ref, pw_ref, ps_ref, o_ref,
                      qf, kf, vf, bias, ms, ls, accs):
    h = pl.program_id(1)
    seq = q_ref.shape[0]
    blk = HEAD_DIM
    n_blk = seq // blk

    @pl.when(h < A_HEADS)
    def _attention():
        qf[...] = q_ref[...].astype(F32)
        kf[...] = k_ref[...].astype(F32)
        vf[...] = v_ref[...].astype(F32)
        _attn_bias(bias)

        def gather(ref, starts, size):
            return jnp.concatenate([ref[s0:s0 + size, :] for s0 in starts], axis=0).astype(BF16)

        def scatter(br, starts, size, m, l, acc):
            mb = jnp.broadcast_to(m, acc.shape)
            lb = jnp.broadcast_to(l, acc.shape)
            for c, s0 in enumerate(starts):
                src = slice(c * size, (c + 1) * size)
                ms[br, s0:s0 + size, :] = mb[src]
                ls[br, s0:s0 + size, :] = lb[src]
                accs[br, s0:s0 + size, :] = acc[src]

        for br, n_chunk in enumerate((1, 4, 16)):
            size = blk // n_chunk
            for grp in range(N_RES // n_chunk):
                bases = [(grp + (N_RES // n_chunk) * c) * blk for c in range(n_chunk)]
                for n in range(n_chunk):
                    qs = [b0 + n * size for b0 in bases]
                    if n == 0:
                        ks, ksize, bi = qs, size, bias[2 * br][:, :blk]
                    else:
                        ks, ksize, bi = [s0 - size for s0 in qs], 2 * size, bias[2 * br + 1]
                    m, l, acc = _softmax_block(gather(qf, qs, size), gather(kf, ks, ksize),
                                               gather(vf, ks, ksize), bi)
                    scatter(br, qs, size, m, l, acc)

        slab = 2 * blk
        for s0 in range(0, seq, slab):
            rows = slice(s0, s0 + slab)
            m_all = jnp.maximum(jnp.maximum(ms[0, rows, :], ms[1, rows, :]), ms[2, rows, :])
            num = jnp.zeros((slab, blk), F32)
            den = jnp.zeros((slab, blk), F32)
            for br in range(3):
                w = jnp.exp2(ms[br, rows, :] - m_all)
                num = num + accs[br, rows, :] * w
                den = den + ls[br, rows, :] * w
            o_ref[rows, :] = (num / den).astype(o_ref.dtype)

    for gi, w in enumerate(POOL_WINDOWS):
        @pl.when(h == A_HEADS + gi)
        def _pool(gi=gi, w=w):
            pf = q_ref[...].astype(F32)
            row = lax.broadcasted_iota(jnp.int32, pf.shape, 0)
            l_idx = row % blk
            pos = N_RES * l_idx + row // blk
            s = pf
            k = 1
            while k < w:
                y = pltpu.roll(s, k * blk, axis=0)
                y1 = jnp.where(l_idx == 0, 0.0, pltpu.roll(y, 1, axis=0))
                s = s + jnp.where(row < k * blk, y1, y)
                k *= 2
            cnt = jnp.minimum(pos + 1, w).astype(F32)
            pooled = s / cnt - pf
            y = _dot(pooled.astype(BF16), pw_ref[gi].astype(BF16)) * ps_ref[...]
            o_ref[...] = y.astype(o_ref.dtype)


def _attn_pool(proj, pool_w, pool_scale):
    bsz, seq, _ = proj.shape
    n_steps = A_HEADS + N_POOL
    qmap = lambda b, h: (b, 0, jnp.where(h < A_HEADS, h, h + 2 * A_HEADS))
    kmap = lambda b, h: (b, 0, A_HEADS + jnp.minimum(h, A_HEADS - 1))
    vmap = lambda b, h: (b, 0, 2 * A_HEADS + jnp.minimum(h, A_HEADS - 1))
    f32s = pltpu.VMEM((seq, HEAD_DIM), F32)
    br3 = pltpu.VMEM((3, seq, HEAD_DIM), F32)
    return pl.pallas_call(
        _attn_pool_kernel,
        grid=(bsz, n_steps),
        in_specs=[
            pl.BlockSpec((None, seq, HEAD_DIM), qmap),
            pl.BlockSpec((None, seq, HEAD_DIM), kmap),
            pl.BlockSpec((None, seq, HEAD_DIM), vmap),
            pl.BlockSpec((N_POOL, HEAD_DIM, HEAD_DIM), lambda b, h: (0, 0, 0)),
            pl.BlockSpec((1, HEAD_DIM), lambda b, h: (0, jnp.maximum(h - A_HEADS, 0))),
        ],
        out_specs=pl.BlockSpec((None, seq, HEAD_DIM), lambda b, h: (b, 0, h)),
        out_shape=jax.ShapeDtypeStruct((bsz, seq, n_steps * HEAD_DIM), BF16),
        scratch_shapes=[f32s, f32s, f32s,
                        pltpu.VMEM((6, HEAD_DIM, 2 * HEAD_DIM), F32), br3, br3, br3],
        compiler_params=_cparams(("parallel", "arbitrary")),
        name="attn_pool",
    )(proj, proj, proj, pool_w, pool_scale.reshape(1, N_POOL * HEAD_DIM))


def _hgrn_kernel(q_ref, f_ref, i_ref, gt_ref, lb_ref, on_ref, o_ref,
                 qe_scr, o_scr, u_scr, d_scr, s_scr):
    seq = q_ref.shape[0]
    ck = HGRN_CHUNK
    grp = 4 * ck
    n_grp = seq // grp
    n_chunk = seq // ck
    lb = lb_ref[...]

    ri = lax.broadcasted_iota(jnp.int32, (grp, grp), 0)
    ci = lax.broadcasted_iota(jnp.int32, (grp, grp), 1)
    tri_mask = (ri // ck == ci // ck) & (ci <= ri)
    tri = tri_mask.astype(BF16)

    for gi in range(n_grp):
        rows = slice(gi * grp, (gi + 1) * grp)
        q = _silu(q_ref[rows, :].astype(F32))
        f = lb + (1.0 - lb) * jax.nn.sigmoid(f_ref[rows, :].astype(F32))
        k = 1.0 - f
        g = jnp.log(f)
        v = i_ref[rows, :]
        g_hi = g.astype(BF16)
        g_lo = (g - g_hi.astype(F32)).astype(BF16)
        b = _dot(tri, g_hi) + _dot(tri, g_lo)
        b3 = b.reshape(4, ck, HEAD_DIM)
        b_mid = b3[:, ck // 2 - 1:ck // 2, :]
        b_last = b3[:, ck - 1:ck, :]
        q3 = q.reshape(4, ck, HEAD_DIM)
        k3 = k.reshape(4, ck, HEAD_DIM)
        qs = (q3 * jnp.exp(b3 - b_mid)).reshape(grp, HEAD_DIM).astype(BF16)
        ks = (k3 * jnp.exp(b_mid - b3)).reshape(grp, HEAD_DIM).astype(BF16)
        att = jnp.where(tri_mask, _dot_nt(qs, ks), 0.0)
        o_scr[rows, :] = _dot(att.astype(BF16), v)
        qe_scr[rows, :] = (q * jnp.exp(b)).astype(BF16)
        kd = (k3 * jnp.exp(b_last - b3)).reshape(grp, HEAD_DIM).astype(BF16)
        for c in range(4):
            cr = slice(c * ck, (c + 1) * ck)
            u_scr[gi * 4 + c] = _dot_tn(v[cr, :], kd[cr, :])
        d_scr[gi * 4:(gi + 1) * 4, :] = jnp.exp(b_last).reshape(4, HEAD_DIM)

    st = jnp.zeros((HEAD_DIM, HEAD_DIM), F32)
    for c in range(n_chunk):
        s_scr[c] = st.astype(BF16)
        st = st * d_scr[c:c + 1, :] + u_scr[c]

    for gi in range(n_grp):
        rows = slice(gi * grp, (gi + 1) * grp)
        inter = jnp.concatenate(
            [_dot_nt(qe_scr[gi * grp + c * ck:gi * grp + (c + 1) * ck, :], s_scr[gi * 4 + c])
             for c in range(4)], axis=0)
        o = _rms(o_scr[rows, :] + inter, on_ref[...])
        o_ref[rows, :] = (o * _silu(gt_ref[rows, :].astype(F32))).astype(o_ref.dtype)


def _hgrn(proj, lb, out_norm):
    bsz, seq, _ = proj.shape
    nh = HGRN_HEADS
    n_chunk = seq // HGRN_CHUNK
    spec = lambda off: pl.BlockSpec((None, seq, HEAD_DIM), lambda b, h: (b, 0, off + h))
    vec = pl.BlockSpec((1, HEAD_DIM), lambda b, h: (0, h))
    return pl.pallas_call(
        _hgrn_kernel,
        grid=(bsz, nh),
        in_specs=[spec(0), spec(nh), spec(2 * nh), spec(3 * nh), vec, vec],
        out_specs=pl.BlockSpec((None, seq, HEAD_DIM), lambda b, h: (b, 0, h)),
        out_shape=jax.ShapeDtypeStruct((bsz, seq, nh * HEAD_DIM), BF16),
        scratch_shapes=[
            pltpu.VMEM((seq, HEAD_DIM), BF16),
            pltpu.VMEM((seq, HEAD_DIM), F32),
            pltpu.VMEM((n_chunk, HEAD_DIM, HEAD_DIM), F32),
            pltpu.VMEM((n_chunk, HEAD_DIM), F32),
            pltpu.VMEM((n_chunk, HEAD_DIM, HEAD_DIM), BF16),
        ],
        compiler_params=_cparams(("parallel", "arbitrary")),
        name="hgrn2",
    )(proj, proj, proj, proj, lb.reshape(1, -1), out_norm.reshape(1, -1))


def _router_kernel(x_ref, g_ref, w_ref, h_ref, meta_ref, cnt_ref, carry):
    i = pl.program_id(0)
    tm = x_ref.shape[0]

    @pl.when(i == 0)
    def _():
        carry[...] = jnp.zeros_like(carry)

    h = _rms(x_ref[...], g_ref[...])
    h_ref[...] = _pack_bf16_pair(h)
    h_hi = h.astype(BF16)
    h_lo = (h - h_hi.astype(F32)).astype(BF16)
    w = w_ref[...]
    w_hi = w.astype(BF16)
    w_lo = (w - w_hi.astype(F32)).astype(BF16)
    logits = _dot(h_hi, w_hi) + (_dot(h_hi, w_lo) + _dot(h_lo, w_hi))
    lane = lax.broadcasted_iota(jnp.int32, logits.shape, 1)
    logits = jnp.where(lane < N_EXPERTS, logits, NEG)
    m1 = jnp.max(logits, axis=-1, keepdims=True)
    i1 = jnp.min(jnp.where(logits == m1, lane, LANES), axis=-1, keepdims=True)
    rest = jnp.where(lane == i1, NEG, logits)
    m2 = jnp.max(rest, axis=-1, keepdims=True)
    i2 = jnp.min(jnp.where(rest == m2, lane, LANES), axis=-1, keepdims=True)
    e = jnp.exp(m2 - m1)
    g1 = 1.0 / (1.0 + e)
    g2 = e / (1.0 + e)
    oh1 = (lane == i1).astype(F32)
    oh2 = (lane == i2).astype(F32)
    cnt = oh1 + oh2
    ri = lax.broadcasted_iota(jnp.int32, (tm, tm), 0)
    ci = lax.broadcasted_iota(jnp.int32, (tm, tm), 1)
    before = (ci < ri).astype(BF16)
    prior = _dot(before, cnt.astype(BF16)) + carry[...]
    r1 = jnp.sum(oh1 * prior, axis=-1, keepdims=True)
    r2 = jnp.sum(oh2 * prior, axis=-1, keepdims=True)
    carry[...] += jnp.sum(cnt, axis=0, keepdims=True)
    cnt_ref[...] = carry[...]
    meta = jnp.where(lane == 0, i1.astype(F32), 0.0)
    meta = jnp.where(lane == 1, i2.astype(F32), meta)
    meta = jnp.where(lane == 2, r1, meta)
    meta = jnp.where(lane == 3, r2, meta)
    meta = jnp.where(lane == 4, g1, meta)
    meta = jnp.where(lane == 5, g2, meta)
    meta_ref[...] = meta


def _router(x, g, router_w, tm=512):
    t, d = x.shape
    w_pad = jnp.pad(router_w, ((0, 0), (0, LANES - router_w.shape[1])))
    return pl.pallas_call(
        _router_kernel,
        grid=(t // tm,),
        in_specs=[
            pl.BlockSpec((tm, d), lambda i: (i, 0)),
            pl.BlockSpec((1, d), lambda i: (0, 0)),
            pl.BlockSpec((d, LANES), lambda i: (0, 0)),
        ],
        out_specs=[
            pl.BlockSpec((tm, d // 2), lambda i: (i, 0)),
            pl.BlockSpec((tm, LANES), lambda i: (i, 0)),
            pl.BlockSpec((1, LANES), lambda i: (0, 0)),
        ],
        out_shape=[
            jax.ShapeDtypeStruct((t, d // 2), jnp.uint32),
            jax.ShapeDtypeStruct((t, LANES), F32),
            jax.ShapeDtypeStruct((1, LANES), F32),
        ],
        scratch_shapes=[pltpu.VMEM((1, LANES), F32)],
        compiler_params=_cparams(("arbitrary",)),
        name="moe_router",
    )(x, g.reshape(1, d), w_pad)


def _row_copy(src, dst, s, d, sem):
    return pltpu.make_async_copy(src.at[pl.ds(s, 1)], dst.at[pl.ds(d, 1)], sem)


def _dispatch_kernel(dest_ref, h_ref, o_hbm, sem):
    tm = h_ref.shape[0]

    def start(t, carry):
        _row_copy(h_ref, o_hbm, t, dest_ref[0, 0, 2 * t], sem).start(priority=0)
        _row_copy(h_ref, o_hbm, t, dest_ref[0, 0, 2 * t + 1], sem).start(priority=1)
        return carry

    lax.fori_loop(0, tm, start, 0, unroll=8)

    def wait(t, carry):
        _row_copy(h_ref, o_hbm, 0, 0, sem).wait()
        _row_copy(h_ref, o_hbm, 0, 0, sem).wait()
        return carry

    lax.fori_loop(0, tm, wait, 0, unroll=8)


def _dispatch(h, dest, n_rows, tm=256):
    t, d = h.shape
    dest3 = dest.reshape(t // tm, 1, 2 * tm)
    return pl.pallas_call(
        _dispatch_kernel,
        grid=(t // tm,),
        in_specs=[
            pl.BlockSpec((1, 1, 2 * tm), lambda i: (i, 0, 0), memory_space=pltpu.SMEM),
            pl.BlockSpec((tm, d), lambda i: (i, 0)),
        ],
        out_specs=pl.BlockSpec(memory_space=pl.ANY),
        out_shape=jax.ShapeDtypeStruct((n_rows, d), h.dtype),
        scratch_shapes=[pltpu.SemaphoreType.DMA(())],
        compiler_params=_cparams(("arbitrary",)),
        name="moe_dispatch",
    )(dest3, h)


def _moe_kernel(be_ref, bb_ref, nv_ref, na_ref, x_ref, wg_ref, wu_ref, wd_ref, o_ref, xb_scr):
    i = pl.program_id(0)
    j = pl.program_id(1)

    n_valid = nv_ref[i]
    half = x_ref.shape[0] // 2

    @pl.when(i < na_ref[0])
    def _():
        @pl.when(j == 0)
        def _():
            row = lax.broadcasted_iota(jnp.int32, x_ref.shape, 0)
            lo, hi = _unpack_bf16_pair(jnp.where(row < n_valid, x_ref[...], jnp.uint32(0)))
            xb_scr[:, :lo.shape[1]] = lo
            xb_scr[:, lo.shape[1]:] = hi
            o_ref[...] = jnp.zeros_like(o_ref)

        def run(rows):
            xb = xb_scr[rows, :]
            a = _dot(xb, wg_ref[0].astype(BF16))
            b = _dot(xb, wu_ref[0].astype(BF16))
            hid = (_silu(a) * b).astype(BF16)
            o_ref[rows, :] += _dot(hid, wd_ref[0].astype(BF16))

        @pl.when(n_valid > half)
        def _():
            run(slice(None))

        @pl.when(n_valid <= half)
        def _():
            run(slice(0, half))


def _moe_experts(x_rows, wg, wu, wd, blk_e, blk_idx, blk_valid, n_live, n_blocks):
    dp = x_rows.shape[1]
    d = 2 * dp
    dff = wg.shape[2]
    n_ff = dff // MOE_TF
    r = MOE_ROWS

    def ffi(i, j, na):
        return jnp.where(i < na[0], j, n_ff - 1)

    grid_spec = pltpu.PrefetchScalarGridSpec(
        num_scalar_prefetch=4,
        grid=(n_blocks, n_ff),
        in_specs=[
            pl.BlockSpec((r, dp), lambda i, j, be, bb, nv, na: (bb[i], 0)),
            pl.BlockSpec((1, d, MOE_TF), lambda i, j, be, bb, nv, na: (be[i], 0, ffi(i, j, na))),
            pl.BlockSpec((1, d, MOE_TF), lambda i, j, be, bb, nv, na: (be[i], 0, ffi(i, j, na))),
            pl.BlockSpec((1, MOE_TF, d), lambda i, j, be, bb, nv, na: (be[i], ffi(i, j, na), 0)),
        ],
        out_specs=pl.BlockSpec((r, d), lambda i, j, be, bb, nv, na: (bb[i], 0)),
        scratch_shapes=[pltpu.VMEM((r, d), BF16)],
    )
    return pl.pallas_call(
        _moe_kernel,
        grid_spec=grid_spec,
        out_shape=jax.ShapeDtypeStruct((x_rows.shape[0], d), F32),
        compiler_params=_cparams(("arbitrary", "arbitrary")),
        name="moe_experts",
    )(blk_e, blk_idx, blk_valid, n_live, x_rows, wg, wu, wd)


def _combine_kernel(dest_ref, x_ref, meta_ref, nf_ref, y_hbm, o_ref, buf0, buf1, sem):
    tm = x_ref.shape[0]

    def start(t, carry):
        _row_copy(y_hbm, buf0, dest_ref[0, 0, 2 * t], t, sem).start(priority=0)
        _row_copy(y_hbm, buf1, dest_ref[0, 0, 2 * t + 1], t, sem).start(priority=1)
        return carry

    lax.fori_loop(0, tm, start, 0, unroll=8)

    def wait(t, carry):
        _row_copy(y_hbm, buf0, 0, 0, sem).wait()
        _row_copy(y_hbm, buf1, 0, 0, sem).wait()
        return carry

    lax.fori_loop(0, tm, wait, 0, unroll=8)

    meta = meta_ref[...]
    g1 = meta[:, 4:5]
    g2 = meta[:, 5:6]
    x = x_ref[...] + (g1 * buf0[...] + g2 * buf1[...])
    o_ref[...] = _rms(x, nf_ref[...])


def _combine(x, meta, norm_final, y_rows, dest, tm=256):
    t, d = x.shape
    dest3 = dest.reshape(t // tm, 1, 2 * tm)
    return pl.pallas_call(
        _combine_kernel,
        grid=(t // tm,),
        in_specs=[
            pl.BlockSpec((1, 1, 2 * tm), lambda i: (i, 0, 0), memory_space=pltpu.SMEM),
            pl.BlockSpec((tm, d), lambda i: (i, 0)),
            pl.BlockSpec((tm, LANES), lambda i: (i, 0)),
            pl.BlockSpec((1, d), lambda i: (0, 0)),
            pl.BlockSpec(memory_space=pl.ANY),
        ],
        out_specs=pl.BlockSpec((tm, d), lambda i: (i, 0)),
        out_shape=jax.ShapeDtypeStruct((t, d), F32),
        scratch_shapes=[pltpu.VMEM((tm, d), F32), pltpu.VMEM((tm, d), F32),
                        pltpu.SemaphoreType.DMA(())],
        compiler_params=_cparams(("arbitrary",)),
        name="moe_combine",
    )(dest3, x, meta, norm_final.reshape(1, d), y_rows)


def _moe_block_table(counts, t):
    r = MOE_ROWS
    n_blocks = (2 * t) // r + N_EXPERTS
    nblk = (counts + r - 1) // r
    ends = jnp.cumsum(nblk)
    starts = ends - nblk
    n_live = ends[-1]
    step = jnp.minimum(jnp.arange(n_blocks, dtype=jnp.int32), n_live - 1)
    blk_e = jnp.searchsorted(ends, step, side="right").astype(jnp.int32)
    local = step - starts[blk_e]
    blk_valid = jnp.clip(counts[blk_e] - local * r, 0, r)
    return (starts * r, blk_e, step, blk_valid.astype(jnp.int32),
            n_live.reshape(1).astype(jnp.int32), n_blocks)


def kernel(x, norm_mix, norm_ffn, w_in_even, pool_w, pool_scale, w_out_even, ffn_w_gate,
           ffn_w_up, ffn_w_down, w_in_odd, lower_bound_logits, hgrn_out_norm, w_out_odd,
           router_w, moe_w_gate, moe_w_up, moe_w_down, norm_final):
    bsz, seq, d = x.shape
    t = bsz * seq

    lb_all = jnp.cumsum(jax.nn.softmax(lower_bound_logits.astype(F32), axis=0), axis=0)
    lb_all = lb_all - lb_all[:1]

    n_in = w_in_even.shape[2]
    q_scale = math.log2(math.e) / math.sqrt(HEAD_DIM)
    col_scale = jnp.where(jnp.arange(n_in) < A_WIDTH, q_scale, 1.0).astype(F32)
    proj = _norm_matmul_res(x, norm_mix[0], w_in_even[0].astype(BF16), col_scale)
    mix = _attn_pool(proj.reshape(bsz, seq, -1), pool_w[0], pool_scale[0])
    xt = _matmul_unres(mix.reshape(t, -1), w_out_even[0].astype(BF16), x)
    xt = _swiglu_res(xt, norm_ffn[0], ffn_w_gate[0].astype(BF16), ffn_w_up[0].astype(BF16),
                     ffn_w_down[0].astype(BF16))

    proj = _norm_matmul(xt, norm_mix[1], w_in_odd[0].astype(BF16))
    o = _hgrn(proj.reshape(bsz, seq, -1), lb_all[1], hgrn_out_norm[0])
    xt = _matmul_res(o.reshape(t, -1), w_out_odd[0].astype(BF16), xt)

    h, meta, cnt = _router(xt, norm_ffn[1], router_w[0])
    counts = cnt[0, :N_EXPERTS].astype(jnp.int32)
    row0, blk_e, blk_idx, blk_valid, n_live, n_blocks = _moe_block_table(counts, t)
    dest = (row0[meta[:, 0:2].astype(jnp.int32)] + meta[:, 2:4].astype(jnp.int32)).reshape(-1)
    x_rows = _dispatch(h, dest, n_blocks * MOE_ROWS)
    y_rows = _moe_experts(x_rows, moe_w_gate[0], moe_w_up[0], moe_w_down[0],
                          blk_e, blk_idx, blk_valid, n_live, n_blocks)
    out = _combine(xt, meta, norm_final, y_rows, dest)
    return out.reshape(bsz, seq, d)
```

```python
import math

import jax
import jax.numpy as jnp
import numpy as np
from jax import lax
from jax.experimental import pallas as pl
from jax.experimental.pallas import tpu as pltpu

F32 = jnp.float32
BF16 = jnp.bfloat16

D_MODEL = 2048
HEAD_DIM = 128
A_HEADS = 12
A_WIDTH = A_HEADS * HEAD_DIM
N_POOL = 4
POOL_WINDOWS = (2, 4, 8, 16)
HGRN_HEADS = 16
HGRN_CHUNK = 64
N_EXPERTS = 8
EPS = 1e-6
NEG = -1e30

LANES = 128
VMEM_LIMIT = 56 * 1024 * 1024

MOE_ROWS = 1024
MOE_TF = 256
HI16 = 0xFFFF0000


def _pack_bf16_pair(x):
    n = x.shape[1] // 2
    bits = lax.bitcast_convert_type(x.astype(BF16).astype(F32), jnp.uint32)
    return (bits[:, n:] & jnp.uint32(HI16)) | (bits[:, :n] >> 16)


def _unpack_bf16_pair(p):
    lo = lax.bitcast_convert_type(p << 16, F32).astype(BF16)
    hi = lax.bitcast_convert_type(p & jnp.uint32(HI16), F32).astype(BF16)
    return lo, hi

N_RES = 16


def _cparams(sem, vmem=VMEM_LIMIT):
    return pltpu.CompilerParams(dimension_semantics=sem, vmem_limit_bytes=vmem)


def _rms(x, g):
    ms = jnp.mean(x * x, axis=-1, keepdims=True)
    return x * lax.rsqrt(ms + EPS) * g


def _silu(x):
    return x * jax.nn.sigmoid(x)


def _dot(a, b):
    return jnp.dot(a, b, preferred_element_type=F32)


def _dot_nt(a, b):
    return lax.dot_general(a, b, (((1,), (1,)), ((), ())), preferred_element_type=F32)


def _dot_tn(a, b):
    return lax.dot_general(a, b, (((0,), (0,)), ((), ())), preferred_element_type=F32)


def _norm_matmul_kernel(x_ref, g_ref, w_ref, o_ref, h_scr):
    @pl.when(pl.program_id(1) == 0)
    def _():
        h_scr[...] = _rms(x_ref[...], g_ref[...]).astype(BF16)

    o_ref[...] = _dot(h_scr[...], w_ref[...]).astype(o_ref.dtype)


def _norm_matmul(x, g, w, tm=1024, tn=1024):
    m, d = x.shape
    n = w.shape[1]
    return pl.pallas_call(
        _norm_matmul_kernel,
        grid=(m // tm, n // tn),
        in_specs=[
            pl.BlockSpec((tm, d), lambda i, j: (i, 0)),
            pl.BlockSpec((1, d), lambda i, j: (0, 0)),
            pl.BlockSpec((d, tn), lambda i, j: (0, j)),
        ],
        out_specs=pl.BlockSpec((tm, tn), lambda i, j: (i, j)),
        out_shape=jax.ShapeDtypeStruct((m, n), BF16),
        scratch_shapes=[pltpu.VMEM((tm, d), BF16)],
        compiler_params=_cparams(("parallel", "arbitrary")),
        name="norm_matmul",
    )(x, g.reshape(1, d), w)


def _norm_matmul_res_kernel(x_hbm, g_ref, w_ref, cs_ref, o_ref, xbuf, h_scr, sem):
    i = pl.program_id(0)
    n_res_tile, rows, _ = xbuf.shape[1:]
    halves = N_RES // n_res_tile

    def fetch(step, slot):
        b = step // halves
        r0 = (step % halves) * n_res_tile
        return [pltpu.make_async_copy(x_hbm.at[b, :, r0 + r, :], xbuf.at[slot, r], sem.at[slot, r])
                for r in range(n_res_tile)]

    @pl.when(pl.program_id(1) == 0)
    def _():
        slot = i % 2

        @pl.when(i == 0)
        def _():
            for c in fetch(i, slot):
                c.start()

        @pl.when(i + 1 < pl.num_programs(0))
        def _():
            for c in fetch(i + 1, 1 - slot):
                c.start()

        for r, c in enumerate(fetch(i, slot)):
            c.wait()
            h_scr[r * rows:(r + 1) * rows, :] = _rms(xbuf[slot, r], g_ref[...]).astype(BF16)

    o_ref[...] = (_dot(h_scr[...], w_ref[...]) * cs_ref[...]).astype(o_ref.dtype)


def _norm_matmul_res(x, g, w, col_scale, n_res_tile=8, tn=1024):
    bsz, seq, d = x.shape
    n = w.shape[1]
    rows = seq // N_RES
    tm = n_res_tile * rows
    halves = N_RES // n_res_tile
    return pl.pallas_call(
        _norm_matmul_res_kernel,
        grid=(bsz * halves, n // tn),
        in_specs=[
            pl.BlockSpec(memory_space=pl.ANY),
            pl.BlockSpec((1, d), lambda i, j: (0, 0)),
            pl.BlockSpec((d, tn), lambda i, j: (0, j)),
            pl.BlockSpec((1, tn), lambda i, j: (0, j)),
        ],
        out_specs=pl.BlockSpec((tm, tn), lambda i, j: (i, j)),
        out_shape=jax.ShapeDtypeStruct((bsz * seq, n), BF16),
        scratch_shapes=[pltpu.VMEM((2, n_res_tile, rows, d), F32), pltpu.VMEM((tm, d), BF16),
                        pltpu.SemaphoreType.DMA((2, n_res_tile))],
        compiler_params=_cparams(("arbitrary", "arbitrary")),
        name="norm_matmul_res",
    )(x.reshape(bsz, rows, N_RES, d), g.reshape(1, d), w, col_scale.reshape(1, n))


def _matmul_res_kernel(a_ref, w_ref, r_ref, o_ref):
    o_ref[...] = r_ref[...] + _dot(a_ref[...], w_ref[...])


def _matmul_res(a, w, res, tm=512, tn=2048):
    m, k = a.shape
    n = w.shape[1]
    return pl.pallas_call(
        _matmul_res_kernel,
        grid=(m // tm, n // tn),
        in_specs=[
            pl.BlockSpec((tm, k), lambda i, j: (i, 0)),
            pl.BlockSpec((k, tn), lambda i, j: (0, j)),
            pl.BlockSpec((tm, tn), lambda i, j: (i, j)),
        ],
        out_specs=pl.BlockSpec((tm, tn), lambda i, j: (i, j)),
        out_shape=jax.ShapeDtypeStruct((m, n), F32),
        compiler_params=_cparams(("parallel", "arbitrary")),
        name="matmul_res",
    )(a, w, res)


def _matmul_unres_kernel(a_ref, w_ref, r_hbm, o_hbm, rbuf, obuf, rsem, osem):
    i = pl.program_id(0)
    n_steps = pl.num_programs(0)
    n_res_tile, rows, _ = rbuf.shape[1:]
    parts = N_RES // n_res_tile

    def res_fetch(step, slot):
        b, r0 = step // parts, (step % parts) * n_res_tile
        return [pltpu.make_async_copy(r_hbm.at[b, :, r0 + r, :], rbuf.at[slot, r], rsem.at[slot, r])
                for r in range(n_res_tile)]

    def out_store(step, slot):
        b, r0 = step // parts, (step % parts) * n_res_tile
        return [pltpu.make_async_copy(obuf.at[slot, r], o_hbm.at[b, :, r0 + r, :], osem.at[slot, r])
                for r in range(n_res_tile)]

    slot = i % 2

    @pl.when(i == 0)
    def _():
        for c in res_fetch(i, slot):
            c.start()

    @pl.when(i + 1 < n_steps)
    def _():
        for c in res_fetch(i + 1, 1 - slot):
            c.start()

    y = _dot(a_ref[...], w_ref[...])

    @pl.when(i >= 2)
    def _():
        for c in out_store(i - 2, slot):
            c.wait()

    for r, c in enumerate(res_fetch(i, slot)):
        c.wait()
        obuf[slot, r] = rbuf[slot, r] + y[r * rows:(r + 1) * rows, :]
    for c in out_store(i, slot):
        c.start()

    @pl.when(i == n_steps - 1)
    def _():
        for c in out_store(i - 1, 1 - slot) + out_store(i, slot):
            c.wait()


def _matmul_unres(a, w, res, n_res_tile=4):
    bsz, seq, d = res.shape
    k = a.shape[1]
    rows = seq // N_RES
    tm = n_res_tile * rows
    parts = N_RES // n_res_tile
    assert bsz * parts >= 2
    buf = pltpu.VMEM((2, n_res_tile, rows, d), F32)
    out = pl.pallas_call(
        _matmul_unres_kernel,
        grid=(bsz * parts,),
        in_specs=[
            pl.BlockSpec((tm, k), lambda i: (i, 0)),
            pl.BlockSpec((k, d), lambda i: (0, 0)),
            pl.BlockSpec(memory_space=pl.ANY),
        ],
        out_specs=pl.BlockSpec(memory_space=pl.ANY),
        out_shape=jax.ShapeDtypeStruct((bsz, rows, N_RES, d), F32),
        scratch_shapes=[buf, buf, pltpu.SemaphoreType.DMA((2, n_res_tile)),
                        pltpu.SemaphoreType.DMA((2, n_res_tile))],
        compiler_params=_cparams(("arbitrary",)),
        name="matmul_unres",
    )(a, w, res.reshape(bsz, rows, N_RES, d))
    return out.reshape(bsz * seq, d)


def _swiglu_kernel(x_ref, g_ref, wg_ref, wu_ref, wd_ref, o_ref, h_scr, acc_scr):
    j = pl.program_id(1)

    @pl.when(j == 0)
    def _():
        h_scr[...] = _rms(x_ref[...], g_ref[...]).astype(BF16)
        acc_scr[...] = jnp.zeros_like(acc_scr)

    h = h_scr[...]
    a = _dot(h, wg_ref[...])
    b = _dot(h, wu_ref[...])
    hid = (_silu(a) * b).astype(BF16)
    acc_scr[...] += _dot(hid, wd_ref[...])

    @pl.when(j == pl.num_programs(1) - 1)
    def _():
        o_ref[...] = x_ref[...] + acc_scr[...]


def _swiglu_res(x, g, wg, wu, wd, tm=512, tf=512):
    m, d = x.shape
    ff = wg.shape[1]
    return pl.pallas_call(
        _swiglu_kernel,
        grid=(m // tm, ff // tf),
        in_specs=[
            pl.BlockSpec((tm, d), lambda i, j: (i, 0)),
            pl.BlockSpec((1, d), lambda i, j: (0, 0)),
            pl.BlockSpec((d, tf), lambda i, j: (0, j)),
            pl.BlockSpec((d, tf), lambda i, j: (0, j)),
            pl.BlockSpec((tf, d), lambda i, j: (j, 0)),
        ],
        out_specs=pl.BlockSpec((tm, d), lambda i, j: (i, 0)),
        out_shape=jax.ShapeDtypeStruct((m, d), F32),
        scratch_shapes=[pltpu.VMEM((tm, d), BF16), pltpu.VMEM((tm, d), F32)],
        compiler_params=_cparams(("parallel", "arbitrary")),
        name="swiglu_res",
    )(x, g.reshape(1, d), wg, wu, wd)


def _attn_bias_table():
    blk = HEAD_DIM
    iq = np.arange(blk)[:, None]
    jk = np.arange(2 * blk)[None, :]

    def offs(idx, n_chunk, size):
        return n_chunk * (idx % size) + idx // size

    masks = []
    for n_chunk in (1, 4, 16):
        size = blk // n_chunk
        lq = offs(iq, n_chunk, size)
        masks.append((offs(jk, n_chunk, size) <= lq) & (jk < blk))
        dist = lq - (offs(jk, n_chunk, 2 * size) - blk)
        masks.append((dist >= 0) & (dist <= blk))
    return np.where(np.stack(masks), 0.0, NEG).astype(np.float32)


def _softmax_block(qb, kb, vb, bias):
    s = _dot_nt(qb, kb) + bias
    m = jnp.max(s, axis=-1, keepdims=True)
    p = jnp.exp2(s - m)
    l = jnp.sum(p, axis=-1, keepdims=True)
    acc = _dot(p.astype(BF16), vb)
    return m, l, acc


def _attn_pool_kernel(q_ref, k_ref, v_ref, bias, pw_ref, ps_ref, o_ref,
                      qf, kf, vf, ms, ls, accs):
    h = pl.program_id(1)
    seq = q_ref.shape[0]
    blk = HEAD_DIM
    bf16_rows = 16

    @pl.when(h < A_HEADS)
    def _attention():
        qf[...] = q_ref[...].astype(F32)
        kf[...] = k_ref[...].astype(F32)
        vf[...] = v_ref[...].astype(F32)

        def gather(ref, ref_f32, starts, size):
            if all(v % bf16_rows == 0 for v in [size] + starts):
                return jnp.concatenate([ref[s0:s0 + size, :] for s0 in starts], axis=0)
            return jnp.concatenate([ref_f32[s0:s0 + size, :] for s0 in starts],
                                   axis=0).astype(BF16)

        def scatter(br, starts, size, m, l, acc):
            mb = jnp.broadcast_to(m, acc.shape)
            lb = jnp.broadcast_to(l, acc.shape)
            for c, s0 in enumerate(starts):
                src = slice(c * size, (c + 1) * size)
                ms[br, s0:s0 + size, :] = mb[src]
                ls[br, s0:s0 + size, :] = lb[src]
                accs[br, s0:s0 + size, :] = acc[src]

        for br, n_chunk in enumerate((1, 4, 16)):
            size = blk // n_chunk
            for grp in range(N_RES // n_chunk):
                bases = [(grp + (N_RES // n_chunk) * c) * blk for c in range(n_chunk)]
                for n in range(n_chunk):
                    qs = [b0 + n * size for b0 in bases]
                    if n == 0:
                        ks, ksize, bi = qs, size, bias[2 * br][:, :blk]
                    else:
                        ks, ksize, bi = [s0 - size for s0 in qs], 2 * size, bias[2 * br + 1]
                    m, l, acc = _softmax_block(gather(q_ref, qf, qs, size),
                                               gather(k_ref, kf, ks, ksize),
                                               gather(v_ref, vf, ks, ksize), bi)
                    scatter(br, qs, size, m, l, acc)

        slab = 2 * blk
        for s0 in range(0, seq, slab):
            rows = slice(s0, s0 + slab)
            m_all = jnp.maximum(jnp.maximum(ms[0, rows, :], ms[1, rows, :]), ms[2, rows, :])
            num = jnp.zeros((slab, blk), F32)
            den = jnp.zeros((slab, blk), F32)
            for br in range(3):
                w = jnp.exp2(ms[br, rows, :] - m_all)
                num = num + accs[br, rows, :] * w
                den = den + ls[br, rows, :] * w
            o_ref[rows, :] = (num / den).astype(o_ref.dtype)

    for gi, w in enumerate(POOL_WINDOWS):
        @pl.when(h == A_HEADS + gi)
        def _pool(gi=gi, w=w):
            pf = q_ref[...].astype(F32)
            row = lax.broadcasted_iota(jnp.int32, pf.shape, 0)
            l_idx = row % blk
            pos = N_RES * l_idx + row // blk
            s = pf
            k = 1
            while k < w:
                y = pltpu.roll(s, k * blk, axis=0)
                y1 = jnp.where(l_idx == 0, 0.0, pltpu.roll(y, 1, axis=0))
                s = s + jnp.where(row < k * blk, y1, y)
                k *= 2
            cnt = jnp.minimum(pos + 1, w).astype(F32)
            pooled = s / cnt - pf
            y = _dot(pooled.astype(BF16), pw_ref[gi].astype(BF16)) * ps_ref[...]
            o_ref[...] = y.astype(o_ref.dtype)


def _attn_pool(proj, pool_w, pool_scale):
    bsz, seq, _ = proj.shape
    n_steps = A_HEADS + N_POOL
    qmap = lambda b, h: (b, 0, jnp.where(h < A_HEADS, h, h + 2 * A_HEADS))
    kmap = lambda b, h: (b, 0, A_HEADS + jnp.minimum(h, A_HEADS - 1))
    vmap = lambda b, h: (b, 0, 2 * A_HEADS + jnp.minimum(h, A_HEADS - 1))
    f32s = pltpu.VMEM((seq, HEAD_DIM), F32)
    br3 = pltpu.VMEM((3, seq, HEAD_DIM), F32)
    bias = jnp.asarray(_attn_bias_table())
    return pl.pallas_call(
        _attn_pool_kernel,
        grid=(bsz, n_steps),
        in_specs=[
            pl.BlockSpec((None, seq, HEAD_DIM), qmap),
            pl.BlockSpec((None, seq, HEAD_DIM), kmap),
            pl.BlockSpec((None, seq, HEAD_DIM), vmap),
            pl.BlockSpec(bias.shape, lambda b, h: (0, 0, 0)),
            pl.BlockSpec((N_POOL, HEAD_DIM, HEAD_DIM), lambda b, h: (0, 0, 0)),
            pl.BlockSpec((1, HEAD_DIM), lambda b, h: (0, jnp.maximum(h - A_HEADS, 0))),
        ],
        out_specs=pl.BlockSpec((None, seq, HEAD_DIM), lambda b, h: (b, 0, h)),
        out_shape=jax.ShapeDtypeStruct((bsz, seq, n_steps * HEAD_DIM), BF16),
        scratch_shapes=[f32s, f32s, f32s, br3, br3, br3],
        compiler_params=_cparams(("parallel", "arbitrary")),
        name="attn_pool",
    )(proj, proj, proj, bias, pool_w, pool_scale.reshape(1, N_POOL * HEAD_DIM))


def _hgrn_kernel(q_ref, f_ref, i_ref, gt_ref, lb_ref, on_ref, o_ref,
                 qe_scr, o_scr, u_scr, d_scr, s_scr):
    seq = q_ref.shape[0]
    ck = HGRN_CHUNK
    grp = 4 * ck
    n_grp = seq // grp
    n_chunk = seq // ck
    lb = lb_ref[...]

    ri = lax.broadcasted_iota(jnp.int32, (grp, grp), 0)
    ci = lax.broadcasted_iota(jnp.int32, (grp, grp), 1)
    tri_mask = (ri // ck == ci // ck) & (ci <= ri)
    tri = tri_mask.astype(BF16)

    for gi in range(n_grp):
        rows = slice(gi * grp, (gi + 1) * grp)
        q = _silu(q_ref[rows, :].astype(F32))
        f = lb + (1.0 - lb) * jax.nn.sigmoid(f_ref[rows, :].astype(F32))
        k = 1.0 - f
        g = jnp.log(f)
        v = i_ref[rows, :]
        g_hi = g.astype(BF16)
        g_lo = (g - g_hi.astype(F32)).astype(BF16)
        b = _dot(tri, g_hi) + _dot(tri, g_lo)
        b3 = b.reshape(4, ck, HEAD_DIM)
        b_mid = b3[:, ck // 2 - 1:ck // 2, :]
        b_last = b3[:, ck - 1:ck, :]
        q3 = q.reshape(4, ck, HEAD_DIM)
        k3 = k.reshape(4, ck, HEAD_DIM)
        qs = (q3 * jnp.exp(b3 - b_mid)).reshape(grp, HEAD_DIM).astype(BF16)
        ks = (k3 * jnp.exp(b_mid - b3)).reshape(grp, HEAD_DIM).astype(BF16)
        att = jnp.where(tri_mask, _dot_nt(qs, ks), 0.0)
        o_scr[rows, :] = _dot(att.astype(BF16), v)
        qe_scr[rows, :] = (q * jnp.exp(b)).astype(BF16)
        kd = (k3 * jnp.exp(b_last - b3)).reshape(grp, HEAD_DIM).astype(BF16)
        for c in range(4):
            cr = slice(c * ck, (c + 1) * ck)
            u_scr[gi * 4 + c] = _dot_tn(v[cr, :], kd[cr, :])
        d_scr[gi * 4:(gi + 1) * 4, :] = jnp.exp(b_last).reshape(4, HEAD_DIM)

    st = jnp.zeros((HEAD_DIM, HEAD_DIM), F32)
    for c in range(n_chunk):
        s_scr[c] = st.astype(BF16)
        st = st * d_scr[c:c + 1, :] + u_scr[c]

    for gi in range(n_grp):
        rows = slice(gi * grp, (gi + 1) * grp)
        inter = jnp.concatenate(
            [_dot_nt(qe_scr[gi * grp + c * ck:gi * grp + (c + 1) * ck, :], s_scr[gi * 4 + c])
             for c in range(4)], axis=0)
        o = _rms(o_scr[rows, :] + inter, on_ref[...])
        o_ref[rows, :] = (o * _silu(gt_ref[rows, :].astype(F32))).astype(o_ref.dtype)


def _hgrn(proj, lb, out_norm):
    bsz, seq, _ = proj.shape
    nh = HGRN_HEADS
    n_chunk = seq // HGRN_CHUNK
    spec = lambda off: pl.BlockSpec((None, seq, HEAD_DIM), lambda b, h: (b, 0, off + h))
    vec = pl.BlockSpec((1, HEAD_DIM), lambda b, h: (0, h))
    return pl.pallas_call(
        _hgrn_kernel,
        grid=(bsz, nh),
        in_specs=[spec(0), spec(nh), spec(2 * nh), spec(3 * nh), vec, vec],
        out_specs=pl.BlockSpec((None, seq, HEAD_DIM), lambda b, h: (b, 0, h)),
        out_shape=jax.ShapeDtypeStruct((bsz, seq, nh * HEAD_DIM), BF16),
        scratch_shapes=[
            pltpu.VMEM((seq, HEAD_DIM), BF16),
            pltpu.VMEM((seq, HEAD_DIM), F32),
            pltpu.VMEM((n_chunk, HEAD_DIM, HEAD_DIM), F32),
            pltpu.VMEM((n_chunk, HEAD_DIM), F32),
            pltpu.VMEM((n_chunk, HEAD_DIM, HEAD_DIM), BF16),
        ],
        compiler_params=_cparams(("parallel", "arbitrary")),
        name="hgrn2",
    )(proj, proj, proj, proj, lb.reshape(1, -1), out_norm.reshape(1, -1))


def _router_kernel(x_ref, g_ref, w_ref, h_ref, meta_ref, cnt_ref, carry):
    i = pl.program_id(0)
    tm = x_ref.shape[0]

    @pl.when(i == 0)
    def _():
        carry[...] = jnp.zeros_like(carry)

    h = _rms(x_ref[...], g_ref[...])
    h_ref[...] = _pack_bf16_pair(h)
    h_hi = h.astype(BF16)
    h_lo = (h - h_hi.astype(F32)).astype(BF16)
    w = w_ref[...]
    w_hi = w.astype(BF16)
    w_lo = (w - w_hi.astype(F32)).astype(BF16)
    logits = _dot(h_hi, w_hi) + (_dot(h_hi, w_lo) + _dot(h_lo, w_hi))
    lane = lax.broadcasted_iota(jnp.int32, logits.shape, 1)
    logits = jnp.where(lane < N_EXPERTS, logits, NEG)
    m1 = jnp.max(logits, axis=-1, keepdims=True)
    i1 = jnp.min(jnp.where(logits == m1, lane, LANES), axis=-1, keepdims=True)
    rest = jnp.where(lane == i1, NEG, logits)
    m2 = jnp.max(rest, axis=-1, keepdims=True)
    i2 = jnp.min(jnp.where(rest == m2, lane, LANES), axis=-1, keepdims=True)
    e = jnp.exp(m2 - m1)
    g1 = 1.0 / (1.0 + e)
    g2 = e / (1.0 + e)
    oh1 = (lane == i1).astype(F32)
    oh2 = (lane == i2).astype(F32)
    cnt = oh1 + oh2
    ri = lax.broadcasted_iota(jnp.int32, (tm, tm), 0)
    ci = lax.broadcasted_iota(jnp.int32, (tm, tm), 1)
    before = (ci < ri).astype(BF16)
    prior = _dot(before, cnt.astype(BF16)) + carry[...]
    r1 = jnp.sum(oh1 * prior, axis=-1, keepdims=True)
    r2 = jnp.sum(oh2 * prior, axis=-1, keepdims=True)
    carry[...] += jnp.sum(cnt, axis=0, keepdims=True)
    cnt_ref[...] = carry[...]
    meta = jnp.where(lane == 0, i1.astype(F32), 0.0)
    meta = jnp.where(lane == 1, i2.astype(F32), meta)
    meta = jnp.where(lane == 2, r1, meta)
    meta = jnp.where(lane == 3, r2, meta)
    meta = jnp.where(lane == 4, g1, meta)
    meta = jnp.where(lane == 5, g2, meta)
    meta_ref[...] = meta


def _router(x, g, router_w, tm=512):
    t, d = x.shape
    w_pad = jnp.pad(router_w, ((0, 0), (0, LANES - router_w.shape[1])))
    return pl.pallas_call(
        _router_kernel,
        grid=(t // tm,),
        in_specs=[
            pl.BlockSpec((tm, d), lambda i: (i, 0)),
            pl.BlockSpec((1, d), lambda i: (0, 0)),
            pl.BlockSpec((d, LANES), lambda i: (0, 0)),
        ],
        out_specs=[
            pl.BlockSpec((tm, d // 2), lambda i: (i, 0)),
            pl.BlockSpec((tm, LANES), lambda i: (i, 0)),
            pl.BlockSpec((1, LANES), lambda i: (0, 0)),
        ],
        out_shape=[
            jax.ShapeDtypeStruct((t, d // 2), jnp.uint32),
            jax.ShapeDtypeStruct((t, LANES), F32),
            jax.ShapeDtypeStruct((1, LANES), F32),
        ],
        scratch_shapes=[pltpu.VMEM((1, LANES), F32)],
        compiler_params=_cparams(("arbitrary",)),
        name="moe_router",
    )(x, g.reshape(1, d), w_pad)


def _row_copy(src, dst, s, d, sem):
    return pltpu.make_async_copy(src.at[pl.ds(s, 1)], dst.at[pl.ds(d, 1)], sem)


def _dispatch_kernel(dest_ref, h_ref, o_hbm, sem):
    tm = h_ref.shape[0]

    def start(t, carry):
        _row_copy(h_ref, o_hbm, t, dest_ref[0, 0, 2 * t], sem).start(priority=0)
        _row_copy(h_ref, o_hbm, t, dest_ref[0, 0, 2 * t + 1], sem).start(priority=1)
        return carry

    lax.fori_loop(0, tm, start, 0, unroll=8)

    def wait(t, carry):
        _row_copy(h_ref, o_hbm, 0, 0, sem).wait()
        _row_copy(h_ref, o_hbm, 0, 0, sem).wait()
        return carry

    lax.fori_loop(0, tm, wait, 0, unroll=8)


def _dispatch(h, dest, n_rows, tm=256):
    t, d = h.shape
    dest3 = dest.reshape(t // tm, 1, 2 * tm)
    return pl.pallas_call(
        _dispatch_kernel,
        grid=(t // tm,),
        in_specs=[
            pl.BlockSpec((1, 1, 2 * tm), lambda i: (i, 0, 0), memory_space=pltpu.SMEM),
            pl.BlockSpec((tm, d), lambda i: (i, 0)),
        ],
        out_specs=pl.BlockSpec(memory_space=pl.ANY),
        out_shape=jax.ShapeDtypeStruct((n_rows, d), h.dtype),
        scratch_shapes=[pltpu.SemaphoreType.DMA(())],
        compiler_params=_cparams(("arbitrary",)),
        name="moe_dispatch",
    )(dest3, h)


def _moe_kernel(be_ref, bb_ref, nv_ref, na_ref, x_ref, wg_ref, wu_ref, wd_ref, o_ref, xb_scr):
    i = pl.program_id(0)
    j = pl.program_id(1)

    n_valid = nv_ref[i]

    @pl.when(i < na_ref[0])
    def _():
        @pl.when(j == 0)
        def _():
            row = lax.broadcasted_iota(jnp.int32, x_ref.shape, 0)
            lo, hi = _unpack_bf16_pair(jnp.where(row < n_valid, x_ref[...], jnp.uint32(0)))
            xb_scr[:, :lo.shape[1]] = lo
            xb_scr[:, lo.shape[1]:] = hi
            o_ref[...] = jnp.zeros_like(o_ref)

        def run(rows):
            xb = xb_scr[rows, :]
            a = _dot(xb, wg_ref[0].astype(BF16))
            b = _dot(xb, wu_ref[0].astype(BF16))
            hid = (_silu(a) * b).astype(BF16)
            o_ref[rows, :] += _dot(hid, wd_ref[0].astype(BF16))

        quarter = x_ref.shape[0] // 4
        for nq in range(1, 5):
            @pl.when((n_valid > (nq - 1) * quarter) & (n_valid <= nq * quarter))
            def _(nq=nq):
                run(slice(0, nq * quarter))


def _moe_experts(x_rows, wg, wu, wd, blk_e, blk_idx, blk_valid, n_live, n_blocks):
    dp = x_rows.shape[1]
    d = 2 * dp
    dff = wg.shape[2]
    n_ff = dff // MOE_TF
    r = MOE_ROWS

    def ffi(i, j, na):
        return jnp.where(i < na[0], j, n_ff - 1)

    grid_spec = pltpu.PrefetchScalarGridSpec(
        num_scalar_prefetch=4,
        grid=(n_blocks, n_ff),
        in_specs=[
            pl.BlockSpec((r, dp), lambda i, j, be, bb, nv, na: (bb[i], 0)),
            pl.BlockSpec((1, d, MOE_TF), lambda i, j, be, bb, nv, na: (be[i], 0, ffi(i, j, na))),
            pl.BlockSpec((1, d, MOE_TF), lambda i, j, be, bb, nv, na: (be[i], 0, ffi(i, j, na))),
            pl.BlockSpec((1, MOE_TF, d), lambda i, j, be, bb, nv, na: (be[i], ffi(i, j, na), 0)),
        ],
        out_specs=pl.BlockSpec((r, d), lambda i, j, be, bb, nv, na: (bb[i], 0)),
        scratch_shapes=[pltpu.VMEM((r, d), BF16)],
    )
    return pl.pallas_call(
        _moe_kernel,
        grid_spec=grid_spec,
        out_shape=jax.ShapeDtypeStruct((x_rows.shape[0], d), F32),
        compiler_params=_cparams(("arbitrary", "arbitrary")),
        name="moe_experts",
    )(blk_e, blk_idx, blk_valid, n_live, x_rows, wg, wu, wd)


def _combine_kernel(dest_ref, x_ref, meta_ref, nf_ref, y_hbm, o_ref, buf0, buf1, sem):
    tm = x_ref.shape[0]

    def start(t, carry):
        _row_copy(y_hbm, buf0, dest_ref[0, 0, 2 * t], t, sem).start(priority=0)
        _row_copy(y_hbm, buf1, dest_ref[0, 0, 2 * t + 1], t, sem).start(priority=1)
        return carry

    lax.fori_loop(0, tm, start, 0, unroll=8)

    def wait(t, carry):
        _row_copy(y_hbm, buf0, 0, 0, sem).wait()
        _row_copy(y_hbm, buf1, 0, 0, sem).wait()
        return carry

    lax.fori_loop(0, tm, wait, 0, unroll=8)

    meta = meta_ref[...]
    g1 = meta[:, 4:5]
    g2 = meta[:, 5:6]
    x = x_ref[...] + (g1 * buf0[...] + g2 * buf1[...])
    o_ref[...] = _rms(x, nf_ref[...])


def _combine(x, meta, norm_final, y_rows, dest, tm=256):
    t, d = x.shape
    dest3 = dest.reshape(t // tm, 1, 2 * tm)
    return pl.pallas_call(
        _combine_kernel,
        grid=(t // tm,),
        in_specs=[
            pl.BlockSpec((1, 1, 2 * tm), lambda i: (i, 0, 0), memory_space=pltpu.SMEM),
            pl.BlockSpec((tm, d), lambda i: (i, 0)),
            pl.BlockSpec((tm, LANES), lambda i: (i, 0)),
            pl.BlockSpec((1, d), lambda i: (0, 0)),
            pl.BlockSpec(memory_space=pl.ANY),
        ],
        out_specs=pl.BlockSpec((tm, d), lambda i: (i, 0)),
        out_shape=jax.ShapeDtypeStruct((t, d), F32),
        scratch_shapes=[pltpu.VMEM((tm, d), F32), pltpu.VMEM((tm, d), F32),
                        pltpu.SemaphoreType.DMA(())],
        compiler_params=_cparams(("arbitrary",)),
        name="moe_combine",
    )(dest3, x, meta, norm_final.reshape(1, d), y_rows)


def _moe_block_table(counts, t):
    r = MOE_ROWS
    n_blocks = (2 * t) // r + N_EXPERTS
    nblk = (counts + r - 1) // r
    ends = jnp.cumsum(nblk)
    starts = ends - nblk
    n_live = ends[-1]
    step = jnp.minimum(jnp.arange(n_blocks, dtype=jnp.int32), n_live - 1)
    blk_e = jnp.searchsorted(ends, step, side="right").astype(jnp.int32)
    local = step - starts[blk_e]
    blk_valid = jnp.clip(counts[blk_e] - local * r, 0, r)
    return (starts * r, blk_e, step, blk_valid.astype(jnp.int32),
            n_live.reshape(1).astype(jnp.int32), n_blocks)


def kernel(x, norm_mix, norm_ffn, w_in_even, pool_w, pool_scale, w_out_even, ffn_w_gate,
           ffn_w_up, ffn_w_down, w_in_odd, lower_bound_logits, hgrn_out_norm, w_out_odd,
           router_w, moe_w_gate, moe_w_up, moe_w_down, norm_final):
    bsz, seq, d = x.shape
    t = bsz * seq

    lb_all = jnp.cumsum(jax.nn.softmax(lower_bound_logits.astype(F32), axis=0), axis=0)
    lb_all = lb_all - lb_all[:1]

    n_in = w_in_even.shape[2]
    q_scale = math.log2(math.e) / math.sqrt(HEAD_DIM)
    col_scale = jnp.where(jnp.arange(n_in) < A_WIDTH, q_scale, 1.0).astype(F32)
    proj = _norm_matmul_res(x, norm_mix[0], w_in_even[0].astype(BF16), col_scale)
    mix = _attn_pool(proj.reshape(bsz, seq, -1), pool_w[0], pool_scale[0])
    xt = _matmul_unres(mix.reshape(t, -1), w_out_even[0].astype(BF16), x)
    xt = _swiglu_res(xt, norm_ffn[0], ffn_w_gate[0].astype(BF16), ffn_w_up[0].astype(BF16),
                     ffn_w_down[0].astype(BF16))

    proj = _norm_matmul(xt, norm_mix[1], w_in_odd[0].astype(BF16))
    o = _hgrn(proj.reshape(bsz, seq, -1), lb_all[1], hgrn_out_norm[0])
    xt = _matmul_res(o.reshape(t, -1), w_out_odd[0].astype(BF16), xt)

    h, meta, cnt = _router(xt, norm_ffn[1], router_w[0])
    counts = cnt[0, :N_EXPERTS].astype(jnp.int32)
    row0, blk_e, blk_idx, blk_valid, n_live, n_blocks = _moe_block_table(counts, t)
    dest = (row0[meta[:, 0:2].astype(jnp.int32)] + meta[:, 2:4].astype(jnp.int32)).reshape(-1)
    x_rows = _dispatch(h, dest, n_blocks * MOE_ROWS)
    y_rows = _moe_experts(x_rows, moe_w_gate[0], moe_w_up[0], moe_w_down[0],
                          blk_e, blk_idx, blk_valid, n_live, n_blocks)
    out = _combine(xt, meta, norm_final, y_rows, dest)
    return out.reshape(bsz, seq, d)
```

```python
import math

import jax
import jax.numpy as jnp
import numpy as np
from jax import lax
from jax.experimental import pallas as pl
from jax.experimental.pallas import tpu as pltpu

F32 = jnp.float32
BF16 = jnp.bfloat16

D_MODEL = 2048
HEAD_DIM = 128
A_HEADS = 12
A_WIDTH = A_HEADS * HEAD_DIM
N_POOL = 4
POOL_WINDOWS = (2, 4, 8, 16)
HGRN_HEADS = 16
HGRN_CHUNK = 64
N_EXPERTS = 8
EPS = 1e-6
NEG = -1e30

LANES = 128
VMEM_LIMIT = 56 * 1024 * 1024

MOE_ROWS = 1024
MOE_TF = 256
HI16 = 0xFFFF0000


def _pack_bf16_pair(x):
    n = x.shape[1] // 2
    bits = lax.bitcast_convert_type(x.astype(BF16).astype(F32), jnp.uint32)
    return (bits[:, n:] & jnp.uint32(HI16)) | (bits[:, :n] >> 16)


def _unpack_bf16_pair(p):
    lo = lax.bitcast_convert_type(p << 16, F32).astype(BF16)
    hi = lax.bitcast_convert_type(p & jnp.uint32(HI16), F32).astype(BF16)
    return lo, hi

N_RES = 16


def _cparams(sem, vmem=VMEM_LIMIT):
    return pltpu.CompilerParams(dimension_semantics=sem, vmem_limit_bytes=vmem)


def _rms(x, g):
    ms = jnp.mean(x * x, axis=-1, keepdims=True)
    return x * lax.rsqrt(ms + EPS) * g


def _silu(x):
    return x * jax.nn.sigmoid(x)


def _dot(a, b):
    return jnp.dot(a, b, preferred_element_type=F32)


def _dot_nt(a, b):
    return lax.dot_general(a, b, (((1,), (1,)), ((), ())), preferred_element_type=F32)


def _dot_tn(a, b):
    return lax.dot_general(a, b, (((0,), (0,)), ((), ())), preferred_element_type=F32)


def _norm_matmul_kernel(x_ref, g_ref, w_ref, o_ref, h_scr):
    @pl.when(pl.program_id(1) == 0)
    def _():
        h_scr[...] = _rms(x_ref[...], g_ref[...]).astype(BF16)

    o_ref[...] = _dot(h_scr[...], w_ref[...]).astype(o_ref.dtype)


def _norm_matmul(x, g, w, tm=1024, tn=1024):
    m, d = x.shape
    n = w.shape[1]
    return pl.pallas_call(
        _norm_matmul_kernel,
        grid=(m // tm, n // tn),
        in_specs=[
            pl.BlockSpec((tm, d), lambda i, j: (i, 0)),
            pl.BlockSpec((1, d), lambda i, j: (0, 0)),
            pl.BlockSpec((d, tn), lambda i, j: (0, j)),
        ],
        out_specs=pl.BlockSpec((tm, tn), lambda i, j: (i, j)),
        out_shape=jax.ShapeDtypeStruct((m, n), BF16),
        scratch_shapes=[pltpu.VMEM((tm, d), BF16)],
        compiler_params=_cparams(("parallel", "arbitrary")),
        name="norm_matmul",
    )(x, g.reshape(1, d), w)


def _norm_matmul_res_kernel(x_hbm, g_ref, w_ref, cs_ref, o_ref, xbuf, h_scr, sem):
    i = pl.program_id(0)
    n_res_tile, rows, _ = xbuf.shape[1:]
    halves = N_RES // n_res_tile

    def fetch(step, slot):
        b = step // halves
        r0 = (step % halves) * n_res_tile
        return [pltpu.make_async_copy(x_hbm.at[b, :, r0 + r, :], xbuf.at[slot, r], sem.at[slot, r])
                for r in range(n_res_tile)]

    @pl.when(pl.program_id(1) == 0)
    def _():
        slot = i % 2

        @pl.when(i == 0)
        def _():
            for c in fetch(i, slot):
                c.start()

        @pl.when(i + 1 < pl.num_programs(0))
        def _():
            for c in fetch(i + 1, 1 - slot):
                c.start()

        for r, c in enumerate(fetch(i, slot)):
            c.wait()
            h_scr[r * rows:(r + 1) * rows, :] = _rms(xbuf[slot, r], g_ref[...]).astype(BF16)

    o_ref[...] = (_dot(h_scr[...], w_ref[...]) * cs_ref[...]).astype(o_ref.dtype)


def _norm_matmul_res(x, g, w, col_scale, n_res_tile=8, tn=1024):
    bsz, seq, d = x.shape
    n = w.shape[1]
    rows = seq // N_RES
    tm = n_res_tile * rows
    halves = N_RES // n_res_tile
    return pl.pallas_call(
        _norm_matmul_res_kernel,
        grid=(bsz * halves, n // tn),
        in_specs=[
            pl.BlockSpec(memory_space=pl.ANY),
            pl.BlockSpec((1, d), lambda i, j: (0, 0)),
            pl.BlockSpec((d, tn), lambda i, j: (0, j)),
            pl.BlockSpec((1, tn), lambda i, j: (0, j)),
        ],
        out_specs=pl.BlockSpec((tm, tn), lambda i, j: (i, j)),
        out_shape=jax.ShapeDtypeStruct((bsz * seq, n), BF16),
        scratch_shapes=[pltpu.VMEM((2, n_res_tile, rows, d), F32), pltpu.VMEM((tm, d), BF16),
                        pltpu.SemaphoreType.DMA((2, n_res_tile))],
        compiler_params=_cparams(("arbitrary", "arbitrary")),
        name="norm_matmul_res",
    )(x.reshape(bsz, rows, N_RES, d), g.reshape(1, d), w, col_scale.reshape(1, n))


def _matmul_res_kernel(a_ref, w_ref, r_ref, o_ref):
    o_ref[...] = r_ref[...] + _dot(a_ref[...], w_ref[...])


def _matmul_res(a, w, res, tm=512, tn=2048):
    m, k = a.shape
    n = w.shape[1]
    return pl.pallas_call(
        _matmul_res_kernel,
        grid=(m // tm, n // tn),
        in_specs=[
            pl.BlockSpec((tm, k), lambda i, j: (i, 0)),
            pl.BlockSpec((k, tn), lambda i, j: (0, j)),
            pl.BlockSpec((tm, tn), lambda i, j: (i, j)),
        ],
        out_specs=pl.BlockSpec((tm, tn), lambda i, j: (i, j)),
        out_shape=jax.ShapeDtypeStruct((m, n), F32),
        compiler_params=_cparams(("parallel", "arbitrary")),
        name="matmul_res",
    )(a, w, res)


def _matmul_unres_kernel(a_ref, w_ref, r_hbm, o_hbm, rbuf, obuf, rsem, osem):
    i = pl.program_id(0)
    n_steps = pl.num_programs(0)
    n_res_tile, rows, _ = rbuf.shape[1:]
    parts = N_RES // n_res_tile

    def res_fetch(step, slot):
        b, r0 = step // parts, (step % parts) * n_res_tile
        return [pltpu.make_async_copy(r_hbm.at[b, :, r0 + r, :], rbuf.at[slot, r], rsem.at[slot, r])
                for r in range(n_res_tile)]

    def out_store(step, slot):
        b, r0 = step // parts, (step % parts) * n_res_tile
        return [pltpu.make_async_copy(obuf.at[slot, r], o_hbm.at[b, :, r0 + r, :], osem.at[slot, r])
                for r in range(n_res_tile)]

    slot = i % 2

    @pl.when(i == 0)
    def _():
        for c in res_fetch(i, slot):
            c.start()

    @pl.when(i + 1 < n_steps)
    def _():
        for c in res_fetch(i + 1, 1 - slot):
            c.start()

    y = _dot(a_ref[...], w_ref[...])

    @pl.when(i >= 2)
    def _():
        for c in out_store(i - 2, slot):
            c.wait()

    for r, c in enumerate(res_fetch(i, slot)):
        c.wait()
        obuf[slot, r] = rbuf[slot, r] + y[r * rows:(r + 1) * rows, :]
    for c in out_store(i, slot):
        c.start()

    @pl.when(i == n_steps - 1)
    def _():
        for c in out_store(i - 1, 1 - slot) + out_store(i, slot):
            c.wait()


def _matmul_unres(a, w, res, n_res_tile=4):
    bsz, seq, d = res.shape
    k = a.shape[1]
    rows = seq // N_RES
    tm = n_res_tile * rows
    parts = N_RES // n_res_tile
    assert bsz * parts >= 2
    buf = pltpu.VMEM((2, n_res_tile, rows, d), F32)
    out = pl.pallas_call(
        _matmul_unres_kernel,
        grid=(bsz * parts,),
        in_specs=[
            pl.BlockSpec((tm, k), lambda i: (i, 0)),
            pl.BlockSpec((k, d), lambda i: (0, 0)),
            pl.BlockSpec(memory_space=pl.ANY),
        ],
        out_specs=pl.BlockSpec(memory_space=pl.ANY),
        out_shape=jax.ShapeDtypeStruct((bsz, rows, N_RES, d), F32),
        scratch_shapes=[buf, buf, pltpu.SemaphoreType.DMA((2, n_res_tile)),
                        pltpu.SemaphoreType.DMA((2, n_res_tile))],
        compiler_params=_cparams(("arbitrary",)),
        name="matmul_unres",
    )(a, w, res.reshape(bsz, rows, N_RES, d))
    return out.reshape(bsz * seq, d)


def _swiglu_kernel(x_ref, g_ref, wg_ref, wu_ref, wd_ref, o_ref, h_scr, acc_scr):
    j = pl.program_id(1)

    @pl.when(j == 0)
    def _():
        h_scr[...] = _rms(x_ref[...], g_ref[...]).astype(BF16)
        acc_scr[...] = jnp.zeros_like(acc_scr)

    h = h_scr[...]
    a = _dot(h, wg_ref[...])
    b = _dot(h, wu_ref[...])
    hid = (_silu(a) * b).astype(BF16)
    acc_scr[...] += _dot(hid, wd_ref[...])

    @pl.when(j == pl.num_programs(1) - 1)
    def _():
        o_ref[...] = x_ref[...] + acc_scr[...]


def _swiglu_res(x, g, wg, wu, wd, tm=512, tf=512):
    m, d = x.shape
    ff = wg.shape[1]
    return pl.pallas_call(
        _swiglu_kernel,
        grid=(m // tm, ff // tf),
        in_specs=[
            pl.BlockSpec((tm, d), lambda i, j: (i, 0)),
            pl.BlockSpec((1, d), lambda i, j: (0, 0)),
            pl.BlockSpec((d, tf), lambda i, j: (0, j)),
            pl.BlockSpec((d, tf), lambda i, j: (0, j)),
            pl.BlockSpec((tf, d), lambda i, j: (j, 0)),
        ],
        out_specs=pl.BlockSpec((tm, d), lambda i, j: (i, 0)),
        out_shape=jax.ShapeDtypeStruct((m, d), F32),
        scratch_shapes=[pltpu.VMEM((tm, d), BF16), pltpu.VMEM((tm, d), F32)],
        compiler_params=_cparams(("parallel", "arbitrary")),
        name="swiglu_res",
    )(x, g.reshape(1, d), wg, wu, wd)


def _attn_bias_table():
    blk = HEAD_DIM
    iq = np.arange(blk)[:, None]
    jk = np.arange(2 * blk)[None, :]

    def offs(idx, n_chunk, size):
        return n_chunk * (idx % size) + idx // size

    masks = []
    for n_chunk in (1, 4, 16):
        size = blk // n_chunk
        lq = offs(iq, n_chunk, size)
        masks.append((offs(jk, n_chunk, size) <= lq) & (jk < blk))
        dist = lq - (offs(jk, n_chunk, 2 * size) - blk)
        masks.append((dist >= 0) & (dist <= blk))
    return np.where(np.stack(masks), 0.0, NEG).astype(np.float32)


def _attn_pool_kernel(q_ref, k_ref, v_ref, bias, pw_ref, ps_ref, o_ref,
                      qf, kf, vf, ms, ls, accs):
    h = pl.program_id(1)
    seq = q_ref.shape[0]
    blk = HEAD_DIM
    bf16_rows = 16

    @pl.when(h < A_HEADS)
    def _attention():
        qf[...] = q_ref[...].astype(F32)
        kf[...] = k_ref[...].astype(F32)
        vf[...] = v_ref[...].astype(F32)

        def gather(ref, ref_f32, starts, size):
            if all(v % bf16_rows == 0 for v in [size] + starts):
                return jnp.concatenate([ref[s0:s0 + size, :] for s0 in starts], axis=0)
            return jnp.concatenate([ref_f32[s0:s0 + size, :] for s0 in starts],
                                   axis=0).astype(BF16)

        def scatter(br, starts, size, m, l, acc):
            mb = jnp.broadcast_to(m, acc.shape)
            lb = jnp.broadcast_to(l, acc.shape)
            for c, s0 in enumerate(starts):
                src = slice(c * size, (c + 1) * size)
                ms[br, s0:s0 + size, :] = mb[src]
                ls[br, s0:s0 + size, :] = lb[src]
                accs[br, s0:s0 + size, :] = acc[src]

        kinds = {}
        for br, n_chunk in enumerate((1, 4, 16)):
            size = blk // n_chunk
            for grp in range(N_RES // n_chunk):
                bases = [(grp + (N_RES // n_chunk) * c) * blk for c in range(n_chunk)]
                for n in range(n_chunk):
                    qs = [b0 + n * size for b0 in bases]
                    if n == 0:
                        kinds.setdefault((2 * br, blk), []).append((br, size, qs, qs, size))
                    else:
                        ks = [s0 - size for s0 in qs]
                        kinds.setdefault((2 * br + 1, 2 * blk), []).append(
                            (br, size, qs, ks, 2 * size))

        batch = 4
        for (bias_idx, kw), blocks in kinds.items():
            bi = bias[bias_idx][:, :kw]
            for g0 in range(0, len(blocks), batch):
                group = blocks[g0:g0 + batch]
                s = jnp.concatenate(
                    [_dot_nt(gather(q_ref, qf, qs, size), gather(k_ref, kf, ks, ksize))
                     for _, size, qs, ks, ksize in group], axis=0)
                s = (s.reshape(len(group), blk, kw) + bi[None]).reshape(len(group) * blk, kw)
                m = jnp.max(s, axis=-1, keepdims=True)
                p = jnp.exp2(s - m)
                l = jnp.sum(p, axis=-1, keepdims=True)
                pb = p.astype(BF16)
                for g, (br, size, qs, ks, ksize) in enumerate(group):
                    rows = slice(g * blk, (g + 1) * blk)
                    acc = _dot(pb[rows], gather(v_ref, vf, ks, ksize))
                    scatter(br, qs, size, m[rows], l[rows], acc)

        slab = 2 * blk
        for s0 in range(0, seq, slab):
            rows = slice(s0, s0 + slab)
            m_all = jnp.maximum(jnp.maximum(ms[0, rows, :], ms[1, rows, :]), ms[2, rows, :])
            num = jnp.zeros((slab, blk), F32)
            den = jnp.zeros((slab, blk), F32)
            for br in range(3):
                w = jnp.exp2(ms[br, rows, :] - m_all)
                num = num + accs[br, rows, :] * w
                den = den + ls[br, rows, :] * w
            o_ref[rows, :] = (num / den).astype(o_ref.dtype)

    for gi, w in enumerate(POOL_WINDOWS):
        @pl.when(h == A_HEADS + gi)
        def _pool(gi=gi, w=w):
            pf = q_ref[...].astype(F32)
            row = lax.broadcasted_iota(jnp.int32, pf.shape, 0)
            l_idx = row % blk
            pos = N_RES * l_idx + row // blk
            s = pf
            k = 1
            while k < w:
                y = pltpu.roll(s, k * blk, axis=0)
                y1 = jnp.where(l_idx == 0, 0.0, pltpu.roll(y, 1, axis=0))
                s = s + jnp.where(row < k * blk, y1, y)
                k *= 2
            cnt = jnp.minimum(pos + 1, w).astype(F32)
            pooled = s / cnt - pf
            y = _dot(pooled.astype(BF16), pw_ref[gi].astype(BF16)) * ps_ref[...]
            o_ref[...] = y.astype(o_ref.dtype)


def _attn_pool(proj, pool_w, pool_scale):
    bsz, seq, _ = proj.shape
    n_steps = A_HEADS + N_POOL
    qmap = lambda b, h: (b, 0, jnp.where(h < A_HEADS, h, h + 2 * A_HEADS))
    kmap = lambda b, h: (b, 0, A_HEADS + jnp.minimum(h, A_HEADS - 1))
    vmap = lambda b, h: (b, 0, 2 * A_HEADS + jnp.minimum(h, A_HEADS - 1))
    f32s = pltpu.VMEM((seq, HEAD_DIM), F32)
    br3 = pltpu.VMEM((3, seq, HEAD_DIM), F32)
    bias = jnp.asarray(_attn_bias_table())
    return pl.pallas_call(
        _attn_pool_kernel,
        grid=(bsz, n_steps),
        in_specs=[
            pl.BlockSpec((None, seq, HEAD_DIM), qmap),
            pl.BlockSpec((None, seq, HEAD_DIM), kmap),
            pl.BlockSpec((None, seq, HEAD_DIM), vmap),
            pl.BlockSpec(bias.shape, lambda b, h: (0, 0, 0)),
            pl.BlockSpec((N_POOL, HEAD_DIM, HEAD_DIM), lambda b, h: (0, 0, 0)),
            pl.BlockSpec((1, HEAD_DIM), lambda b, h: (0, jnp.maximum(h - A_HEADS, 0))),
        ],
        out_specs=pl.BlockSpec((None, seq, HEAD_DIM), lambda b, h: (b, 0, h)),
        out_shape=jax.ShapeDtypeStruct((bsz, seq, n_steps * HEAD_DIM), BF16),
        scratch_shapes=[f32s, f32s, f32s, br3, br3, br3],
        compiler_params=_cparams(("parallel", "arbitrary")),
        name="attn_pool",
    )(proj, proj, proj, bias, pool_w, pool_scale.reshape(1, N_POOL * HEAD_DIM))


def _hgrn_kernel(q_ref, f_ref, i_ref, gt_ref, lb_ref, on_ref, o_ref,
                 qe_scr, o_scr, u_scr, d_scr, s_scr):
    seq = q_ref.shape[0]
    ck = HGRN_CHUNK
    grp = 4 * ck
    n_grp = seq // grp
    n_chunk = seq // ck
    lb = lb_ref[...]

    ri = lax.broadcasted_iota(jnp.int32, (grp, grp), 0)
    ci = lax.broadcasted_iota(jnp.int32, (grp, grp), 1)
    tri_mask = (ri // ck == ci // ck) & (ci <= ri)
    tri = tri_mask.astype(BF16)

    groups = [slice(gi * grp, (gi + 1) * grp) for gi in range(n_grp)]
    q = _silu(q_ref[...].astype(F32))
    f = lb + (1.0 - lb) * jax.nn.sigmoid(f_ref[...].astype(F32))
    k = 1.0 - f
    g = jnp.log(f)
    g_hi = g.astype(BF16)
    g_lo = (g - g_hi.astype(F32)).astype(BF16)
    b = jnp.concatenate([_dot(tri, g_hi[r]) + _dot(tri, g_lo[r]) for r in groups], axis=0)
    b3 = b.reshape(n_chunk, ck, HEAD_DIM)
    b_mid = b3[:, ck // 2 - 1:ck // 2, :]
    b_last = b3[:, ck - 1:ck, :]
    q3 = q.reshape(n_chunk, ck, HEAD_DIM)
    k3 = k.reshape(n_chunk, ck, HEAD_DIM)
    qs = (q3 * jnp.exp(b3 - b_mid)).reshape(seq, HEAD_DIM).astype(BF16)
    ks = (k3 * jnp.exp(b_mid - b3)).reshape(seq, HEAD_DIM).astype(BF16)
    qe_scr[...] = (q * jnp.exp(b)).astype(BF16)
    kd = (k3 * jnp.exp(b_last - b3)).reshape(seq, HEAD_DIM).astype(BF16)
    d_scr[...] = jnp.exp(b_last).reshape(n_chunk, HEAD_DIM)
    for r in groups:
        att = jnp.where(tri_mask, _dot_nt(qs[r], ks[r]), 0.0)
        o_scr[r, :] = _dot(att.astype(BF16), i_ref[r, :])
    for c in range(n_chunk):
        cr = slice(c * ck, (c + 1) * ck)
        u_scr[c] = _dot_tn(i_ref[cr, :], kd[cr])

    st = jnp.zeros((HEAD_DIM, HEAD_DIM), F32)
    for c in range(n_chunk):
        s_scr[c] = st.astype(BF16)
        st = st * d_scr[c:c + 1, :] + u_scr[c]

    for gi in range(n_grp):
        rows = slice(gi * grp, (gi + 1) * grp)
        inter = jnp.concatenate(
            [_dot_nt(qe_scr[gi * grp + c * ck:gi * grp + (c + 1) * ck, :], s_scr[gi * 4 + c])
             for c in range(4)], axis=0)
        o = _rms(o_scr[rows, :] + inter, on_ref[...])
        o_ref[rows, :] = (o * _silu(gt_ref[rows, :].astype(F32))).astype(o_ref.dtype)


def _hgrn(proj, lb, out_norm):
    bsz, seq, _ = proj.shape
    nh = HGRN_HEADS
    n_chunk = seq // HGRN_CHUNK
    spec = lambda off: pl.BlockSpec((None, seq, HEAD_DIM), lambda b, h: (b, 0, off + h))
    vec = pl.BlockSpec((1, HEAD_DIM), lambda b, h: (0, h))
    return pl.pallas_call(
        _hgrn_kernel,
        grid=(bsz, nh),
        in_specs=[spec(0), spec(nh), spec(2 * nh), spec(3 * nh), vec, vec],
        out_specs=pl.BlockSpec((None, seq, HEAD_DIM), lambda b, h: (b, 0, h)),
        out_shape=jax.ShapeDtypeStruct((bsz, seq, nh * HEAD_DIM), BF16),
        scratch_shapes=[
            pltpu.VMEM((seq, HEAD_DIM), BF16),
            pltpu.VMEM((seq, HEAD_DIM), F32),
            pltpu.VMEM((n_chunk, HEAD_DIM, HEAD_DIM), F32),
            pltpu.VMEM((n_chunk, HEAD_DIM), F32),
            pltpu.VMEM((n_chunk, HEAD_DIM, HEAD_DIM), BF16),
        ],
        compiler_params=_cparams(("parallel", "arbitrary")),
        name="hgrn2",
    )(proj, proj, proj, proj, lb.reshape(1, -1), out_norm.reshape(1, -1))


def _router_kernel(x_ref, g_ref, w_ref, h_ref, meta_ref, cnt_ref, carry):
    i = pl.program_id(0)
    tm = x_ref.shape[0]

    @pl.when(i == 0)
    def _():
        carry[...] = jnp.zeros_like(carry)

    h = _rms(x_ref[...], g_ref[...])
    h_ref[...] = _pack_bf16_pair(h)
    h_hi = h.astype(BF16)
    h_lo = (h - h_hi.astype(F32)).astype(BF16)
    w = w_ref[...]
    w_hi = w.astype(BF16)
    w_lo = (w - w_hi.astype(F32)).astype(BF16)
    logits = _dot(h_hi, w_hi) + (_dot(h_hi, w_lo) + _dot(h_lo, w_hi))
    lane = lax.broadcasted_iota(jnp.int32, logits.shape, 1)
    logits = jnp.where(lane < N_EXPERTS, logits, NEG)
    m1 = jnp.max(logits, axis=-1, keepdims=True)
    i1 = jnp.min(jnp.where(logits == m1, lane, LANES), axis=-1, keepdims=True)
    rest = jnp.where(lane == i1, NEG, logits)
    m2 = jnp.max(rest, axis=-1, keepdims=True)
    i2 = jnp.min(jnp.where(rest == m2, lane, LANES), axis=-1, keepdims=True)
    e = jnp.exp(m2 - m1)
    g1 = 1.0 / (1.0 + e)
    g2 = e / (1.0 + e)
    oh1 = (lane == i1).astype(F32)
    oh2 = (lane == i2).astype(F32)
    cnt = oh1 + oh2
    ri = lax.broadcasted_iota(jnp.int32, (tm, tm), 0)
    ci = lax.broadcasted_iota(jnp.int32, (tm, tm), 1)
    before = (ci < ri).astype(BF16)
    prior = _dot(before, cnt.astype(BF16)) + carry[...]
    r1 = jnp.sum(oh1 * prior, axis=-1, keepdims=True)
    r2 = jnp.sum(oh2 * prior, axis=-1, keepdims=True)
    carry[...] += jnp.sum(cnt, axis=0, keepdims=True)
    cnt_ref[...] = carry[...]
    meta = jnp.where(lane == 0, i1.astype(F32), 0.0)
    meta = jnp.where(lane == 1, i2.astype(F32), meta)
    meta = jnp.where(lane == 2, r1, meta)
    meta = jnp.where(lane == 3, r2, meta)
    meta = jnp.where(lane == 4, g1, meta)
    meta = jnp.where(lane == 5, g2, meta)
    meta_ref[...] = meta


def _router(x, g, router_w, tm=512):
    t, d = x.shape
    w_pad = jnp.pad(router_w, ((0, 0), (0, LANES - router_w.shape[1])))
    return pl.pallas_call(
        _router_kernel,
        grid=(t // tm,),
        in_specs=[
            pl.BlockSpec((tm, d), lambda i: (i, 0)),
            pl.BlockSpec((1, d), lambda i: (0, 0)),
            pl.BlockSpec((d, LANES), lambda i: (0, 0)),
        ],
        out_specs=[
            pl.BlockSpec((tm, d // 2), lambda i: (i, 0)),
            pl.BlockSpec((tm, LANES), lambda i: (i, 0)),
            pl.BlockSpec((1, LANES), lambda i: (0, 0)),
        ],
        out_shape=[
            jax.ShapeDtypeStruct((t, d // 2), jnp.uint32),
            jax.ShapeDtypeStruct((t, LANES), F32),
            jax.ShapeDtypeStruct((1, LANES), F32),
        ],
        scratch_shapes=[pltpu.VMEM((1, LANES), F32)],
        compiler_params=_cparams(("arbitrary",)),
        name="moe_router",
    )(x, g.reshape(1, d), w_pad)


def _row_copy(src, dst, s, d, sem):
    return pltpu.make_async_copy(src.at[pl.ds(s, 1)], dst.at[pl.ds(d, 1)], sem)


def _dispatch_kernel(dest_ref, h_ref, o_hbm, sem):
    tm = h_ref.shape[0]

    def start(t, carry):
        _row_copy(h_ref, o_hbm, t, dest_ref[0, 0, 2 * t], sem).start(priority=0)
        _row_copy(h_ref, o_hbm, t, dest_ref[0, 0, 2 * t + 1], sem).start(priority=1)
        return carry

    lax.fori_loop(0, tm, start, 0, unroll=8)

    def wait(t, carry):
        _row_copy(h_ref, o_hbm, 0, 0, sem).wait()
        _row_copy(h_ref, o_hbm, 0, 0, sem).wait()
        return carry

    lax.fori_loop(0, tm, wait, 0, unroll=8)


def _dispatch(h, dest, n_rows, tm=512):
    t, d = h.shape
    dest3 = dest.reshape(t // tm, 1, 2 * tm)
    return pl.pallas_call(
        _dispatch_kernel,
        grid=(t // tm,),
        in_specs=[
            pl.BlockSpec((1, 1, 2 * tm), lambda i: (i, 0, 0), memory_space=pltpu.SMEM),
            pl.BlockSpec((tm, d), lambda i: (i, 0)),
        ],
        out_specs=pl.BlockSpec(memory_space=pl.ANY),
        out_shape=jax.ShapeDtypeStruct((n_rows, d), h.dtype),
        scratch_shapes=[pltpu.SemaphoreType.DMA(())],
        compiler_params=_cparams(("arbitrary",)),
        name="moe_dispatch",
    )(dest3, h)


def _moe_kernel(be_ref, bb_ref, nv_ref, na_ref, x_ref, wg_ref, wu_ref, wd_ref, o_ref, xb_scr):
    i = pl.program_id(0)
    j = pl.program_id(1)
    n_valid = nv_ref[i]

    @pl.when(i < na_ref[0])
    def _():
        @pl.when(j == 0)
        def _():
            row = lax.broadcasted_iota(jnp.int32, x_ref.shape, 0)
            lo, hi = _unpack_bf16_pair(jnp.where(row < n_valid, x_ref[...], jnp.uint32(0)))
            xb_scr[:, :lo.shape[1]] = lo
            xb_scr[:, lo.shape[1]:] = hi
            o_ref[...] = jnp.zeros_like(o_ref)

        def run(rows):
            xb = xb_scr[rows, :]
            a = _dot(xb, wg_ref[0].astype(BF16))
            b = _dot(xb, wu_ref[0].astype(BF16))
            hid = (_silu(a) * b).astype(BF16)
            o_ref[rows, :] += _dot(hid, wd_ref[0].astype(BF16))

        quarter = x_ref.shape[0] // 4
        for nq in range(1, 5):
            @pl.when((n_valid > (nq - 1) * quarter) & (n_valid <= nq * quarter))
            def _(nq=nq):
                run(slice(0, nq * quarter))


def _moe_experts(x_rows, wg, wu, wd, blk_e, blk_idx, blk_valid, n_live, n_blocks):
    dp = x_rows.shape[1]
    d = 2 * dp
    dff = wg.shape[2]
    n_ff = dff // MOE_TF
    r = MOE_ROWS

    def ffi(i, j, na):
        return jnp.where(i < na[0], j, n_ff - 1)

    grid_spec = pltpu.PrefetchScalarGridSpec(
        num_scalar_prefetch=4,
        grid=(n_blocks, n_ff),
        in_specs=[
            pl.BlockSpec((r, dp), lambda i, j, be, bb, nv, na: (bb[i], 0)),
            pl.BlockSpec((1, d, MOE_TF), lambda i, j, be, bb, nv, na: (be[i], 0, ffi(i, j, na))),
            pl.BlockSpec((1, d, MOE_TF), lambda i, j, be, bb, nv, na: (be[i], 0, ffi(i, j, na))),
            pl.BlockSpec((1, MOE_TF, d), lambda i, j, be, bb, nv, na: (be[i], ffi(i, j, na), 0)),
        ],
        out_specs=pl.BlockSpec((r, d), lambda i, j, be, bb, nv, na: (bb[i], 0)),
        scratch_shapes=[pltpu.VMEM((r, d), BF16)],
    )
    return pl.pallas_call(
        _moe_kernel,
        grid_spec=grid_spec,
        out_shape=jax.ShapeDtypeStruct((x_rows.shape[0], d), F32),
        compiler_params=_cparams(("arbitrary", "arbitrary")),
        name="moe_experts",
    )(blk_e, blk_idx, blk_valid, n_live, x_rows, wg, wu, wd)


def _combine_kernel(dest_ref, next_ref, x_ref, meta_ref, nf_ref, y_hbm, o_ref, buf, sem):
    i = pl.program_id(0)
    tm = x_ref.shape[0]
    slot = i % 2

    def gather(idx_ref, s):
        def start(t, carry):
            _row_copy(y_hbm, buf.at[s, 0], idx_ref[0, 0, 2 * t], t, sem.at[s]).start(priority=0)
            _row_copy(y_hbm, buf.at[s, 1], idx_ref[0, 0, 2 * t + 1], t, sem.at[s]).start(priority=1)
            return carry

        lax.fori_loop(0, tm, start, 0, unroll=8)

    @pl.when(i == 0)
    def _():
        gather(dest_ref, slot)

    @pl.when(i + 1 < pl.num_programs(0))
    def _():
        gather(next_ref, 1 - slot)

    def wait(t, carry):
        _row_copy(y_hbm, buf.at[slot, 0], 0, 0, sem.at[slot]).wait()
        _row_copy(y_hbm, buf.at[slot, 1], 0, 0, sem.at[slot]).wait()
        return carry

    lax.fori_loop(0, tm, wait, 0, unroll=8)

    meta = meta_ref[...]
    g1 = meta[:, 4:5]
    g2 = meta[:, 5:6]
    x = x_ref[...] + (g1 * buf[slot, 0] + g2 * buf[slot, 1])
    o_ref[...] = _rms(x, nf_ref[...])


def _combine(x, meta, norm_final, y_rows, dest, tm=256):
    t, d = x.shape
    n_steps = t // tm
    dest3 = dest.reshape(n_steps, 1, 2 * tm)
    return pl.pallas_call(
        _combine_kernel,
        grid=(n_steps,),
        in_specs=[
            pl.BlockSpec((1, 1, 2 * tm), lambda i: (i, 0, 0), memory_space=pltpu.SMEM),
            pl.BlockSpec((1, 1, 2 * tm), lambda i: (jnp.minimum(i + 1, n_steps - 1), 0, 0),
                         memory_space=pltpu.SMEM),
            pl.BlockSpec((tm, d), lambda i: (i, 0)),
            pl.BlockSpec((tm, LANES), lambda i: (i, 0)),
            pl.BlockSpec((1, d), lambda i: (0, 0)),
            pl.BlockSpec(memory_space=pl.ANY),
        ],
        out_specs=pl.BlockSpec((tm, d), lambda i: (i, 0)),
        out_shape=jax.ShapeDtypeStruct((t, d), F32),
        scratch_shapes=[pltpu.VMEM((2, 2, tm, d), F32), pltpu.SemaphoreType.DMA((2,))],
        compiler_params=_cparams(("arbitrary",)),
        name="moe_combine",
    )(dest3, dest3, x, meta, norm_final.reshape(1, d), y_rows)


def _moe_block_table(counts, t):
    r = MOE_ROWS
    n_blocks = (2 * t) // r + N_EXPERTS
    nblk = (counts + r - 1) // r
    ends = jnp.cumsum(nblk)
    starts = ends - nblk
    n_live = ends[-1]
    step = jnp.minimum(jnp.arange(n_blocks, dtype=jnp.int32), n_live - 1)
    blk_e = jnp.searchsorted(ends, step, side="right").astype(jnp.int32)
    local = step - starts[blk_e]
    blk_valid = jnp.clip(counts[blk_e] - local * r, 0, r)
    return (starts * r, blk_e, step, blk_valid.astype(jnp.int32),
            n_live.reshape(1).astype(jnp.int32), n_blocks)


def kernel(x, norm_mix, norm_ffn, w_in_even, pool_w, pool_scale, w_out_even, ffn_w_gate,
           ffn_w_up, ffn_w_down, w_in_odd, lower_bound_logits, hgrn_out_norm, w_out_odd,
           router_w, moe_w_gate, moe_w_up, moe_w_down, norm_final):
    bsz, seq, d = x.shape
    t = bsz * seq

    lb_all = jnp.cumsum(jax.nn.softmax(lower_bound_logits.astype(F32), axis=0), axis=0)
    lb_all = lb_all - lb_all[:1]

    n_in = w_in_even.shape[2]
    q_scale = math.log2(math.e) / math.sqrt(HEAD_DIM)
    col_scale = jnp.where(jnp.arange(n_in) < A_WIDTH, q_scale, 1.0).astype(F32)
    proj = _norm_matmul_res(x, norm_mix[0], w_in_even[0].astype(BF16), col_scale)
    mix = _attn_pool(proj.reshape(bsz, seq, -1), pool_w[0], pool_scale[0])
    xt = _matmul_unres(mix.reshape(t, -1), w_out_even[0].astype(BF16), x)
    xt = _swiglu_res(xt, norm_ffn[0], ffn_w_gate[0].astype(BF16), ffn_w_up[0].astype(BF16),
                     ffn_w_down[0].astype(BF16))

    proj = _norm_matmul(xt, norm_mix[1], w_in_odd[0].astype(BF16))
    o = _hgrn(proj.reshape(bsz, seq, -1), lb_all[1], hgrn_out_norm[0])
    xt = _matmul_res(o.reshape(t, -1), w_out_odd[0].astype(BF16), xt)

    h, meta, cnt = _router(xt, norm_ffn[1], router_w[0])
    counts = cnt[0, :N_EXPERTS].astype(jnp.int32)
    row0, blk_e, blk_idx, blk_valid, n_live, n_blocks = _moe_block_table(counts, t)
    dest = (row0[meta[:, 0:2].astype(jnp.int32)] + meta[:, 2:4].astype(jnp.int32)).reshape(-1)
    x_rows = _dispatch(h, dest, n_blocks * MOE_ROWS)
    y_rows = _moe_experts(x_rows, moe_w_gate[0], moe_w_up[0], moe_w_down[0],
                          blk_e, blk_idx, blk_valid, n_live, n_blocks)
    out = _combine(xt, meta, norm_final, y_rows, dest)
    return out.reshape(bsz, seq, d)
```

```python
import math

import jax
import jax.numpy as jnp
import numpy as np
from jax import lax
from jax.experimental import pallas as pl
from jax.experimental.pallas import tpu as pltpu

F32 = jnp.float32
BF16 = jnp.bfloat16

D_MODEL = 2048
HEAD_DIM = 128
A_HEADS = 12
A_WIDTH = A_HEADS * HEAD_DIM
N_POOL = 4
POOL_WINDOWS = (2, 4, 8, 16)
HGRN_HEADS = 16
HGRN_CHUNK = 64
N_EXPERTS = 8
EPS = 1e-6
NEG = -1e30

LANES = 128
VMEM_LIMIT = 56 * 1024 * 1024

MOE_ROWS = 1024
MOE_TF = 256
HI16 = 0xFFFF0000


def _pack_bf16_pair(x):
    n = x.shape[1] // 2
    bits = lax.bitcast_convert_type(x.astype(BF16).astype(F32), jnp.uint32)
    return (bits[:, n:] & jnp.uint32(HI16)) | (bits[:, :n] >> 16)


def _unpack_bf16_pair(p):
    lo = lax.bitcast_convert_type(p << 16, F32).astype(BF16)
    hi = lax.bitcast_convert_type(p & jnp.uint32(HI16), F32).astype(BF16)
    return lo, hi

N_RES = 16


def _cparams(sem, vmem=VMEM_LIMIT):
    return pltpu.CompilerParams(dimension_semantics=sem, vmem_limit_bytes=vmem)


def _rms(x, g):
    ms = jnp.mean(x * x, axis=-1, keepdims=True)
    return x * lax.rsqrt(ms + EPS) * g


def _half_silu(h):
    return h + h * jnp.tanh(h)


def _silu(x):
    return _half_silu(0.5 * x)


def _dot(a, b):
    return jnp.dot(a, b, preferred_element_type=F32)


def _dot_nt(a, b):
    return lax.dot_general(a, b, (((1,), (1,)), ((), ())), preferred_element_type=F32)


def _dot_tn(a, b):
    return lax.dot_general(a, b, (((0,), (0,)), ((), ())), preferred_element_type=F32)


def _norm_matmul_kernel(x_ref, g_ref, w_ref, cs_ref, o_ref, h_scr):
    @pl.when(pl.program_id(1) == 0)
    def _():
        h_scr[...] = _rms(x_ref[...], g_ref[...]).astype(BF16)

    o_ref[...] = (_dot(h_scr[...], w_ref[...]) * cs_ref[...]).astype(o_ref.dtype)


def _norm_matmul(x, g, w, col_scale, tm=1024, tn=1024):
    m, d = x.shape
    n = w.shape[1]
    return pl.pallas_call(
        _norm_matmul_kernel,
        grid=(m // tm, n // tn),
        in_specs=[
            pl.BlockSpec((tm, d), lambda i, j: (i, 0)),
            pl.BlockSpec((1, d), lambda i, j: (0, 0)),
            pl.BlockSpec((d, tn), lambda i, j: (0, j)),
            pl.BlockSpec((1, tn), lambda i, j: (0, j)),
        ],
        out_specs=pl.BlockSpec((tm, tn), lambda i, j: (i, j)),
        out_shape=jax.ShapeDtypeStruct((m, n), BF16),
        scratch_shapes=[pltpu.VMEM((tm, d), BF16)],
        compiler_params=_cparams(("parallel", "arbitrary")),
        name="norm_matmul",
    )(x, g.reshape(1, d), w, col_scale.reshape(1, n))


def _norm_matmul_res_kernel(x_hbm, g_ref, w_ref, cs_ref, o_ref, xbuf, h_scr, sem):
    i = pl.program_id(0)
    n_res_tile, rows, _ = xbuf.shape[1:]
    halves = N_RES // n_res_tile

    def fetch(step, slot):
        b = step // halves
        r0 = (step % halves) * n_res_tile
        return [pltpu.make_async_copy(x_hbm.at[b, :, r0 + r, :], xbuf.at[slot, r], sem.at[slot, r])
                for r in range(n_res_tile)]

    @pl.when(pl.program_id(1) == 0)
    def _():
        slot = i % 2

        @pl.when(i == 0)
        def _():
            for c in fetch(i, slot):
                c.start()

        @pl.when(i + 1 < pl.num_programs(0))
        def _():
            for c in fetch(i + 1, 1 - slot):
                c.start()

        for r, c in enumerate(fetch(i, slot)):
            c.wait()
            h_scr[r * rows:(r + 1) * rows, :] = _rms(xbuf[slot, r], g_ref[...]).astype(BF16)

    o_ref[...] = (_dot(h_scr[...], w_ref[...]) * cs_ref[...]).astype(o_ref.dtype)


def _norm_matmul_res(x, g, w, col_scale, n_res_tile=8, tn=1024):
    bsz, seq, d = x.shape
    n = w.shape[1]
    rows = seq // N_RES
    tm = n_res_tile * rows
    halves = N_RES // n_res_tile
    return pl.pallas_call(
        _norm_matmul_res_kernel,
        grid=(bsz * halves, n // tn),
        in_specs=[
            pl.BlockSpec(memory_space=pl.ANY),
            pl.BlockSpec((1, d), lambda i, j: (0, 0)),
            pl.BlockSpec((d, tn), lambda i, j: (0, j)),
            pl.BlockSpec((1, tn), lambda i, j: (0, j)),
        ],
        out_specs=pl.BlockSpec((tm, tn), lambda i, j: (i, j)),
        out_shape=jax.ShapeDtypeStruct((bsz * seq, n), BF16),
        scratch_shapes=[pltpu.VMEM((2, n_res_tile, rows, d), F32), pltpu.VMEM((tm, d), BF16),
                        pltpu.SemaphoreType.DMA((2, n_res_tile))],
        compiler_params=_cparams(("arbitrary", "arbitrary")),
        name="norm_matmul_res",
    )(x.reshape(bsz, rows, N_RES, d), g.reshape(1, d), w, col_scale.reshape(1, n))


def _matmul_res_kernel(a_ref, w_ref, r_ref, o_ref):
    o_ref[...] = r_ref[...] + _dot(a_ref[...], w_ref[...])


def _matmul_res(a, w, res, tm=512, tn=2048):
    m, k = a.shape
    n = w.shape[1]
    return pl.pallas_call(
        _matmul_res_kernel,
        grid=(m // tm, n // tn),
        in_specs=[
            pl.BlockSpec((tm, k), lambda i, j: (i, 0)),
            pl.BlockSpec((k, tn), lambda i, j: (0, j)),
            pl.BlockSpec((tm, tn), lambda i, j: (i, j)),
        ],
        out_specs=pl.BlockSpec((tm, tn), lambda i, j: (i, j)),
        out_shape=jax.ShapeDtypeStruct((m, n), F32),
        compiler_params=_cparams(("parallel", "arbitrary")),
        name="matmul_res",
    )(a, w, res)


def _matmul_unres_kernel(a_ref, w_ref, r_hbm, o_hbm, rbuf, obuf, rsem, osem):
    i = pl.program_id(0)
    n_steps = pl.num_programs(0)
    n_res_tile, rows, _ = rbuf.shape[1:]
    parts = N_RES // n_res_tile

    def res_fetch(step, slot):
        b, r0 = step // parts, (step % parts) * n_res_tile
        return [pltpu.make_async_copy(r_hbm.at[b, :, r0 + r, :], rbuf.at[slot, r], rsem.at[slot, r])
                for r in range(n_res_tile)]

    def out_store(step, slot):
        b, r0 = step // parts, (step % parts) * n_res_tile
        return [pltpu.make_async_copy(obuf.at[slot, r], o_hbm.at[b, :, r0 + r, :], osem.at[slot, r])
                for r in range(n_res_tile)]

    slot = i % 2

    @pl.when(i == 0)
    def _():
        for c in res_fetch(i, slot):
            c.start()

    @pl.when(i + 1 < n_steps)
    def _():
        for c in res_fetch(i + 1, 1 - slot):
            c.start()

    y = _dot(a_ref[...], w_ref[...])

    @pl.when(i >= 2)
    def _():
        for c in out_store(i - 2, slot):
            c.wait()

    for r, c in enumerate(res_fetch(i, slot)):
        c.wait()
        obuf[slot, r] = rbuf[slot, r] + y[r * rows:(r + 1) * rows, :]
    for c in out_store(i, slot):
        c.start()

    @pl.when(i == n_steps - 1)
    def _():
        for c in out_store(i - 1, 1 - slot) + out_store(i, slot):
            c.wait()


def _matmul_unres(a, w, res, n_res_tile=4):
    bsz, seq, d = res.shape
    k = a.shape[1]
    rows = seq // N_RES
    tm = n_res_tile * rows
    parts = N_RES // n_res_tile
    assert bsz * parts >= 2
    buf = pltpu.VMEM((2, n_res_tile, rows, d), F32)
    out = pl.pallas_call(
        _matmul_unres_kernel,
        grid=(bsz * parts,),
        in_specs=[
            pl.BlockSpec((tm, k), lambda i: (i, 0)),
            pl.BlockSpec((k, d), lambda i: (0, 0)),
            pl.BlockSpec(memory_space=pl.ANY),
        ],
        out_specs=pl.BlockSpec(memory_space=pl.ANY),
        out_shape=jax.ShapeDtypeStruct((bsz, rows, N_RES, d), F32),
        scratch_shapes=[buf, buf, pltpu.SemaphoreType.DMA((2, n_res_tile)),
                        pltpu.SemaphoreType.DMA((2, n_res_tile))],
        compiler_params=_cparams(("arbitrary",)),
        name="matmul_unres",
    )(a, w, res.reshape(bsz, rows, N_RES, d))
    return out.reshape(bsz * seq, d)


def _swiglu_kernel(x_ref, g_ref, wg_ref, wu_ref, wd_ref, o_ref, h_scr, acc_scr):
    j = pl.program_id(1)

    @pl.when(j == 0)
    def _():
        h_scr[...] = _rms(x_ref[...], g_ref[...]).astype(BF16)
        acc_scr[...] = jnp.zeros_like(acc_scr)

    h = h_scr[...]
    a = _dot(h, wg_ref[...])
    b = _dot(h, wu_ref[...])
    hid = (_silu(a) * b).astype(BF16)
    acc_scr[...] += _dot(hid, wd_ref[...])

    @pl.when(j == pl.num_programs(1) - 1)
    def _():
        o_ref[...] = x_ref[...] + acc_scr[...]


def _swiglu_res(x, g, wg, wu, wd, tm=512, tf=512):
    m, d = x.shape
    ff = wg.shape[1]
    return pl.pallas_call(
        _swiglu_kernel,
        grid=(m // tm, ff // tf),
        in_specs=[
            pl.BlockSpec((tm, d), lambda i, j: (i, 0)),
            pl.BlockSpec((1, d), lambda i, j: (0, 0)),
            pl.BlockSpec((d, tf), lambda i, j: (0, j)),
            pl.BlockSpec((d, tf), lambda i, j: (0, j)),
            pl.BlockSpec((tf, d), lambda i, j: (j, 0)),
        ],
        out_specs=pl.BlockSpec((tm, d), lambda i, j: (i, 0)),
        out_shape=jax.ShapeDtypeStruct((m, d), F32),
        scratch_shapes=[pltpu.VMEM((tm, d), BF16), pltpu.VMEM((tm, d), F32)],
        compiler_params=_cparams(("parallel", "arbitrary")),
        name="swiglu_res",
    )(x, g.reshape(1, d), wg, wu, wd)


def _attn_bias_table():
    blk = HEAD_DIM
    iq = np.arange(blk)[:, None]
    jk = np.arange(2 * blk)[None, :]

    def offs(idx, n_chunk, size):
        return n_chunk * (idx % size) + idx // size

    masks = []
    for n_chunk in (1, 4, 16):
        size = blk // n_chunk
        lq = offs(iq, n_chunk, size)
        masks.append((offs(jk, n_chunk, size) <= lq) & (jk < blk))
        dist = lq - (offs(jk, n_chunk, 2 * size) - blk)
        masks.append((dist >= 0) & (dist <= blk))
    return np.where(np.stack(masks), 0.0, NEG).astype(np.float32)


def _attn_pool_kernel(q_ref, k_ref, v_ref, bias, pw_ref, ps_ref, o_ref,
                      qf, kf, vf, ms, ls, accs):
    h = pl.program_id(1)
    seq = q_ref.shape[0]
    blk = HEAD_DIM
    bf16_rows = 16

    @pl.when(h < A_HEADS)
    def _attention():
        qf[...] = q_ref[...].astype(F32)
        kf[...] = k_ref[...].astype(F32)
        vf[...] = v_ref[...].astype(F32)

        def gather(ref, ref_f32, starts, size):
            if all(v % bf16_rows == 0 for v in [size] + starts):
                return jnp.concatenate([ref[s0:s0 + size, :] for s0 in starts], axis=0)
            return jnp.concatenate([ref_f32[s0:s0 + size, :] for s0 in starts],
                                   axis=0).astype(BF16)

        def scatter(br, starts, size, m, l, acc):
            mb = jnp.broadcast_to(m, acc.shape)
            lb = jnp.broadcast_to(l, acc.shape)
            for c, s0 in enumerate(starts):
                src = slice(c * size, (c + 1) * size)
                ms[br, s0:s0 + size, :] = mb[src]
                ls[br, s0:s0 + size, :] = lb[src]
                accs[br, s0:s0 + size, :] = acc[src]

        kinds = {}
        for br, n_chunk in enumerate((1, 4, 16)):
            size = blk // n_chunk
            for grp in range(N_RES // n_chunk):
                bases = [(grp + (N_RES // n_chunk) * c) * blk for c in range(n_chunk)]
                for n in range(n_chunk):
                    qs = [b0 + n * size for b0 in bases]
                    if n == 0:
                        kinds.setdefault((2 * br, blk), []).append((br, size, qs, qs, size))
                    else:
                        ks = [s0 - size for s0 in qs]
                        kinds.setdefault((2 * br + 1, 2 * blk), []).append(
                            (br, size, qs, ks, 2 * size))

        batch = 8
        for (bias_idx, kw), blocks in kinds.items():
            bi = bias[bias_idx][:, :kw]
            for g0 in range(0, len(blocks), batch):
                group = blocks[g0:g0 + batch]
                s = jnp.concatenate(
                    [_dot_nt(gather(q_ref, qf, qs, size), gather(k_ref, kf, ks, ksize))
                     for _, size, qs, ks, ksize in group], axis=0)
                s = (s.reshape(len(group), blk, kw) + bi[None]).reshape(len(group) * blk, kw)
                m = jnp.max(s, axis=-1, keepdims=True)
                p = jnp.exp2(s - m)
                l = jnp.sum(p, axis=-1, keepdims=True)
                pb = p.astype(BF16)
                for g, (br, size, qs, ks, ksize) in enumerate(group):
                    rows = slice(g * blk, (g + 1) * blk)
                    acc = _dot(pb[rows], gather(v_ref, vf, ks, ksize))
                    scatter(br, qs, size, m[rows], l[rows], acc)

        slab = 2 * blk
        for s0 in range(0, seq, slab):
            rows = slice(s0, s0 + slab)
            m_all = jnp.maximum(jnp.maximum(ms[0, rows, :], ms[1, rows, :]), ms[2, rows, :])
            num = jnp.zeros((slab, blk), F32)
            den = jnp.zeros((slab, blk), F32)
            for br in range(3):
                w = jnp.exp2(ms[br, rows, :] - m_all)
                num = num + accs[br, rows, :] * w
                den = den + ls[br, rows, :] * w
            o_ref[rows, :] = (num / den).astype(o_ref.dtype)

    for gi, w in enumerate(POOL_WINDOWS):
        @pl.when(h == A_HEADS + gi)
        def _pool(gi=gi, w=w):
            pf = q_ref[...].astype(F32)
            row = lax.broadcasted_iota(jnp.int32, pf.shape, 0)
            l_idx = row % blk
            pos = N_RES * l_idx + row // blk
            s = pf
            k = 1
            while k < w:
                y = pltpu.roll(s, k * blk, axis=0)
                y1 = jnp.where(l_idx == 0, 0.0, pltpu.roll(y, 1, axis=0))
                s = s + jnp.where(row < k * blk, y1, y)
                k *= 2
            cnt = jnp.minimum(pos + 1, w).astype(F32)
            pooled = s / cnt - pf
            y = _dot(pooled.astype(BF16), pw_ref[gi].astype(BF16)) * ps_ref[...]
            o_ref[...] = y.astype(o_ref.dtype)


def _attn_pool(proj, pool_w, pool_scale):
    bsz, seq, _ = proj.shape
    n_steps = A_HEADS + N_POOL
    qmap = lambda b, h: (b, 0, jnp.where(h < A_HEADS, h, h + 2 * A_HEADS))
    kmap = lambda b, h: (b, 0, A_HEADS + jnp.minimum(h, A_HEADS - 1))
    vmap = lambda b, h: (b, 0, 2 * A_HEADS + jnp.minimum(h, A_HEADS - 1))
    f32s = pltpu.VMEM((seq, HEAD_DIM), F32)
    br3 = pltpu.VMEM((3, seq, HEAD_DIM), F32)
    bias = jnp.asarray(_attn_bias_table())
    return pl.pallas_call(
        _attn_pool_kernel,
        grid=(bsz, n_steps),
        in_specs=[
            pl.BlockSpec((None, seq, HEAD_DIM), qmap),
            pl.BlockSpec((None, seq, HEAD_DIM), kmap),
            pl.BlockSpec((None, seq, HEAD_DIM), vmap),
            pl.BlockSpec(bias.shape, lambda b, h: (0, 0, 0)),
            pl.BlockSpec((N_POOL, HEAD_DIM, HEAD_DIM), lambda b, h: (0, 0, 0)),
            pl.BlockSpec((1, HEAD_DIM), lambda b, h: (0, jnp.maximum(h - A_HEADS, 0))),
        ],
        out_specs=pl.BlockSpec((None, seq, HEAD_DIM), lambda b, h: (b, 0, h)),
        out_shape=jax.ShapeDtypeStruct((bsz, seq, n_steps * HEAD_DIM), BF16),
        scratch_shapes=[f32s, f32s, f32s, br3, br3, br3],
        compiler_params=_cparams(("parallel", "arbitrary")),
        name="attn_pool",
    )(proj, proj, proj, bias, pool_w, pool_scale.reshape(1, N_POOL * HEAD_DIM))


def _hgrn_kernel(q_ref, f_ref, i_ref, gt_ref, lb_ref, on_ref, o_ref,
                 qe_scr, o_scr, u_scr, d_scr, s_scr):
    seq = q_ref.shape[0]
    ck = HGRN_CHUNK
    grp = 4 * ck
    n_grp = seq // grp
    n_chunk = seq // ck
    lb = lb_ref[...]

    ri = lax.broadcasted_iota(jnp.int32, (grp, grp), 0)
    ci = lax.broadcasted_iota(jnp.int32, (grp, grp), 1)
    tri_mask = (ri // ck == ci // ck) & (ci <= ri)
    tri = tri_mask.astype(BF16)

    groups = [slice(gi * grp, (gi + 1) * grp) for gi in range(n_grp)]
    q = _half_silu(q_ref[...].astype(F32))
    f = (0.5 + 0.5 * lb) + (0.5 - 0.5 * lb) * jnp.tanh(f_ref[...].astype(F32))
    k = 1.0 - f
    g = jnp.log2(f)
    g_hi = g.astype(BF16)
    g_lo = (g - g_hi.astype(F32)).astype(BF16)
    b = jnp.concatenate([_dot(tri, g_hi[r]) + _dot(tri, g_lo[r]) for r in groups], axis=0)
    b3 = b.reshape(n_chunk, ck, HEAD_DIM)
    b_mid = b3[:, ck // 2 - 1:ck // 2, :]
    b_last = b3[:, ck - 1:ck, :]
    q3 = q.reshape(n_chunk, ck, HEAD_DIM)
    k3 = k.reshape(n_chunk, ck, HEAD_DIM)
    q_mid = q3 * jnp.exp2(b3 - b_mid)
    k_mid = k3 * jnp.exp2(b_mid - b3)
    qs = q_mid.reshape(seq, HEAD_DIM).astype(BF16)
    ks = k_mid.reshape(seq, HEAD_DIM).astype(BF16)
    qe_scr[...] = (q_mid * jnp.exp2(b_mid)).reshape(seq, HEAD_DIM).astype(BF16)
    kd = (k_mid * jnp.exp2(b_last - b_mid)).reshape(seq, HEAD_DIM).astype(BF16)
    d_scr[...] = jnp.exp2(b_last).reshape(n_chunk, HEAD_DIM)
    for r in groups:
        att = jnp.where(tri_mask, _dot_nt(qs[r], ks[r]), 0.0)
        o_scr[r, :] = _dot(att.astype(BF16), i_ref[r, :])
    for c in range(n_chunk):
        cr = slice(c * ck, (c + 1) * ck)
        u_scr[c] = _dot_tn(i_ref[cr, :], kd[cr])

    st = jnp.zeros((HEAD_DIM, HEAD_DIM), F32)
    for c in range(n_chunk):
        s_scr[c] = st.astype(BF16)
        st = st * d_scr[c:c + 1, :] + u_scr[c]

    for gi in range(n_grp):
        rows = slice(gi * grp, (gi + 1) * grp)
        inter = jnp.concatenate(
            [_dot_nt(qe_scr[gi * grp + c * ck:gi * grp + (c + 1) * ck, :], s_scr[gi * 4 + c])
             for c in range(4)], axis=0)
        o = _rms(o_scr[rows, :] + inter, on_ref[...])
        o_ref[rows, :] = (o * _half_silu(gt_ref[rows, :].astype(F32))).astype(o_ref.dtype)


def _hgrn(proj, lb, out_norm):
    bsz, seq, _ = proj.shape
    nh = HGRN_HEADS
    n_chunk = seq // HGRN_CHUNK
    spec = lambda off: pl.BlockSpec((None, seq, HEAD_DIM), lambda b, h: (b, 0, off + h))
    vec = pl.BlockSpec((1, HEAD_DIM), lambda b, h: (0, h))
    return pl.pallas_call(
        _hgrn_kernel,
        grid=(bsz, nh),
        in_specs=[spec(0), spec(nh), spec(2 * nh), spec(3 * nh), vec, vec],
        out_specs=pl.BlockSpec((None, seq, HEAD_DIM), lambda b, h: (b, 0, h)),
        out_shape=jax.ShapeDtypeStruct((bsz, seq, nh * HEAD_DIM), BF16),
        scratch_shapes=[
            pltpu.VMEM((seq, HEAD_DIM), BF16),
            pltpu.VMEM((seq, HEAD_DIM), F32),
            pltpu.VMEM((n_chunk, HEAD_DIM, HEAD_DIM), F32),
            pltpu.VMEM((n_chunk, HEAD_DIM), F32),
            pltpu.VMEM((n_chunk, HEAD_DIM, HEAD_DIM), BF16),
        ],
        compiler_params=_cparams(("parallel", "arbitrary")),
        name="hgrn2",
    )(proj, proj, proj, proj, lb.reshape(1, -1), out_norm.reshape(1, -1))


def _router_kernel(x_ref, g_ref, w_ref, h_ref, meta_ref, cnt_ref, carry):
    i = pl.program_id(0)
    tm = x_ref.shape[0]

    @pl.when(i == 0)
    def _():
        carry[...] = jnp.zeros_like(carry)

    h = _rms(x_ref[...], g_ref[...])
    h_ref[...] = _pack_bf16_pair(h)
    h_hi = h.astype(BF16)
    h_lo = (h - h_hi.astype(F32)).astype(BF16)
    w = w_ref[...]
    w_hi = w.astype(BF16)
    w_lo = (w - w_hi.astype(F32)).astype(BF16)
    logits = _dot(h_hi, w_hi) + (_dot(h_hi, w_lo) + _dot(h_lo, w_hi))
    lane = lax.broadcasted_iota(jnp.int32, logits.shape, 1)
    logits = jnp.where(lane < N_EXPERTS, logits, NEG)
    m1 = jnp.max(logits, axis=-1, keepdims=True)
    i1 = jnp.min(jnp.where(logits == m1, lane, LANES), axis=-1, keepdims=True)
    rest = jnp.where(lane == i1, NEG, logits)
    m2 = jnp.max(rest, axis=-1, keepdims=True)
    i2 = jnp.min(jnp.where(rest == m2, lane, LANES), axis=-1, keepdims=True)
    e = jnp.exp(m2 - m1)
    g1 = 1.0 / (1.0 + e)
    g2 = e / (1.0 + e)
    oh1 = (lane == i1).astype(F32)
    oh2 = (lane == i2).astype(F32)
    cnt = oh1 + oh2
    ri = lax.broadcasted_iota(jnp.int32, (tm, tm), 0)
    ci = lax.broadcasted_iota(jnp.int32, (tm, tm), 1)
    before = (ci < ri).astype(BF16)
    prior = _dot(before, cnt.astype(BF16)) + carry[...]
    r1 = jnp.sum(oh1 * prior, axis=-1, keepdims=True)
    r2 = jnp.sum(oh2 * prior, axis=-1, keepdims=True)
    carry[...] += jnp.sum(cnt, axis=0, keepdims=True)
    cnt_ref[...] = carry[...]
    meta = jnp.where(lane == 0, i1.astype(F32), 0.0)
    meta = jnp.where(lane == 1, i2.astype(F32), meta)
    meta = jnp.where(lane == 2, r1, meta)
    meta = jnp.where(lane == 3, r2, meta)
    meta = jnp.where(lane == 4, g1, meta)
    meta = jnp.where(lane == 5, g2, meta)
    meta_ref[...] = meta


def _router(x, g, router_w, tm=512):
    t, d = x.shape
    w_pad = jnp.pad(router_w, ((0, 0), (0, LANES - router_w.shape[1])))
    return pl.pallas_call(
        _router_kernel,
        grid=(t // tm,),
        in_specs=[
            pl.BlockSpec((tm, d), lambda i: (i, 0)),
            pl.BlockSpec((1, d), lambda i: (0, 0)),
            pl.BlockSpec((d, LANES), lambda i: (0, 0)),
        ],
        out_specs=[
            pl.BlockSpec((tm, d // 2), lambda i: (i, 0)),
            pl.BlockSpec((tm, LANES), lambda i: (i, 0)),
            pl.BlockSpec((1, LANES), lambda i: (0, 0)),
        ],
        out_shape=[
            jax.ShapeDtypeStruct((t, d // 2), jnp.uint32),
            jax.ShapeDtypeStruct((t, LANES), F32),
            jax.ShapeDtypeStruct((1, LANES), F32),
        ],
        scratch_shapes=[pltpu.VMEM((1, LANES), F32)],
        compiler_params=_cparams(("arbitrary",)),
        name="moe_router",
    )(x, g.reshape(1, d), w_pad)


def _row_copy(src, dst, s, d, sem):
    return pltpu.make_async_copy(src.at[pl.ds(s, 1)], dst.at[pl.ds(d, 1)], sem)


def _dispatch_kernel(dest_ref, h_ref, o_hbm, sem):
    tm = h_ref.shape[0]

    def start(t, carry):
        _row_copy(h_ref, o_hbm, t, dest_ref[0, 0, 2 * t], sem).start(priority=0)
        _row_copy(h_ref, o_hbm, t, dest_ref[0, 0, 2 * t + 1], sem).start(priority=1)
        return carry

    lax.fori_loop(0, tm, start, 0, unroll=8)

    def wait(t, carry):
        _row_copy(h_ref, o_hbm, 0, 0, sem).wait()
        _row_copy(h_ref, o_hbm, 0, 0, sem).wait()
        return carry

    lax.fori_loop(0, tm, wait, 0, unroll=8)


def _dispatch(h, dest, n_rows, tm=512):
    t, d = h.shape
    dest3 = dest.reshape(t // tm, 1, 2 * tm)
    return pl.pallas_call(
        _dispatch_kernel,
        grid=(t // tm,),
        in_specs=[
            pl.BlockSpec((1, 1, 2 * tm), lambda i: (i, 0, 0), memory_space=pltpu.SMEM),
            pl.BlockSpec((tm, d), lambda i: (i, 0)),
        ],
        out_specs=pl.BlockSpec(memory_space=pl.ANY),
        out_shape=jax.ShapeDtypeStruct((n_rows, d), h.dtype),
        scratch_shapes=[pltpu.SemaphoreType.DMA(())],
        compiler_params=_cparams(("arbitrary",)),
        name="moe_dispatch",
    )(dest3, h)


def _moe_kernel(be_ref, bb_ref, nv_ref, na_ref, x_ref, wg_ref, wu_ref, wd_ref, o_ref, xb_scr):
    i = pl.program_id(0)
    j = pl.program_id(1)
    n_valid = nv_ref[i]

    @pl.when(i < na_ref[0])
    def _():
        @pl.when(j == 0)
        def _():
            row = lax.broadcasted_iota(jnp.int32, x_ref.shape, 0)
            lo, hi = _unpack_bf16_pair(jnp.where(row < n_valid, x_ref[...], jnp.uint32(0)))
            xb_scr[:, :lo.shape[1]] = lo
            xb_scr[:, lo.shape[1]:] = hi
            o_ref[...] = jnp.zeros_like(o_ref)

        def run(rows):
            xb = xb_scr[rows, :]
            a = _dot(xb, wg_ref[0].astype(BF16))
            b = _dot(xb, wu_ref[0].astype(BF16))
            hid = (_silu(a) * b).astype(BF16)
            o_ref[rows, :] += _dot(hid, wd_ref[0].astype(BF16))

        quarter = x_ref.shape[0] // 4
        for nq in range(1, 5):
            @pl.when((n_valid > (nq - 1) * quarter) & (n_valid <= nq * quarter))
            def _(nq=nq):
                run(slice(0, nq * quarter))


def _moe_experts(x_rows, wg, wu, wd, blk_e, blk_idx, blk_valid, n_live, n_blocks):
    dp = x_rows.shape[1]
    d = 2 * dp
    dff = wg.shape[2]
    n_ff = dff // MOE_TF
    r = MOE_ROWS

    def ffi(i, j, na):
        return jnp.where(i < na[0], j, n_ff - 1)

    grid_spec = pltpu.PrefetchScalarGridSpec(
        num_scalar_prefetch=4,
        grid=(n_blocks, n_ff),
        in_specs=[
            pl.BlockSpec((r, dp), lambda i, j, be, bb, nv, na: (bb[i], 0)),
            pl.BlockSpec((1, d, MOE_TF), lambda i, j, be, bb, nv, na: (be[i], 0, ffi(i, j, na))),
            pl.BlockSpec((1, d, MOE_TF), lambda i, j, be, bb, nv, na: (be[i], 0, ffi(i, j, na))),
            pl.BlockSpec((1, MOE_TF, d), lambda i, j, be, bb, nv, na: (be[i], ffi(i, j, na), 0)),
        ],
        out_specs=pl.BlockSpec((r, d), lambda i, j, be, bb, nv, na: (bb[i], 0)),
        scratch_shapes=[pltpu.VMEM((r, d), BF16)],
    )
    return pl.pallas_call(
        _moe_kernel,
        grid_spec=grid_spec,
        out_shape=jax.ShapeDtypeStruct((x_rows.shape[0], d), F32),
        compiler_params=_cparams(("arbitrary", "arbitrary")),
        name="moe_experts",
    )(blk_e, blk_idx, blk_valid, n_live, x_rows, wg, wu, wd)


def _combine_kernel(dest_ref, next_ref, x_ref, meta_ref, nf_ref, y_hbm, o_ref, buf, sem):
    i = pl.program_id(0)
    tm = x_ref.shape[0]
    slot = i % 2

    def gather(idx_ref, s):
        def start(t, carry):
            _row_copy(y_hbm, buf.at[s, 0], idx_ref[0, 0, 2 * t], t, sem.at[s]).start(priority=0)
            _row_copy(y_hbm, buf.at[s, 1], idx_ref[0, 0, 2 * t + 1], t, sem.at[s]).start(priority=1)
            return carry

        lax.fori_loop(0, tm, start, 0, unroll=8)

    @pl.when(i == 0)
    def _():
        gather(dest_ref, slot)

    @pl.when(i + 1 < pl.num_programs(0))
    def _():
        gather(next_ref, 1 - slot)

    def wait(t, carry):
        _row_copy(y_hbm, buf.at[slot, 0], 0, 0, sem.at[slot]).wait()
        _row_copy(y_hbm, buf.at[slot, 1], 0, 0, sem.at[slot]).wait()
        return carry

    lax.fori_loop(0, tm, wait, 0, unroll=8)

    meta = meta_ref[...]
    g1 = meta[:, 4:5]
    g2 = meta[:, 5:6]
    x = x_ref[...] + (g1 * buf[slot, 0] + g2 * buf[slot, 1])
    o_ref[...] = _rms(x, nf_ref[...])


def _combine(x, meta, norm_final, y_rows, dest, tm=512):
    t, d = x.shape
    n_steps = t // tm
    dest3 = dest.reshape(n_steps, 1, 2 * tm)
    return pl.pallas_call(
        _combine_kernel,
        grid=(n_steps,),
        in_specs=[
            pl.BlockSpec((1, 1, 2 * tm), lambda i: (i, 0, 0), memory_space=pltpu.SMEM),
            pl.BlockSpec((1, 1, 2 * tm), lambda i: (jnp.minimum(i + 1, n_steps - 1), 0, 0),
                         memory_space=pltpu.SMEM),
            pl.BlockSpec((tm, d), lambda i: (i, 0)),
            pl.BlockSpec((tm, LANES), lambda i: (i, 0)),
            pl.BlockSpec((1, d), lambda i: (0, 0)),
            pl.BlockSpec(memory_space=pl.ANY),
        ],
        out_specs=pl.BlockSpec((tm, d), lambda i: (i, 0)),
        out_shape=jax.ShapeDtypeStruct((t, d), F32),
        scratch_shapes=[pltpu.VMEM((2, 2, tm, d), F32), pltpu.SemaphoreType.DMA((2,))],
        compiler_params=_cparams(("arbitrary",)),
        name="moe_combine",
    )(dest3, dest3, x, meta, norm_final.reshape(1, d), y_rows)


def _moe_block_table(counts, t):
    r = MOE_ROWS
    n_blocks = (2 * t) // r + N_EXPERTS - 1
    nblk = (counts + r - 1) // r
    ends = jnp.cumsum(nblk)
    starts = ends - nblk
    n_live = ends[-1]
    step = jnp.minimum(jnp.arange(n_blocks, dtype=jnp.int32), n_live - 1)
    blk_e = jnp.sum(step[:, None] >= ends[None, :], axis=1).astype(jnp.int32)
    local = step - starts[blk_e]
    blk_valid = jnp.clip(counts[blk_e] - local * r, 0, r)
    return (starts * r, blk_e, step, blk_valid.astype(jnp.int32),
            n_live.reshape(1).astype(jnp.int32), n_blocks)


def kernel(x, norm_mix, norm_ffn, w_in_even, pool_w, pool_scale, w_out_even, ffn_w_gate,
           ffn_w_up, ffn_w_down, w_in_odd, lower_bound_logits, hgrn_out_norm, w_out_odd,
           router_w, moe_w_gate, moe_w_up, moe_w_down, norm_final):
    bsz, seq, d = x.shape
    t = bsz * seq

    lb_all = jnp.cumsum(jax.nn.softmax(lower_bound_logits.astype(F32), axis=0), axis=0)
    lb_all = lb_all - lb_all[:1]

    n_in = w_in_even.shape[2]
    q_scale = math.log2(math.e) / math.sqrt(HEAD_DIM)
    col_scale = jnp.where(jnp.arange(n_in) < A_WIDTH, q_scale, 1.0).astype(F32)
    proj = _norm_matmul_res(x, norm_mix[0], w_in_even[0].astype(BF16), col_scale)
    mix = _attn_pool(proj.reshape(bsz, seq, -1), pool_w[0], pool_scale[0])
    xt = _matmul_unres(mix.reshape(t, -1), w_out_even[0].astype(BF16), x)
    xt = _swiglu_res(xt, norm_ffn[0], ffn_w_gate[0].astype(BF16), ffn_w_up[0].astype(BF16),
                     ffn_w_down[0].astype(BF16))

    wid = HGRN_HEADS * HEAD_DIM
    col = jnp.arange(4 * wid) // wid
    proj = _norm_matmul(xt, norm_mix[1], w_in_odd[0].astype(BF16),
                        jnp.where(col == 2, 1.0, 0.5).astype(F32))
    o = _hgrn(proj.reshape(bsz, seq, -1), lb_all[1], hgrn_out_norm[0])
    xt = _matmul_res(o.reshape(t, -1), w_out_odd[0].astype(BF16), xt)

    h, meta, cnt = _router(xt, norm_ffn[1], router_w[0])
    counts = cnt[0, :N_EXPERTS].astype(jnp.int32)
    row0, blk_e, blk_idx, blk_valid, n_live, n_blocks = _moe_block_table(counts, t)
    dest = (row0[meta[:, 0:2].astype(jnp.int32)] + meta[:, 2:4].astype(jnp.int32)).reshape(-1)
    x_rows = _dispatch(h, dest, n_blocks * MOE_ROWS)
    y_rows = _moe_experts(x_rows, moe_w_gate[0], moe_w_up[0], moe_w_down[0],
                          blk_e, blk_idx, blk_valid, n_live, n_blocks)
    out = _combine(xt, meta, norm_final, y_rows, dest)
    return out.reshape(bsz, seq, d)
```

```python
import math

import jax
import jax.numpy as jnp
import numpy as np
from jax import lax
from jax.experimental import pallas as pl
from jax.experimental.pallas import tpu as pltpu

F32 = jnp.float32
BF16 = jnp.bfloat16

D_MODEL = 2048
HEAD_DIM = 128
A_HEADS = 12
A_WIDTH = A_HEADS * HEAD_DIM
N_POOL = 4
POOL_WINDOWS = (2, 4, 8, 16)
HGRN_HEADS = 16
HGRN_CHUNK = 64
N_EXPERTS = 8
EPS = 1e-6
NEG = -1e30

LANES = 128
VMEM_LIMIT = 56 * 1024 * 1024

MOE_ROWS = 1152
MOE_PARTS = 3
MOE_TF = 256
HI16 = 0xFFFF0000


def _pack_bf16_pair(x):
    n = x.shape[1] // 2
    bits = lax.bitcast_convert_type(x.astype(BF16).astype(F32), jnp.uint32)
    return (bits[:, n:] & jnp.uint32(HI16)) | (bits[:, :n] >> 16)


def _unpack_bf16_pair(p):
    lo = lax.bitcast_convert_type(p << 16, F32).astype(BF16)
    hi = lax.bitcast_convert_type(p & jnp.uint32(HI16), F32).astype(BF16)
    return lo, hi

N_RES = 16


def _cparams(sem, vmem=VMEM_LIMIT):
    return pltpu.CompilerParams(dimension_semantics=sem, vmem_limit_bytes=vmem)


def _rms(x, g):
    ms = jnp.mean(x * x, axis=-1, keepdims=True)
    return x * lax.rsqrt(ms + EPS) * g


def _half_silu(h):
    return h + h * jnp.tanh(h)


def _silu(x):
    return _half_silu(0.5 * x)


def _dot(a, b):
    return jnp.dot(a, b, preferred_element_type=F32)


def _dot_nt(a, b):
    return lax.dot_general(a, b, (((1,), (1,)), ((), ())), preferred_element_type=F32)


def _dot_tn(a, b):
    return lax.dot_general(a, b, (((0,), (0,)), ((), ())), preferred_element_type=F32)


def _norm_matmul_kernel(x_ref, g_ref, w_ref, cs_ref, o_ref, h_scr):
    @pl.when(pl.program_id(1) == 0)
    def _():
        h_scr[...] = _rms(x_ref[...], g_ref[...]).astype(BF16)

    o_ref[...] = (_dot(h_scr[...], w_ref[...]) * cs_ref[...]).astype(o_ref.dtype)


def _norm_matmul(x, g, w, col_scale, tm=1024, tn=1024):
    m, d = x.shape
    n = w.shape[1]
    return pl.pallas_call(
        _norm_matmul_kernel,
        grid=(m // tm, n // tn),
        in_specs=[
            pl.BlockSpec((tm, d), lambda i, j: (i, 0)),
            pl.BlockSpec((1, d), lambda i, j: (0, 0)),
            pl.BlockSpec((d, tn), lambda i, j: (0, j)),
            pl.BlockSpec((1, tn), lambda i, j: (0, j)),
        ],
        out_specs=pl.BlockSpec((tm, tn), lambda i, j: (i, j)),
        out_shape=jax.ShapeDtypeStruct((m, n), BF16),
        scratch_shapes=[pltpu.VMEM((tm, d), BF16)],
        compiler_params=_cparams(("parallel", "arbitrary")),
        name="norm_matmul",
    )(x, g.reshape(1, d), w, col_scale.reshape(1, n))


def _norm_matmul_res_kernel(x_hbm, g_ref, w_ref, cs_ref, o_ref, xbuf, h_scr, sem):
    i = pl.program_id(0)
    n_res_tile, rows, _ = xbuf.shape[1:]
    halves = N_RES // n_res_tile

    def fetch(step, slot):
        b = step // halves
        r0 = (step % halves) * n_res_tile
        return [pltpu.make_async_copy(x_hbm.at[b, :, r0 + r, :], xbuf.at[slot, r], sem.at[slot, r])
                for r in range(n_res_tile)]

    @pl.when(pl.program_id(1) == 0)
    def _():
        slot = i % 2

        @pl.when(i == 0)
        def _():
            for c in fetch(i, slot):
                c.start()

        @pl.when(i + 1 < pl.num_programs(0))
        def _():
            for c in fetch(i + 1, 1 - slot):
                c.start()

        for r, c in enumerate(fetch(i, slot)):
            c.wait()
            h_scr[r * rows:(r + 1) * rows, :] = _rms(xbuf[slot, r], g_ref[...]).astype(BF16)

    o_ref[...] = (_dot(h_scr[...], w_ref[...]) * cs_ref[...]).astype(o_ref.dtype)


def _norm_matmul_res(x, g, w, col_scale, n_res_tile=8, tn=1024):
    bsz, seq, d = x.shape
    n = w.shape[1]
    rows = seq // N_RES
    tm = n_res_tile * rows
    halves = N_RES // n_res_tile
    return pl.pallas_call(
        _norm_matmul_res_kernel,
        grid=(bsz * halves, n // tn),
        in_specs=[
            pl.BlockSpec(memory_space=pl.ANY),
            pl.BlockSpec((1, d), lambda i, j: (0, 0)),
            pl.BlockSpec((d, tn), lambda i, j: (0, j)),
            pl.BlockSpec((1, tn), lambda i, j: (0, j)),
        ],
        out_specs=pl.BlockSpec((tm, tn), lambda i, j: (i, j)),
        out_shape=jax.ShapeDtypeStruct((bsz * seq, n), BF16),
        scratch_shapes=[pltpu.VMEM((2, n_res_tile, rows, d), F32), pltpu.VMEM((tm, d), BF16),
                        pltpu.SemaphoreType.DMA((2, n_res_tile))],
        compiler_params=_cparams(("arbitrary", "arbitrary")),
        name="norm_matmul_res",
    )(x.reshape(bsz, rows, N_RES, d), g.reshape(1, d), w, col_scale.reshape(1, n))


def _matmul_res_kernel(a_ref, w_ref, r_ref, o_ref):
    o_ref[...] = r_ref[...] + _dot(a_ref[...], w_ref[...])


def _matmul_res(a, w, res, tm=512, tn=2048):
    m, k = a.shape
    n = w.shape[1]
    return pl.pallas_call(
        _matmul_res_kernel,
        grid=(m // tm, n // tn),
        in_specs=[
            pl.BlockSpec((tm, k), lambda i, j: (i, 0)),
            pl.BlockSpec((k, tn), lambda i, j: (0, j)),
            pl.BlockSpec((tm, tn), lambda i, j: (i, j)),
        ],
        out_specs=pl.BlockSpec((tm, tn), lambda i, j: (i, j)),
        out_shape=jax.ShapeDtypeStruct((m, n), F32),
        compiler_params=_cparams(("parallel", "arbitrary")),
        name="matmul_res",
    )(a, w, res)


def _matmul_unres_kernel(a_ref, w_ref, r_hbm, o_hbm, rbuf, obuf, rsem, osem):
    i = pl.program_id(0)
    n_steps = pl.num_programs(0)
    n_res_tile, rows, _ = rbuf.shape[1:]
    parts = N_RES // n_res_tile

    def res_fetch(step, slot):
        b, r0 = step // parts, (step % parts) * n_res_tile
        return [pltpu.make_async_copy(r_hbm.at[b, :, r0 + r, :], rbuf.at[slot, r], rsem.at[slot, r])
                for r in range(n_res_tile)]

    def out_store(step, slot):
        b, r0 = step // parts, (step % parts) * n_res_tile
        return [pltpu.make_async_copy(obuf.at[slot, r], o_hbm.at[b, :, r0 + r, :], osem.at[slot, r])
                for r in range(n_res_tile)]

    slot = i % 2

    @pl.when(i == 0)
    def _():
        for c in res_fetch(i, slot):
            c.start()

    @pl.when(i + 1 < n_steps)
    def _():
        for c in res_fetch(i + 1, 1 - slot):
            c.start()

    y = _dot(a_ref[...], w_ref[...])

    @pl.when(i >= 2)
    def _():
        for c in out_store(i - 2, slot):
            c.wait()

    for r, c in enumerate(res_fetch(i, slot)):
        c.wait()
        obuf[slot, r] = rbuf[slot, r] + y[r * rows:(r + 1) * rows, :]
    for c in out_store(i, slot):
        c.start()

    @pl.when(i == n_steps - 1)
    def _():
        for c in out_store(i - 1, 1 - slot) + out_store(i, slot):
            c.wait()


def _matmul_unres(a, w, res, n_res_tile=4):
    bsz, seq, d = res.shape
    k = a.shape[1]
    rows = seq // N_RES
    tm = n_res_tile * rows
    parts = N_RES // n_res_tile
    assert bsz * parts >= 2
    buf = pltpu.VMEM((2, n_res_tile, rows, d), F32)
    out = pl.pallas_call(
        _matmul_unres_kernel,
        grid=(bsz * parts,),
        in_specs=[
            pl.BlockSpec((tm, k), lambda i: (i, 0)),
            pl.BlockSpec((k, d), lambda i: (0, 0)),
            pl.BlockSpec(memory_space=pl.ANY),
        ],
        out_specs=pl.BlockSpec(memory_space=pl.ANY),
        out_shape=jax.ShapeDtypeStruct((bsz, rows, N_RES, d), F32),
        scratch_shapes=[buf, buf, pltpu.SemaphoreType.DMA((2, n_res_tile)),
                        pltpu.SemaphoreType.DMA((2, n_res_tile))],
        compiler_params=_cparams(("arbitrary",)),
        name="matmul_unres",
    )(a, w, res.reshape(bsz, rows, N_RES, d))
    return out.reshape(bsz * seq, d)


def _swiglu_kernel(x_ref, g_ref, wg_ref, wu_ref, wd_ref, o_ref, h_scr, acc_scr):
    j = pl.program_id(1)

    @pl.when(j == 0)
    def _():
        h_scr[...] = _rms(x_ref[...], g_ref[...]).astype(BF16)
        acc_scr[...] = jnp.zeros_like(acc_scr)

    h = h_scr[...]
    a = _dot(h, wg_ref[...])
    b = _dot(h, wu_ref[...])
    hid = (_silu(a) * b).astype(BF16)
    acc_scr[...] += _dot(hid, wd_ref[...])

    @pl.when(j == pl.num_programs(1) - 1)
    def _():
        o_ref[...] = x_ref[...] + acc_scr[...]


def _swiglu_res(x, g, wg, wu, wd, tm=512, tf=512):
    m, d = x.shape
    ff = wg.shape[1]
    return pl.pallas_call(
        _swiglu_kernel,
        grid=(m // tm, ff // tf),
        in_specs=[
            pl.BlockSpec((tm, d), lambda i, j: (i, 0)),
            pl.BlockSpec((1, d), lambda i, j: (0, 0)),
            pl.BlockSpec((d, tf), lambda i, j: (0, j)),
            pl.BlockSpec((d, tf), lambda i, j: (0, j)),
            pl.BlockSpec((tf, d), lambda i, j: (j, 0)),
        ],
        out_specs=pl.BlockSpec((tm, d), lambda i, j: (i, 0)),
        out_shape=jax.ShapeDtypeStruct((m, d), F32),
        scratch_shapes=[pltpu.VMEM((tm, d), BF16), pltpu.VMEM((tm, d), F32)],
        compiler_params=_cparams(("parallel", "arbitrary")),
        name="swiglu_res",
    )(x, g.reshape(1, d), wg, wu, wd)


def _attn_bias_table():
    blk = HEAD_DIM
    iq = np.arange(blk)[:, None]
    jk = np.arange(2 * blk)[None, :]

    def offs(idx, n_chunk, size):
        return n_chunk * (idx % size) + idx // size

    masks = []
    for n_chunk in (1, 4, 16):
        size = blk // n_chunk
        lq = offs(iq, n_chunk, size)
        masks.append((offs(jk, n_chunk, size) <= lq) & (jk < blk))
        dist = lq - (offs(jk, n_chunk, 2 * size) - blk)
        masks.append((dist >= 0) & (dist <= blk))
    return np.where(np.stack(masks), 0.0, NEG).astype(np.float32)


def _attn_pool_kernel(q_ref, k_ref, v_ref, bias, pw_ref, ps_ref, o_ref,
                      qf, kf, vf, ms, ls, accs):
    h = pl.program_id(1)
    seq = q_ref.shape[0]
    blk = HEAD_DIM
    bf16_rows = 16

    @pl.when(h < A_HEADS)
    def _attention():
        qf[...] = q_ref[...].astype(F32)
        kf[...] = k_ref[...].astype(F32)
        vf[...] = v_ref[...].astype(F32)

        def gather(ref, ref_f32, starts, size):
            if all(v % bf16_rows == 0 for v in [size] + starts):
                return jnp.concatenate([ref[s0:s0 + size, :] for s0 in starts], axis=0)
            return jnp.concatenate([ref_f32[s0:s0 + size, :] for s0 in starts],
                                   axis=0).astype(BF16)

        def scatter(br, starts, size, m, l, acc):
            mb = jnp.broadcast_to(m, acc.shape)
            lb = jnp.broadcast_to(l, acc.shape)
            for c, s0 in enumerate(starts):
                src = slice(c * size, (c + 1) * size)
                ms[br, s0:s0 + size, :] = mb[src]
                ls[br, s0:s0 + size, :] = lb[src]
                accs[br, s0:s0 + size, :] = acc[src]

        kinds = {}
        for br, n_chunk in enumerate((1, 4, 16)):
            size = blk // n_chunk
            for grp in range(N_RES // n_chunk):
                bases = [(grp + (N_RES // n_chunk) * c) * blk for c in range(n_chunk)]
                for n in range(n_chunk):
                    qs = [b0 + n * size for b0 in bases]
                    if n == 0:
                        kinds.setdefault((2 * br, blk), []).append((br, size, qs, qs, size))
                    else:
                        ks = [s0 - size for s0 in qs]
                        kinds.setdefault((2 * br + 1, 2 * blk), []).append(
                            (br, size, qs, ks, 2 * size))

        batch = 8
        for (bias_idx, kw), blocks in kinds.items():
            bi = bias[bias_idx][:, :kw]
            for g0 in range(0, len(blocks), batch):
                group = blocks[g0:g0 + batch]
                s = jnp.concatenate(
                    [_dot_nt(gather(q_ref, qf, qs, size), gather(k_ref, kf, ks, ksize))
                     for _, size, qs, ks, ksize in group], axis=0)
                s = (s.reshape(len(group), blk, kw) + bi[None]).reshape(len(group) * blk, kw)
                m = jnp.max(s, axis=-1, keepdims=True)
                p = jnp.exp2(s - m)
                l = jnp.sum(p, axis=-1, keepdims=True)
                pb = p.astype(BF16)
                for g, (br, size, qs, ks, ksize) in enumerate(group):
                    rows = slice(g * blk, (g + 1) * blk)
                    acc = _dot(pb[rows], gather(v_ref, vf, ks, ksize))
                    scatter(br, qs, size, m[rows], l[rows], acc)

        slab = 2 * blk
        for s0 in range(0, seq, slab):
            rows = slice(s0, s0 + slab)
            m_all = jnp.maximum(jnp.maximum(ms[0, rows, :], ms[1, rows, :]), ms[2, rows, :])
            num = jnp.zeros((slab, blk), F32)
            den = jnp.zeros((slab, blk), F32)
            for br in range(3):
                w = jnp.exp2(ms[br, rows, :] - m_all)
                num = num + accs[br, rows, :] * w
                den = den + ls[br, rows, :] * w
            o_ref[rows, :] = (num / den).astype(o_ref.dtype)

    for gi, w in enumerate(POOL_WINDOWS):
        @pl.when(h == A_HEADS + gi)
        def _pool(gi=gi, w=w):
            pf = q_ref[...].astype(F32)
            row = lax.broadcasted_iota(jnp.int32, pf.shape, 0)
            l_idx = row % blk
            pos = N_RES * l_idx + row // blk
            s = pf
            k = 1
            while k < w:
                y = pltpu.roll(s, k * blk, axis=0)
                y1 = jnp.where(l_idx == 0, 0.0, pltpu.roll(y, 1, axis=0))
                s = s + jnp.where(row < k * blk, y1, y)
                k *= 2
            cnt = jnp.minimum(pos + 1, w).astype(F32)
            pooled = s / cnt - pf
            y = _dot(pooled.astype(BF16), pw_ref[gi].astype(BF16)) * ps_ref[...]
            o_ref[...] = y.astype(o_ref.dtype)


def _attn_pool(proj, pool_w, pool_scale):
    bsz, seq, _ = proj.shape
    n_steps = A_HEADS + N_POOL
    qmap = lambda b, h: (b, 0, jnp.where(h < A_HEADS, h, h + 2 * A_HEADS))
    kmap = lambda b, h: (b, 0, A_HEADS + jnp.minimum(h, A_HEADS - 1))
    vmap = lambda b, h: (b, 0, 2 * A_HEADS + jnp.minimum(h, A_HEADS - 1))
    f32s = pltpu.VMEM((seq, HEAD_DIM), F32)
    br3 = pltpu.VMEM((3, seq, HEAD_DIM), F32)
    bias = jnp.asarray(_attn_bias_table())
    return pl.pallas_call(
        _attn_pool_kernel,
        grid=(bsz, n_steps),
        in_specs=[
            pl.BlockSpec((None, seq, HEAD_DIM), qmap),
            pl.BlockSpec((None, seq, HEAD_DIM), kmap),
            pl.BlockSpec((None, seq, HEAD_DIM), vmap),
            pl.BlockSpec(bias.shape, lambda b, h: (0, 0, 0)),
            pl.BlockSpec((N_POOL, HEAD_DIM, HEAD_DIM), lambda b, h: (0, 0, 0)),
            pl.BlockSpec((1, HEAD_DIM), lambda b, h: (0, jnp.maximum(h - A_HEADS, 0))),
        ],
        out_specs=pl.BlockSpec((None, seq, HEAD_DIM), lambda b, h: (b, 0, h)),
        out_shape=jax.ShapeDtypeStruct((bsz, seq, n_steps * HEAD_DIM), BF16),
        scratch_shapes=[f32s, f32s, f32s, br3, br3, br3],
        compiler_params=_cparams(("parallel", "arbitrary")),
        name="attn_pool",
    )(proj, proj, proj, bias, pool_w, pool_scale.reshape(1, N_POOL * HEAD_DIM))


def _hgrn_kernel(q_ref, f_ref, i_ref, gt_ref, lb_ref, on_ref, o_ref,
                 qe_scr, o_scr, u_scr, d_scr, s_scr):
    seq = q_ref.shape[0]
    ck = HGRN_CHUNK
    grp = 4 * ck
    n_grp = seq // grp
    n_chunk = seq // ck
    lb = lb_ref[...]

    ri = lax.broadcasted_iota(jnp.int32, (grp, grp), 0)
    ci = lax.broadcasted_iota(jnp.int32, (grp, grp), 1)
    tri_mask = (ri // ck == ci // ck) & (ci <= ri)
    tri = tri_mask.astype(BF16)

    groups = [slice(gi * grp, (gi + 1) * grp) for gi in range(n_grp)]
    q = _half_silu(q_ref[...].astype(F32))
    f = (0.5 + 0.5 * lb) + (0.5 - 0.5 * lb) * jnp.tanh(f_ref[...].astype(F32))
    k = 1.0 - f
    g = jnp.log2(f)
    g_hi = g.astype(BF16)
    g_lo = (g - g_hi.astype(F32)).astype(BF16)
    b = jnp.concatenate([_dot(tri, g_hi[r]) + _dot(tri, g_lo[r]) for r in groups], axis=0)
    b3 = b.reshape(n_chunk, ck, HEAD_DIM)
    b_mid = b3[:, ck // 2 - 1:ck // 2, :]
    b_last = b3[:, ck - 1:ck, :]
    q3 = q.reshape(n_chunk, ck, HEAD_DIM)
    k3 = k.reshape(n_chunk, ck, HEAD_DIM)
    q_mid = q3 * jnp.exp2(b3 - b_mid)
    k_mid = k3 * jnp.exp2(b_mid - b3)
    qs = q_mid.reshape(seq, HEAD_DIM).astype(BF16)
    ks = k_mid.reshape(seq, HEAD_DIM).astype(BF16)
    qe_scr[...] = (q_mid * jnp.exp2(b_mid)).reshape(seq, HEAD_DIM).astype(BF16)
    kd = (k_mid * jnp.exp2(b_last - b_mid)).reshape(seq, HEAD_DIM).astype(BF16)
    d_scr[...] = jnp.exp2(b_last).reshape(n_chunk, HEAD_DIM)
    for r in groups:
        att = jnp.where(tri_mask, _dot_nt(qs[r], ks[r]), 0.0)
        o_scr[r, :] = _dot(att.astype(BF16), i_ref[r, :])
    for c in range(n_chunk):
        cr = slice(c * ck, (c + 1) * ck)
        u_scr[c] = _dot_tn(i_ref[cr, :], kd[cr])

    st = jnp.zeros((HEAD_DIM, HEAD_DIM), F32)
    for c in range(n_chunk):
        s_scr[c] = st.astype(BF16)
        st = st * d_scr[c:c + 1, :] + u_scr[c]

    for gi in range(n_grp):
        rows = slice(gi * grp, (gi + 1) * grp)
        inter = jnp.concatenate(
            [_dot_nt(qe_scr[gi * grp + c * ck:gi * grp + (c + 1) * ck, :], s_scr[gi * 4 + c])
             for c in range(4)], axis=0)
        o = _rms(o_scr[rows, :] + inter, on_ref[...])
        o_ref[rows, :] = (o * _half_silu(gt_ref[rows, :].astype(F32))).astype(o_ref.dtype)


def _hgrn(proj, lb, out_norm):
    bsz, seq, _ = proj.shape
    nh = HGRN_HEADS
    n_chunk = seq // HGRN_CHUNK
    spec = lambda off: pl.BlockSpec((None, seq, HEAD_DIM), lambda b, h: (b, 0, off + h))
    vec = pl.BlockSpec((1, HEAD_DIM), lambda b, h: (0, h))
    return pl.pallas_call(
        _hgrn_kernel,
        grid=(bsz, nh),
        in_specs=[spec(0), spec(nh), spec(2 * nh), spec(3 * nh), vec, vec],
        out_specs=pl.BlockSpec((None, seq, HEAD_DIM), lambda b, h: (b, 0, h)),
        out_shape=jax.ShapeDtypeStruct((bsz, seq, nh * HEAD_DIM), BF16),
        scratch_shapes=[
            pltpu.VMEM((seq, HEAD_DIM), BF16),
            pltpu.VMEM((seq, HEAD_DIM), F32),
            pltpu.VMEM((n_chunk, HEAD_DIM, HEAD_DIM), F32),
            pltpu.VMEM((n_chunk, HEAD_DIM), F32),
            pltpu.VMEM((n_chunk, HEAD_DIM, HEAD_DIM), BF16),
        ],
        compiler_params=_cparams(("parallel", "arbitrary")),
        name="hgrn2",
    )(proj, proj, proj, proj, lb.reshape(1, -1), out_norm.reshape(1, -1))


def _router_kernel(x_ref, g_ref, w_ref, h_ref, meta_ref, cnt_ref, carry):
    i = pl.program_id(0)
    tm = x_ref.shape[0]

    @pl.when(i == 0)
    def _():
        carry[...] = jnp.zeros_like(carry)

    h = _rms(x_ref[...], g_ref[...])
    h_ref[...] = _pack_bf16_pair(h)
    h_hi = h.astype(BF16)
    h_lo = (h - h_hi.astype(F32)).astype(BF16)
    w = w_ref[...]
    w_hi = w.astype(BF16)
    w_lo = (w - w_hi.astype(F32)).astype(BF16)
    logits = _dot(h_hi, w_hi) + (_dot(h_hi, w_lo) + _dot(h_lo, w_hi))
    lane = lax.broadcasted_iota(jnp.int32, logits.shape, 1)
    logits = jnp.where(lane < N_EXPERTS, logits, NEG)
    m1 = jnp.max(logits, axis=-1, keepdims=True)
    i1 = jnp.min(jnp.where(logits == m1, lane, LANES), axis=-1, keepdims=True)
    rest = jnp.where(lane == i1, NEG, logits)
    m2 = jnp.max(rest, axis=-1, keepdims=True)
    i2 = jnp.min(jnp.where(rest == m2, lane, LANES), axis=-1, keepdims=True)
    e = jnp.exp(m2 - m1)
    g1 = 1.0 / (1.0 + e)
    g2 = e / (1.0 + e)
    oh1 = (lane == i1).astype(F32)
    oh2 = (lane == i2).astype(F32)
    cnt = oh1 + oh2
    ri = lax.broadcasted_iota(jnp.int32, (tm, tm), 0)
    ci = lax.broadcasted_iota(jnp.int32, (tm, tm), 1)
    before = (ci < ri).astype(BF16)
    prior = _dot(before, cnt.astype(BF16)) + carry[...]
    r1 = jnp.sum(oh1 * prior, axis=-1, keepdims=True)
    r2 = jnp.sum(oh2 * prior, axis=-1, keepdims=True)
    carry[...] += jnp.sum(cnt, axis=0, keepdims=True)
    cnt_ref[...] = carry[...]
    meta = jnp.where(lane == 0, i1.astype(F32), 0.0)
    meta = jnp.where(lane == 1, i2.astype(F32), meta)
    meta = jnp.where(lane == 2, r1, meta)
    meta = jnp.where(lane == 3, r2, meta)
    meta = jnp.where(lane == 4, g1, meta)
    meta = jnp.where(lane == 5, g2, meta)
    meta_ref[...] = meta


def _router(x, g, router_w, tm=512):
    t, d = x.shape
    w_pad = jnp.pad(router_w, ((0, 0), (0, LANES - router_w.shape[1])))
    return pl.pallas_call(
        _router_kernel,
        grid=(t // tm,),
        in_specs=[
            pl.BlockSpec((tm, d), lambda i: (i, 0)),
            pl.BlockSpec((1, d), lambda i: (0, 0)),
            pl.BlockSpec((d, LANES), lambda i: (0, 0)),
        ],
        out_specs=[
            pl.BlockSpec((tm, d // 2), lambda i: (i, 0)),
            pl.BlockSpec((tm, LANES), lambda i: (i, 0)),
            pl.BlockSpec((1, LANES), lambda i: (0, 0)),
        ],
        out_shape=[
            jax.ShapeDtypeStruct((t, d // 2), jnp.uint32),
            jax.ShapeDtypeStruct((t, LANES), F32),
            jax.ShapeDtypeStruct((1, LANES), F32),
        ],
        scratch_shapes=[pltpu.VMEM((1, LANES), F32)],
        compiler_params=_cparams(("arbitrary",)),
        name="moe_router",
    )(x, g.reshape(1, d), w_pad)


def _row_copy(src, dst, s, d, sem):
    return pltpu.make_async_copy(src.at[pl.ds(s, 1)], dst.at[pl.ds(d, 1)], sem)


def _dispatch_kernel(dest_ref, h_ref, o_hbm, sem):
    tm = h_ref.shape[0]

    def start(t, carry):
        _row_copy(h_ref, o_hbm, t, dest_ref[0, 0, 2 * t], sem).start(priority=0)
        _row_copy(h_ref, o_hbm, t, dest_ref[0, 0, 2 * t + 1], sem).start(priority=1)
        return carry

    lax.fori_loop(0, tm, start, 0, unroll=8)

    def wait(t, carry):
        _row_copy(h_ref, o_hbm, 0, 0, sem).wait()
        _row_copy(h_ref, o_hbm, 0, 0, sem).wait()
        return carry

    lax.fori_loop(0, tm, wait, 0, unroll=8)


def _dispatch(h, dest, n_rows, tm=512):
    t, d = h.shape
    dest3 = dest.reshape(t // tm, 1, 2 * tm)
    return pl.pallas_call(
        _dispatch_kernel,
        grid=(t // tm,),
        in_specs=[
            pl.BlockSpec((1, 1, 2 * tm), lambda i: (i, 0, 0), memory_space=pltpu.SMEM),
            pl.BlockSpec((tm, d), lambda i: (i, 0)),
        ],
        out_specs=pl.BlockSpec(memory_space=pl.ANY),
        out_shape=jax.ShapeDtypeStruct((n_rows, d), h.dtype),
        scratch_shapes=[pltpu.SemaphoreType.DMA(())],
        compiler_params=_cparams(("arbitrary",)),
        name="moe_dispatch",
    )(dest3, h)


def _moe_kernel(be_ref, bb_ref, nv_ref, na_ref, x_ref, wg_ref, wu_ref, wd_ref, o_ref, xb_scr):
    i = pl.program_id(0)
    j = pl.program_id(1)
    n_valid = nv_ref[i]

    @pl.when(i < na_ref[0])
    def _():
        @pl.when(j == 0)
        def _():
            row = lax.broadcasted_iota(jnp.int32, x_ref.shape, 0)
            lo, hi = _unpack_bf16_pair(jnp.where(row < n_valid, x_ref[...], jnp.uint32(0)))
            xb_scr[:, :lo.shape[1]] = lo
            xb_scr[:, lo.shape[1]:] = hi
            o_ref[...] = jnp.zeros_like(o_ref)

        def run(rows):
            xb = xb_scr[rows, :]
            a = _dot(xb, wg_ref[0].astype(BF16))
            b = _dot(xb, wu_ref[0].astype(BF16))
            hid = (_silu(a) * b).astype(BF16)
            o_ref[rows, :] += _dot(hid, wd_ref[0].astype(BF16))

        part = x_ref.shape[0] // MOE_PARTS
        for nq in range(1, MOE_PARTS + 1):
            @pl.when((n_valid > (nq - 1) * part) & (n_valid <= nq * part))
            def _(nq=nq):
                run(slice(0, nq * part))


def _moe_experts(x_rows, wg, wu, wd, blk_e, blk_idx, blk_valid, n_live, n_blocks):
    dp = x_rows.shape[1]
    d = 2 * dp
    dff = wg.shape[2]
    n_ff = dff // MOE_TF
    r = MOE_ROWS

    def ffi(i, j, na):
        return jnp.where(i < na[0], j, n_ff - 1)

    grid_spec = pltpu.PrefetchScalarGridSpec(
        num_scalar_prefetch=4,
        grid=(n_blocks, n_ff),
        in_specs=[
            pl.BlockSpec((r, dp), lambda i, j, be, bb, nv, na: (bb[i], 0)),
            pl.BlockSpec((1, d, MOE_TF), lambda i, j, be, bb, nv, na: (be[i], 0, ffi(i, j, na))),
            pl.BlockSpec((1, d, MOE_TF), lambda i, j, be, bb, nv, na: (be[i], 0, ffi(i, j, na))),
            pl.BlockSpec((1, MOE_TF, d), lambda i, j, be, bb, nv, na: (be[i], ffi(i, j, na), 0)),
        ],
        out_specs=pl.BlockSpec((r, d), lambda i, j, be, bb, nv, na: (bb[i], 0)),
        scratch_shapes=[pltpu.VMEM((r, d), BF16)],
    )
    return pl.pallas_call(
        _moe_kernel,
        grid_spec=grid_spec,
        out_shape=jax.ShapeDtypeStruct((x_rows.shape[0], d), F32),
        compiler_params=_cparams(("arbitrary", "arbitrary")),
        name="moe_experts",
    )(blk_e, blk_idx, blk_valid, n_live, x_rows, wg, wu, wd)


def _combine_kernel(dest_ref, next_ref, x_ref, meta_ref, nf_ref, y_hbm, o_ref, buf, sem):
    i = pl.program_id(0)
    tm = x_ref.shape[0]
    slot = i % 2

    def gather(idx_ref, s):
        def start(t, carry):
            _row_copy(y_hbm, buf.at[s, 0], idx_ref[0, 0, 2 * t], t, sem.at[s]).start(priority=0)
            _row_copy(y_hbm, buf.at[s, 1], idx_ref[0, 0, 2 * t + 1], t, sem.at[s]).start(priority=1)
            return carry

        lax.fori_loop(0, tm, start, 0, unroll=8)

    @pl.when(i == 0)
    def _():
        gather(dest_ref, slot)

    @pl.when(i + 1 < pl.num_programs(0))
    def _():
        gather(next_ref, 1 - slot)

    def wait(t, carry):
        _row_copy(y_hbm, buf.at[slot, 0], 0, 0, sem.at[slot]).wait()
        _row_copy(y_hbm, buf.at[slot, 1], 0, 0, sem.at[slot]).wait()
        return carry

    lax.fori_loop(0, tm, wait, 0, unroll=8)

    meta = meta_ref[...]
    g1 = meta[:, 4:5]
    g2 = meta[:, 5:6]
    x = x_ref[...] + (g1 * buf[slot, 0] + g2 * buf[slot, 1])
    o_ref[...] = _rms(x, nf_ref[...])


def _combine(x, meta, norm_final, y_rows, dest, tm=512):
    t, d = x.shape
    n_steps = t // tm
    dest3 = dest.reshape(n_steps, 1, 2 * tm)
    return pl.pallas_call(
        _combine_kernel,
        grid=(n_steps,),
        in_specs=[
            pl.BlockSpec((1, 1, 2 * tm), lambda i: (i, 0, 0), memory_space=pltpu.SMEM),
            pl.BlockSpec((1, 1, 2 * tm), lambda i: (jnp.minimum(i + 1, n_steps - 1), 0, 0),
                         memory_space=pltpu.SMEM),
            pl.BlockSpec((tm, d), lambda i: (i, 0)),
            pl.BlockSpec((tm, LANES), lambda i: (i, 0)),
            pl.BlockSpec((1, d), lambda i: (0, 0)),
            pl.BlockSpec(memory_space=pl.ANY),
        ],
        out_specs=pl.BlockSpec((tm, d), lambda i: (i, 0)),
        out_shape=jax.ShapeDtypeStruct((t, d), F32),
        scratch_shapes=[pltpu.VMEM((2, 2, tm, d), F32), pltpu.SemaphoreType.DMA((2,))],
        compiler_params=_cparams(("arbitrary",)),
        name="moe_combine",
    )(dest3, dest3, x, meta, norm_final.reshape(1, d), y_rows)


def _moe_block_table(counts, t):
    r = MOE_ROWS
    n_blocks = (2 * t) // r + N_EXPERTS
    nblk = (counts + r - 1) // r
    ends = jnp.cumsum(nblk)
    starts = ends - nblk
    n_live = ends[-1]
    step = jnp.minimum(jnp.arange(n_blocks, dtype=jnp.int32), n_live - 1)
    blk_e = jnp.sum(step[:, None] >= ends[None, :], axis=1).astype(jnp.int32)
    local = step - starts[blk_e]
    blk_valid = jnp.clip(counts[blk_e] - local * r, 0, r)
    return (starts * r, blk_e, step, blk_valid.astype(jnp.int32),
            n_live.reshape(1).astype(jnp.int32), n_blocks)


def kernel(x, norm_mix, norm_ffn, w_in_even, pool_w, pool_scale, w_out_even, ffn_w_gate,
           ffn_w_up, ffn_w_down, w_in_odd, lower_bound_logits, hgrn_out_norm, w_out_odd,
           router_w, moe_w_gate, moe_w_up, moe_w_down, norm_final):
    bsz, seq, d = x.shape
    t = bsz * seq

    lb_all = jnp.cumsum(jax.nn.softmax(lower_bound_logits.astype(F32), axis=0), axis=0)
    lb_all = lb_all - lb_all[:1]

    n_in = w_in_even.shape[2]
    q_scale = math.log2(math.e) / math.sqrt(HEAD_DIM)
    col_scale = jnp.where(jnp.arange(n_in) < A_WIDTH, q_scale, 1.0).astype(F32)
    proj = _norm_matmul_res(x, norm_mix[0], w_in_even[0].astype(BF16), col_scale)
    mix = _attn_pool(proj.reshape(bsz, seq, -1), pool_w[0], pool_scale[0])
    xt = _matmul_unres(mix.reshape(t, -1), w_out_even[0].astype(BF16), x)
    xt = _swiglu_res(xt, norm_ffn[0], ffn_w_gate[0].astype(BF16), ffn_w_up[0].astype(BF16),
                     ffn_w_down[0].astype(BF16))

    wid = HGRN_HEADS * HEAD_DIM
    col = jnp.arange(4 * wid) // wid
    proj = _norm_matmul(xt, norm_mix[1], w_in_odd[0].astype(BF16),
                        jnp.where(col == 2, 1.0, 0.5).astype(F32))
    o = _hgrn(proj.reshape(bsz, seq, -1), lb_all[1], hgrn_out_norm[0])
    xt = _matmul_res(o.reshape(t, -1), w_out_odd[0].astype(BF16), xt)

    h, meta, cnt = _router(xt, norm_ffn[1], router_w[0])
    counts = cnt[0, :N_EXPERTS].astype(jnp.int32)
    row0, blk_e, blk_idx, blk_valid, n_live, n_blocks = _moe_block_table(counts, t)
    dest = (row0[meta[:, 0:2].astype(jnp.int32)] + meta[:, 2:4].astype(jnp.int32)).reshape(-1)
    x_rows = _dispatch(h, dest, n_blocks * MOE_ROWS)
    y_rows = _moe_experts(x_rows, moe_w_gate[0], moe_w_up[0], moe_w_down[0],
                          blk_e, blk_idx, blk_valid, n_live, n_blocks)
    out = _combine(xt, meta, norm_final, y_rows, dest)
    return out.reshape(bsz, seq, d)
```

```python
import math

import jax
import jax.numpy as jnp
import numpy as np
from jax import lax
from jax.experimental import pallas as pl
from jax.experimental.pallas import tpu as pltpu

F32 = jnp.float32
BF16 = jnp.bfloat16

D_MODEL = 2048
HEAD_DIM = 128
A_HEADS = 12
A_WIDTH = A_HEADS * HEAD_DIM
N_POOL = 4
POOL_WINDOWS = (2, 4, 8, 16)
HGRN_HEADS = 16
HGRN_CHUNK = 64
N_EXPERTS = 8
EPS = 1e-6
NEG = -1e30

LANES = 128
VMEM_LIMIT = 56 * 1024 * 1024

MOE_ROWS = 1024
MOE_PARTS = 4
MOE_TF = 512
BIG_VMEM_LIMIT = 62 * 1024 * 1024
HI16 = 0xFFFF0000


def _pack_bf16_pair(x):
    n = x.shape[1] // 2
    bits = lax.bitcast_convert_type(x.astype(BF16).astype(F32), jnp.uint32)
    return (bits[:, n:] & jnp.uint32(HI16)) | (bits[:, :n] >> 16)


def _unpack_bf16_pair(p):
    lo = lax.bitcast_convert_type(p << 16, F32).astype(BF16)
    hi = lax.bitcast_convert_type(p & jnp.uint32(HI16), F32).astype(BF16)
    return lo, hi

N_RES = 16


def _cparams(sem, vmem=VMEM_LIMIT, flags=None):
    return pltpu.CompilerParams(dimension_semantics=sem, vmem_limit_bytes=vmem, flags=flags)


def _rms(x, g):
    ms = jnp.mean(x * x, axis=-1, keepdims=True)
    return x * lax.rsqrt(ms + EPS) * g


def _half_silu(h):
    return h + h * jnp.tanh(h)


def _silu(x):
    return _half_silu(0.5 * x)


def _dot(a, b):
    return jnp.dot(a, b, preferred_element_type=F32)


def _dot_nt(a, b):
    return lax.dot_general(a, b, (((1,), (1,)), ((), ())), preferred_element_type=F32)


def _dot_tn(a, b):
    return lax.dot_general(a, b, (((0,), (0,)), ((), ())), preferred_element_type=F32)


def _norm_matmul_kernel(x_ref, g_ref, w_ref, cs_ref, o_ref, h_scr):
    @pl.when(pl.program_id(1) == 0)
    def _():
        h_scr[...] = _rms(x_ref[...], g_ref[...]).astype(BF16)

    o_ref[...] = (_dot(h_scr[...], w_ref[...]) * cs_ref[...]).astype(o_ref.dtype)


def _norm_matmul(x, g, w, col_scale, tm=1024, tn=2048):
    m, d = x.shape
    n = w.shape[1]
    return pl.pallas_call(
        _norm_matmul_kernel,
        grid=(m // tm, n // tn),
        in_specs=[
            pl.BlockSpec((tm, d), lambda i, j: (i, 0)),
            pl.BlockSpec((1, d), lambda i, j: (0, 0)),
            pl.BlockSpec((d, tn), lambda i, j: (0, j)),
            pl.BlockSpec((1, tn), lambda i, j: (0, j)),
        ],
        out_specs=pl.BlockSpec((tm, tn), lambda i, j: (i, j)),
        out_shape=jax.ShapeDtypeStruct((m, n), BF16),
        scratch_shapes=[pltpu.VMEM((tm, d), BF16)],
        compiler_params=_cparams(("parallel", "arbitrary")),
        name="norm_matmul",
    )(x, g.reshape(1, d), w, col_scale.reshape(1, n))


def _norm_matmul_res_kernel(x_hbm, g_ref, w_ref, cs_ref, o_ref, xbuf, h_scr, sem):
    i = pl.program_id(0)
    n_res_tile, rows, _ = xbuf.shape[1:]
    halves = N_RES // n_res_tile

    def fetch(step, slot):
        b = step // halves
        r0 = (step % halves) * n_res_tile
        return [pltpu.make_async_copy(x_hbm.at[b, :, r0 + r, :], xbuf.at[slot, r], sem.at[slot, r])
                for r in range(n_res_tile)]

    @pl.when(pl.program_id(1) == 0)
    def _():
        slot = i % 2

        @pl.when(i == 0)
        def _():
            for c in fetch(i, slot):
                c.start()

        @pl.when(i + 1 < pl.num_programs(0))
        def _():
            for c in fetch(i + 1, 1 - slot):
                c.start()

        for r, c in enumerate(fetch(i, slot)):
            c.wait()
            h_scr[r * rows:(r + 1) * rows, :] = _rms(xbuf[slot, r], g_ref[...]).astype(BF16)

    o_ref[...] = (_dot(h_scr[...], w_ref[...]) * cs_ref[...]).astype(o_ref.dtype)


def _norm_matmul_res(x, g, w, col_scale, n_res_tile=8, tn=2560):
    bsz, seq, d = x.shape
    n = w.shape[1]
    rows = seq // N_RES
    tm = n_res_tile * rows
    halves = N_RES // n_res_tile
    return pl.pallas_call(
        _norm_matmul_res_kernel,
        grid=(bsz * halves, n // tn),
        in_specs=[
            pl.BlockSpec(memory_space=pl.ANY),
            pl.BlockSpec((1, d), lambda i, j: (0, 0)),
            pl.BlockSpec((d, tn), lambda i, j: (0, j)),
            pl.BlockSpec((1, tn), lambda i, j: (0, j)),
        ],
        out_specs=pl.BlockSpec((tm, tn), lambda i, j: (i, j)),
        out_shape=jax.ShapeDtypeStruct((bsz * seq, n), BF16),
        scratch_shapes=[pltpu.VMEM((2, n_res_tile, rows, d), F32), pltpu.VMEM((tm, d), BF16),
                        pltpu.SemaphoreType.DMA((2, n_res_tile))],
        compiler_params=_cparams(("arbitrary", "arbitrary")),
        name="norm_matmul_res",
    )(x.reshape(bsz, rows, N_RES, d), g.reshape(1, d), w, col_scale.reshape(1, n))


def _matmul_res_kernel(a_ref, w_ref, r_ref, o_ref):
    o_ref[...] = r_ref[...] + _dot(a_ref[...], w_ref[...])


def _matmul_res(a, w, res, tm=512, tn=2048):
    m, k = a.shape
    n = w.shape[1]
    return pl.pallas_call(
        _matmul_res_kernel,
        grid=(m // tm, n // tn),
        in_specs=[
            pl.BlockSpec((tm, k), lambda i, j: (i, 0)),
            pl.BlockSpec((k, tn), lambda i, j: (0, j)),
            pl.BlockSpec((tm, tn), lambda i, j: (i, j)),
        ],
        out_specs=pl.BlockSpec((tm, tn), lambda i, j: (i, j)),
        out_shape=jax.ShapeDtypeStruct((m, n), F32),
        compiler_params=_cparams(("parallel", "arbitrary")),
        name="matmul_res",
    )(a, w, res)


def _matmul_unres_kernel(a_ref, w_ref, r_hbm, o_hbm, rbuf, obuf, rsem, osem):
    i = pl.program_id(0)
    n_steps = pl.num_programs(0)
    n_res_tile, rows, _ = rbuf.shape[1:]
    parts = N_RES // n_res_tile

    def res_fetch(step, slot):
        b, r0 = step // parts, (step % parts) * n_res_tile
        return [pltpu.make_async_copy(r_hbm.at[b, :, r0 + r, :], rbuf.at[slot, r], rsem.at[slot, r])
                for r in range(n_res_tile)]

    def out_store(step, slot):
        b, r0 = step // parts, (step % parts) * n_res_tile
        return [pltpu.make_async_copy(obuf.at[slot, r], o_hbm.at[b, :, r0 + r, :], osem.at[slot, r])
                for r in range(n_res_tile)]

    slot = i % 2

    @pl.when(i == 0)
    def _():
        for c in res_fetch(i, slot):
            c.start()

    @pl.when(i + 1 < n_steps)
    def _():
        for c in res_fetch(i + 1, 1 - slot):
            c.start()

    y = _dot(a_ref[...], w_ref[...])

    @pl.when(i >= 2)
    def _():
        for c in out_store(i - 2, slot):
            c.wait()

    for r, c in enumerate(res_fetch(i, slot)):
        c.wait()
        obuf[slot, r] = rbuf[slot, r] + y[r * rows:(r + 1) * rows, :]
    for c in out_store(i, slot):
        c.start()

    @pl.when(i == n_steps - 1)
    def _():
        for c in out_store(i - 1, 1 - slot) + out_store(i, slot):
            c.wait()


def _matmul_unres(a, w, res, n_res_tile=4):
    bsz, seq, d = res.shape
    k = a.shape[1]
    rows = seq // N_RES
    tm = n_res_tile * rows
    parts = N_RES // n_res_tile
    assert bsz * parts >= 2
    buf = pltpu.VMEM((2, n_res_tile, rows, d), F32)
    out = pl.pallas_call(
        _matmul_unres_kernel,
        grid=(bsz * parts,),
        in_specs=[
            pl.BlockSpec((tm, k), lambda i: (i, 0)),
            pl.BlockSpec((k, d), lambda i: (0, 0)),
            pl.BlockSpec(memory_space=pl.ANY),
        ],
        out_specs=pl.BlockSpec(memory_space=pl.ANY),
        out_shape=jax.ShapeDtypeStruct((bsz, rows, N_RES, d), F32),
        scratch_shapes=[buf, buf, pltpu.SemaphoreType.DMA((2, n_res_tile)),
                        pltpu.SemaphoreType.DMA((2, n_res_tile))],
        compiler_params=_cparams(("arbitrary",)),
        name="matmul_unres",
    )(a, w, res.reshape(bsz, rows, N_RES, d))
    return out.reshape(bsz * seq, d)


def _swiglu_kernel(x_ref, g_ref, wg_ref, wu_ref, wd_ref, o_ref, h_scr):
    @pl.when(pl.program_id(1) == 0)
    def _():
        h_scr[...] = _rms(x_ref[...], g_ref[...]).astype(BF16)
        o_ref[...] = x_ref[...]

    h = h_scr[...]
    a = _dot(h, wg_ref[...])
    b = _dot(h, wu_ref[...])
    hid = (_silu(a) * b).astype(BF16)
    o_ref[...] += _dot(hid, wd_ref[...])


def _swiglu_res(x, g, wg, wu, wd, tm=1024, tf=512):
    m, d = x.shape
    ff = wg.shape[1]
    return pl.pallas_call(
        _swiglu_kernel,
        grid=(m // tm, ff // tf),
        in_specs=[
            pl.BlockSpec((tm, d), lambda i, j: (i, 0)),
            pl.BlockSpec((1, d), lambda i, j: (0, 0)),
            pl.BlockSpec((d, tf), lambda i, j: (0, j)),
            pl.BlockSpec((d, tf), lambda i, j: (0, j)),
            pl.BlockSpec((tf, d), lambda i, j: (j, 0)),
        ],
        out_specs=pl.BlockSpec((tm, d), lambda i, j: (i, 0)),
        out_shape=jax.ShapeDtypeStruct((m, d), F32),
        scratch_shapes=[pltpu.VMEM((tm, d), BF16)],
        compiler_params=_cparams(("parallel", "arbitrary"), vmem=BIG_VMEM_LIMIT),
        name="swiglu_res",
    )(x, g.reshape(1, d), wg, wu, wd)


def _attn_bias_table():
    blk = HEAD_DIM
    iq = np.arange(blk)[:, None]
    jk = np.arange(2 * blk)[None, :]

    def offs(idx, n_chunk, size):
        return n_chunk * (idx % size) + idx // size

    masks = []
    for n_chunk in (1, 4, 16):
        size = blk // n_chunk
        lq = offs(iq, n_chunk, size)
        masks.append((offs(jk, n_chunk, size) <= lq) & (jk < blk))
        dist = lq - (offs(jk, n_chunk, 2 * size) - blk)
        masks.append((dist >= 0) & (dist <= blk))
    return np.where(np.stack(masks), 0.0, NEG).astype(np.float32)


def _attn_pool_kernel(q_ref, k_ref, v_ref, bias, pw_ref, ps_ref, o_ref,
                      qf, kf, vf, ms, ls, accs):
    h = pl.program_id(1)
    seq = q_ref.shape[0]
    blk = HEAD_DIM
    bf16_rows = 16

    @pl.when(h < A_HEADS)
    def _attention():
        qf[...] = q_ref[...].astype(F32)
        kf[...] = k_ref[...].astype(F32)
        vf[...] = v_ref[...].astype(F32)

        def gather(ref, ref_f32, starts, size):
            if all(v % bf16_rows == 0 for v in [size] + starts):
                return jnp.concatenate([ref[s0:s0 + size, :] for s0 in starts], axis=0)
            return jnp.concatenate([ref_f32[s0:s0 + size, :] for s0 in starts],
                                   axis=0).astype(BF16)

        def scatter(br, starts, size, m, l, acc):
            mb = jnp.broadcast_to(m, acc.shape)
            lb = jnp.broadcast_to(l, acc.shape)
            for c, s0 in enumerate(starts):
                src = slice(c * size, (c + 1) * size)
                ms[br, s0:s0 + size, :] = mb[src]
                ls[br, s0:s0 + size, :] = lb[src]
                accs[br, s0:s0 + size, :] = acc[src]

        kinds = {}
        for br, n_chunk in enumerate((1, 4, 16)):
            size = blk // n_chunk
            for grp in range(N_RES // n_chunk):
                bases = [(grp + (N_RES // n_chunk) * c) * blk for c in range(n_chunk)]
                for n in range(n_chunk):
                    qs = [b0 + n * size for b0 in bases]
                    if n == 0:
                        kinds.setdefault((2 * br, blk), []).append((br, size, qs, qs, size))
                    else:
                        ks = [s0 - size for s0 in qs]
                        kinds.setdefault((2 * br + 1, 2 * blk), []).append(
                            (br, size, qs, ks, 2 * size))

        batch = 8
        for (bias_idx, kw), blocks in kinds.items():
            bi = bias[bias_idx][:, :kw]
            for g0 in range(0, len(blocks), batch):
                group = blocks[g0:g0 + batch]
                s = jnp.concatenate(
                    [_dot_nt(gather(q_ref, qf, qs, size), gather(k_ref, kf, ks, ksize))
                     for _, size, qs, ks, ksize in group], axis=0)
                s = (s.reshape(len(group), blk, kw) + bi[None]).reshape(len(group) * blk, kw)
                m = jnp.max(s, axis=-1, keepdims=True)
                p = jnp.exp2(s - m)
                l = jnp.sum(p, axis=-1, keepdims=True)
                pb = p.astype(BF16)
                for g, (br, size, qs, ks, ksize) in enumerate(group):
                    rows = slice(g * blk, (g + 1) * blk)
                    acc = _dot(pb[rows], gather(v_ref, vf, ks, ksize))
                    scatter(br, qs, size, m[rows], l[rows], acc)

        slab = 2 * blk
        for s0 in range(0, seq, slab):
            rows = slice(s0, s0 + slab)
            m_all = jnp.maximum(jnp.maximum(ms[0, rows, :], ms[1, rows, :]), ms[2, rows, :])
            num = jnp.zeros((slab, blk), F32)
            den = jnp.zeros((slab, blk), F32)
            for br in range(3):
                w = jnp.exp2(ms[br, rows, :] - m_all)
                num = num + accs[br, rows, :] * w
                den = den + ls[br, rows, :] * w
            o_ref[rows, :] = (num / den).astype(o_ref.dtype)

    for gi, w in enumerate(POOL_WINDOWS):
        @pl.when(h == A_HEADS + gi)
        def _pool(gi=gi, w=w):
            pf = q_ref[...].astype(F32)
            row = lax.broadcasted_iota(jnp.int32, pf.shape, 0)
            l_idx = row % blk
            pos = N_RES * l_idx + row // blk
            s = pf
            k = 1
            while k < w:
                y = pltpu.roll(s, k * blk, axis=0)
                y1 = jnp.where(l_idx == 0, 0.0, pltpu.roll(y, 1, axis=0))
                s = s + jnp.where(row < k * blk, y1, y)
                k *= 2
            cnt = jnp.minimum(pos + 1, w).astype(F32)
            pooled = s / cnt - pf
            y = _dot(pooled.astype(BF16), pw_ref[gi].astype(BF16)) * ps_ref[...]
            o_ref[...] = y.astype(o_ref.dtype)


def _attn_pool(proj, pool_w, pool_scale):
    bsz, seq, _ = proj.shape
    n_steps = A_HEADS + N_POOL
    qmap = lambda b, h: (b, 0, jnp.where(h < A_HEADS, h, h + 2 * A_HEADS))
    kmap = lambda b, h: (b, 0, A_HEADS + jnp.minimum(h, A_HEADS - 1))
    vmap = lambda b, h: (b, 0, 2 * A_HEADS + jnp.minimum(h, A_HEADS - 1))
    f32s = pltpu.VMEM((seq, HEAD_DIM), F32)
    br3 = pltpu.VMEM((3, seq, HEAD_DIM), F32)
    bias = jnp.asarray(_attn_bias_table())
    return pl.pallas_call(
        _attn_pool_kernel,
        grid=(bsz, n_steps),
        in_specs=[
            pl.BlockSpec((None, seq, HEAD_DIM), qmap),
            pl.BlockSpec((None, seq, HEAD_DIM), kmap),
            pl.BlockSpec((None, seq, HEAD_DIM), vmap),
            pl.BlockSpec(bias.shape, lambda b, h: (0, 0, 0)),
            pl.BlockSpec((N_POOL, HEAD_DIM, HEAD_DIM), lambda b, h: (0, 0, 0)),
            pl.BlockSpec((1, HEAD_DIM), lambda b, h: (0, jnp.maximum(h - A_HEADS, 0))),
        ],
        out_specs=pl.BlockSpec((None, seq, HEAD_DIM), lambda b, h: (b, 0, h)),
        out_shape=jax.ShapeDtypeStruct((bsz, seq, n_steps * HEAD_DIM), BF16),
        scratch_shapes=[f32s, f32s, f32s, br3, br3, br3],
        compiler_params=_cparams(("parallel", "arbitrary")),
        name="attn_pool",
    )(proj, proj, proj, bias, pool_w, pool_scale.reshape(1, N_POOL * HEAD_DIM))


def _hgrn_kernel(q_ref, f_ref, i_ref, gt_ref, lb_ref, on_ref, o_ref,
                 qe_scr, o_scr, u_scr, d_scr, s_scr):
    seq = q_ref.shape[0]
    ck = HGRN_CHUNK
    grp = 4 * ck
    n_grp = seq // grp
    n_chunk = seq // ck
    lb = lb_ref[...]

    ri = lax.broadcasted_iota(jnp.int32, (grp, grp), 0)
    ci = lax.broadcasted_iota(jnp.int32, (grp, grp), 1)
    tri_mask = (ri // ck == ci // ck) & (ci <= ri)
    tri = tri_mask.astype(BF16)

    groups = [slice(gi * grp, (gi + 1) * grp) for gi in range(n_grp)]
    q = _half_silu(q_ref[...].astype(F32))
    f = (0.5 + 0.5 * lb) + (0.5 - 0.5 * lb) * jnp.tanh(f_ref[...].astype(F32))
    k = 1.0 - f
    g = jnp.log2(f)
    g_hi = g.astype(BF16)
    g_lo = (g - g_hi.astype(F32)).astype(BF16)
    b = jnp.concatenate([_dot(tri, g_hi[r]) + _dot(tri, g_lo[r]) for r in groups], axis=0)
    b3 = b.reshape(n_chunk, ck, HEAD_DIM)
    b_mid = b3[:, ck // 2 - 1:ck // 2, :]
    b_last = b3[:, ck - 1:ck, :]
    q3 = q.reshape(n_chunk, ck, HEAD_DIM)
    k3 = k.reshape(n_chunk, ck, HEAD_DIM)
    q_mid = q3 * jnp.exp2(b3 - b_mid)
    k_mid = k3 * jnp.exp2(b_mid - b3)
    qs = q_mid.reshape(seq, HEAD_DIM).astype(BF16)
    ks = k_mid.reshape(seq, HEAD_DIM).astype(BF16)
    qe_scr[...] = (q_mid * jnp.exp2(b_mid)).reshape(seq, HEAD_DIM).astype(BF16)
    kd = (k_mid * jnp.exp2(b_last - b_mid)).reshape(seq, HEAD_DIM).astype(BF16)
    d_scr[...] = jnp.exp2(b_last).reshape(n_chunk, HEAD_DIM)
    for r in groups:
        att = jnp.where(tri_mask, _dot_nt(qs[r], ks[r]), 0.0)
        o_scr[r, :] = _dot(att.astype(BF16), i_ref[r, :])
    for c in range(n_chunk):
        cr = slice(c * ck, (c + 1) * ck)
        u_scr[c] = _dot_tn(i_ref[cr, :], kd[cr])

    st = jnp.zeros((HEAD_DIM, HEAD_DIM), F32)
    for c in range(n_chunk):
        s_scr[c] = st.astype(BF16)
        st = st * d_scr[c:c + 1, :] + u_scr[c]

    for gi in range(n_grp):
        rows = slice(gi * grp, (gi + 1) * grp)
        inter = jnp.concatenate(
            [_dot_nt(qe_scr[gi * grp + c * ck:gi * grp + (c + 1) * ck, :], s_scr[gi * 4 + c])
             for c in range(4)], axis=0)
        o = _rms(o_scr[rows, :] + inter, on_ref[...])
        o_ref[rows, :] = (o * _half_silu(gt_ref[rows, :].astype(F32))).astype(o_ref.dtype)


def _hgrn(proj, lb, out_norm):
    bsz, seq, _ = proj.shape
    nh = HGRN_HEADS
    n_chunk = seq // HGRN_CHUNK
    spec = lambda off: pl.BlockSpec((None, seq, HEAD_DIM), lambda b, h: (b, 0, off + h))
    vec = pl.BlockSpec((1, HEAD_DIM), lambda b, h: (0, h))
    return pl.pallas_call(
        _hgrn_kernel,
        grid=(bsz, nh),
        in_specs=[spec(0), spec(nh), spec(2 * nh), spec(3 * nh), vec, vec],
        out_specs=pl.BlockSpec((None, seq, HEAD_DIM), lambda b, h: (b, 0, h)),
        out_shape=jax.ShapeDtypeStruct((bsz, seq, nh * HEAD_DIM), BF16),
        scratch_shapes=[
            pltpu.VMEM((seq, HEAD_DIM), BF16),
            pltpu.VMEM((seq, HEAD_DIM), F32),
            pltpu.VMEM((n_chunk, HEAD_DIM, HEAD_DIM), F32),
            pltpu.VMEM((n_chunk, HEAD_DIM), F32),
            pltpu.VMEM((n_chunk, HEAD_DIM, HEAD_DIM), BF16),
        ],
        compiler_params=_cparams(("parallel", "arbitrary")),
        name="hgrn2",
    )(proj, proj, proj, proj, lb.reshape(1, -1), out_norm.reshape(1, -1))


def _router_kernel(x_ref, g_ref, w_ref, h_ref, meta_ref, cnt_ref, carry):
    i = pl.program_id(0)
    tm = x_ref.shape[0]

    @pl.when(i == 0)
    def _():
        carry[...] = jnp.zeros_like(carry)

    h = _rms(x_ref[...], g_ref[...])
    h_ref[...] = _pack_bf16_pair(h)
    h_hi = h.astype(BF16)
    h_lo = (h - h_hi.astype(F32)).astype(BF16)
    w = w_ref[...]
    w_hi = w.astype(BF16)
    w_lo = (w - w_hi.astype(F32)).astype(BF16)
    logits = _dot(h_hi, w_hi) + (_dot(h_hi, w_lo) + _dot(h_lo, w_hi))
    lane = lax.broadcasted_iota(jnp.int32, logits.shape, 1)
    logits = jnp.where(lane < N_EXPERTS, logits, NEG)
    m1 = jnp.max(logits, axis=-1, keepdims=True)
    i1 = jnp.min(jnp.where(logits == m1, lane, LANES), axis=-1, keepdims=True)
    rest = jnp.where(lane == i1, NEG, logits)
    m2 = jnp.max(rest, axis=-1, keepdims=True)
    i2 = jnp.min(jnp.where(rest == m2, lane, LANES), axis=-1, keepdims=True)
    e = jnp.exp(m2 - m1)
    g1 = 1.0 / (1.0 + e)
    g2 = e / (1.0 + e)
    oh1 = (lane == i1).astype(F32)
    oh2 = (lane == i2).astype(F32)
    cnt = oh1 + oh2
    ri = lax.broadcasted_iota(jnp.int32, (tm, tm), 0)
    ci = lax.broadcasted_iota(jnp.int32, (tm, tm), 1)
    before = (ci < ri).astype(BF16)
    prior = _dot(before, cnt.astype(BF16)) + carry[...]
    r1 = jnp.sum(oh1 * prior, axis=-1, keepdims=True)
    r2 = jnp.sum(oh2 * prior, axis=-1, keepdims=True)
    carry[...] += jnp.sum(cnt, axis=0, keepdims=True)
    cnt_ref[...] = carry[...]
    meta = jnp.where(lane == 0, i1.astype(F32), 0.0)
    meta = jnp.where(lane == 1, i2.astype(F32), meta)
    meta = jnp.where(lane == 2, r1, meta)
    meta = jnp.where(lane == 3, r2, meta)
    meta = jnp.where(lane == 4, g1, meta)
    meta = jnp.where(lane == 5, g2, meta)
    meta_ref[...] = meta


def _router(x, g, router_w, tm=512):
    t, d = x.shape
    w_pad = jnp.pad(router_w, ((0, 0), (0, LANES - router_w.shape[1])))
    return pl.pallas_call(
        _router_kernel,
        grid=(t // tm,),
        in_specs=[
            pl.BlockSpec((tm, d), lambda i: (i, 0)),
            pl.BlockSpec((1, d), lambda i: (0, 0)),
            pl.BlockSpec((d, LANES), lambda i: (0, 0)),
        ],
        out_specs=[
            pl.BlockSpec((tm, d // 2), lambda i: (i, 0)),
            pl.BlockSpec((tm, LANES), lambda i: (i, 0)),
            pl.BlockSpec((1, LANES), lambda i: (0, 0)),
        ],
        out_shape=[
            jax.ShapeDtypeStruct((t, d // 2), jnp.uint32),
            jax.ShapeDtypeStruct((t, LANES), F32),
            jax.ShapeDtypeStruct((1, LANES), F32),
        ],
        scratch_shapes=[pltpu.VMEM((1, LANES), F32)],
        compiler_params=_cparams(("arbitrary",)),
        name="moe_router",
    )(x, g.reshape(1, d), w_pad)


def _row_copy(src, dst, s, d, sem):
    return pltpu.make_async_copy(src.at[pl.ds(s, 1)], dst.at[pl.ds(d, 1)], sem)


def _dispatch_kernel(dest_ref, h_ref, o_hbm, sem):
    tm = h_ref.shape[0]

    def start(t, carry):
        _row_copy(h_ref, o_hbm, t, dest_ref[0, 0, 2 * t], sem).start(priority=0)
        _row_copy(h_ref, o_hbm, t, dest_ref[0, 0, 2 * t + 1], sem).start(priority=1)
        return carry

    lax.fori_loop(0, tm, start, 0, unroll=8)

    def wait(t, carry):
        _row_copy(h_ref, o_hbm, 0, 0, sem).wait()
        _row_copy(h_ref, o_hbm, 0, 0, sem).wait()
        return carry

    lax.fori_loop(0, tm, wait, 0, unroll=8)


def _dispatch(h, dest, n_rows, tm=512):
    t, d = h.shape
    dest3 = dest.reshape(t // tm, 1, 2 * tm)
    return pl.pallas_call(
        _dispatch_kernel,
        grid=(t // tm,),
        in_specs=[
            pl.BlockSpec((1, 1, 2 * tm), lambda i: (i, 0, 0), memory_space=pltpu.SMEM),
            pl.BlockSpec((tm, d), lambda i: (i, 0)),
        ],
        out_specs=pl.BlockSpec(memory_space=pl.ANY),
        out_shape=jax.ShapeDtypeStruct((n_rows, d), h.dtype),
        scratch_shapes=[pltpu.SemaphoreType.DMA(())],
        compiler_params=_cparams(("arbitrary",)),
        name="moe_dispatch",
    )(dest3, h)


def _moe_kernel(be_ref, bb_ref, nv_ref, na_ref, x_ref, wg_ref, wu_ref, wd_ref, o_ref, xb_scr):
    i = pl.program_id(0)
    j = pl.program_id(1)
    n_valid = nv_ref[i]

    @pl.when(i < na_ref[0])
    def _():
        @pl.when(j == 0)
        def _():
            row = lax.broadcasted_iota(jnp.int32, x_ref.shape, 0)
            lo, hi = _unpack_bf16_pair(jnp.where(row < n_valid, x_ref[...], jnp.uint32(0)))
            xb_scr[:, :lo.shape[1]] = lo
            xb_scr[:, lo.shape[1]:] = hi
            o_ref[...] = jnp.zeros_like(o_ref)

        def run(rows):
            xb = xb_scr[rows, :]
            a = _dot(xb, wg_ref[0].astype(BF16))
            b = _dot(xb, wu_ref[0].astype(BF16))
            hid = (_silu(a) * b).astype(BF16)
            o_ref[rows, :] += _dot(hid, wd_ref[0].astype(BF16))

        part = x_ref.shape[0] // MOE_PARTS
        for nq in range(1, MOE_PARTS + 1):
            @pl.when((n_valid > (nq - 1) * part) & (n_valid <= nq * part))
            def _(nq=nq):
                run(slice(0, nq * part))


def _moe_experts(x_rows, wg, wu, wd, blk_e, blk_idx, blk_valid, n_live, n_blocks):
    dp = x_rows.shape[1]
    d = 2 * dp
    dff = wg.shape[2]
    n_ff = dff // MOE_TF
    r = MOE_ROWS

    def ffi(i, j, na):
        return jnp.where(i < na[0], j, n_ff - 1)

    grid_spec = pltpu.PrefetchScalarGridSpec(
        num_scalar_prefetch=4,
        grid=(n_blocks, n_ff),
        in_specs=[
            pl.BlockSpec((r, dp), lambda i, j, be, bb, nv, na: (bb[i], 0)),
            pl.BlockSpec((1, d, MOE_TF), lambda i, j, be, bb, nv, na: (be[i], 0, ffi(i, j, na))),
            pl.BlockSpec((1, d, MOE_TF), lambda i, j, be, bb, nv, na: (be[i], 0, ffi(i, j, na))),
            pl.BlockSpec((1, MOE_TF, d), lambda i, j, be, bb, nv, na: (be[i], ffi(i, j, na), 0)),
        ],
        out_specs=pl.BlockSpec((r, d), lambda i, j, be, bb, nv, na: (bb[i], 0)),
        scratch_shapes=[pltpu.VMEM((r, d), BF16)],
    )
    return pl.pallas_call(
        _moe_kernel,
        grid_spec=grid_spec,
        out_shape=jax.ShapeDtypeStruct((x_rows.shape[0], d), F32),
        compiler_params=_cparams(("arbitrary", "arbitrary"), vmem=BIG_VMEM_LIMIT),
        name="moe_experts",
    )(blk_e, blk_idx, blk_valid, n_live, x_rows, wg, wu, wd)


def _combine_kernel(dest_ref, next_ref, x_ref, meta_ref, nf_ref, y_hbm, o_ref, buf, sem):
    i = pl.program_id(0)
    tm = x_ref.shape[0]
    slot = i % 2

    def gather(idx_ref, s):
        def start(t, carry):
            _row_copy(y_hbm, buf.at[s, 0], idx_ref[0, 0, 2 * t], t, sem.at[s]).start(priority=0)
            _row_copy(y_hbm, buf.at[s, 1], idx_ref[0, 0, 2 * t + 1], t, sem.at[s]).start(priority=1)
            return carry

        lax.fori_loop(0, tm, start, 0, unroll=8)

    @pl.when(i == 0)
    def _():
        gather(dest_ref, slot)

    @pl.when(i + 1 < pl.num_programs(0))
    def _():
        gather(next_ref, 1 - slot)

    def wait(t, carry):
        _row_copy(y_hbm, buf.at[slot, 0], 0, 0, sem.at[slot]).wait()
        _row_copy(y_hbm, buf.at[slot, 1], 0, 0, sem.at[slot]).wait()
        return carry

    lax.fori_loop(0, tm, wait, 0, unroll=8)

    meta = meta_ref[...]
    g1 = meta[:, 4:5]
    g2 = meta[:, 5:6]
    x = x_ref[...] + (g1 * buf[slot, 0] + g2 * buf[slot, 1])
    o_ref[...] = _rms(x, nf_ref[...])


def _combine(x, meta, norm_final, y_rows, dest, tm=512):
    t, d = x.shape
    n_steps = t // tm
    dest3 = dest.reshape(n_steps, 1, 2 * tm)
    return pl.pallas_call(
        _combine_kernel,
        grid=(n_steps,),
        in_specs=[
            pl.BlockSpec((1, 1, 2 * tm), lambda i: (i, 0, 0), memory_space=pltpu.SMEM),
            pl.BlockSpec((1, 1, 2 * tm), lambda i: (jnp.minimum(i + 1, n_steps - 1), 0, 0),
                         memory_space=pltpu.SMEM),
            pl.BlockSpec((tm, d), lambda i: (i, 0)),
            pl.BlockSpec((tm, LANES), lambda i: (i, 0)),
            pl.BlockSpec((1, d), lambda i: (0, 0)),
            pl.BlockSpec(memory_space=pl.ANY),
        ],
        out_specs=pl.BlockSpec((tm, d), lambda i: (i, 0)),
        out_shape=jax.ShapeDtypeStruct((t, d), F32),
        scratch_shapes=[pltpu.VMEM((2, 2, tm, d), F32), pltpu.SemaphoreType.DMA((2,))],
        compiler_params=_cparams(("arbitrary",)),
        name="moe_combine",
    )(dest3, dest3, x, meta, norm_final.reshape(1, d), y_rows)


def _moe_block_table(counts, t):
    r = MOE_ROWS
    n_blocks = (2 * t) // r + N_EXPERTS
    nblk = (counts + r - 1) // r
    ends = jnp.cumsum(nblk)
    starts = ends - nblk
    n_live = ends[-1]
    step = jnp.minimum(jnp.arange(n_blocks, dtype=jnp.int32), n_live - 1)
    blk_e = jnp.sum(step[:, None] >= ends[None, :], axis=1).astype(jnp.int32)
    local = step - starts[blk_e]
    blk_valid = jnp.clip(counts[blk_e] - local * r, 0, r)
    return (starts * r, blk_e, step, blk_valid.astype(jnp.int32),
            n_live.reshape(1).astype(jnp.int32), n_blocks)


def kernel(x, norm_mix, norm_ffn, w_in_even, pool_w, pool_scale, w_out_even, ffn_w_gate,
           ffn_w_up, ffn_w_down, w_in_odd, lower_bound_logits, hgrn_out_norm, w_out_odd,
           router_w, moe_w_gate, moe_w_up, moe_w_down, norm_final):
    bsz, seq, d = x.shape
    t = bsz * seq

    lb_all = jnp.cumsum(jax.nn.softmax(lower_bound_logits.astype(F32), axis=0), axis=0)
    lb_all = lb_all - lb_all[:1]

    n_in = w_in_even.shape[2]
    q_scale = math.log2(math.e) / math.sqrt(HEAD_DIM)
    col_scale = jnp.where(jnp.arange(n_in) < A_WIDTH, q_scale, 1.0).astype(F32)
    proj = _norm_matmul_res(x, norm_mix[0], w_in_even[0].astype(BF16), col_scale)
    mix = _attn_pool(proj.reshape(bsz, seq, -1), pool_w[0], pool_scale[0])
    xt = _matmul_unres(mix.reshape(t, -1), w_out_even[0].astype(BF16), x)
    xt = _swiglu_res(xt, norm_ffn[0], ffn_w_gate[0].astype(BF16), ffn_w_up[0].astype(BF16),
                     ffn_w_down[0].astype(BF16))

    wid = HGRN_HEADS * HEAD_DIM
    col = jnp.arange(4 * wid) // wid
    proj = _norm_matmul(xt, norm_mix[1], w_in_odd[0].astype(BF16),
                        jnp.where(col == 2, 1.0, 0.5).astype(F32))
    o = _hgrn(proj.reshape(bsz, seq, -1), lb_all[1], hgrn_out_norm[0])
    xt = _matmul_res(o.reshape(t, -1), w_out_odd[0].astype(BF16), xt)

    h, meta, cnt = _router(xt, norm_ffn[1], router_w[0])
    counts = cnt[0, :N_EXPERTS].astype(jnp.int32)
    row0, blk_e, blk_idx, blk_valid, n_live, n_blocks = _moe_block_table(counts, t)
    dest = (row0[meta[:, 0:2].astype(jnp.int32)] + meta[:, 2:4].astype(jnp.int32)).reshape(-1)
    x_rows = _dispatch(h, dest, n_blocks * MOE_ROWS)
    y_rows = _moe_experts(x_rows, moe_w_gate[0], moe_w_up[0], moe_w_down[0],
                          blk_e, blk_idx, blk_valid, n_live, n_blocks)
    out = _combine(xt, meta, norm_final, y_rows, dest)
    return out.reshape(bsz, seq, d)
```

```python
import math

import jax
import jax.numpy as jnp
import numpy as np
from jax import lax
from jax.experimental import pallas as pl
from jax.experimental.pallas import tpu as pltpu

F32 = jnp.float32
BF16 = jnp.bfloat16

D_MODEL = 2048
HEAD_DIM = 128
A_HEADS = 12
A_WIDTH = A_HEADS * HEAD_DIM
N_POOL = 4
POOL_WINDOWS = (2, 4, 8, 16)
HGRN_HEADS = 16
HGRN_CHUNK = 64
N_EXPERTS = 8
EPS = 1e-6
NEG = -1e30

LANES = 128
VMEM_LIMIT = 56 * 1024 * 1024

MOE_ROWS = 1024
MOE_PARTS = 4
MOE_TF = 512
BIG_VMEM_LIMIT = 62 * 1024 * 1024
HI16 = 0xFFFF0000


def _pack_bf16_pair(x):
    n = x.shape[1] // 2
    bits = lax.bitcast_convert_type(x.astype(BF16).astype(F32), jnp.uint32)
    return (bits[:, n:] & jnp.uint32(HI16)) | (bits[:, :n] >> 16)


def _unpack_bf16_pair(p):
    lo = lax.bitcast_convert_type(p << 16, F32).astype(BF16)
    hi = lax.bitcast_convert_type(p & jnp.uint32(HI16), F32).astype(BF16)
    return lo, hi

N_RES = 16


def _cparams(sem, vmem=VMEM_LIMIT, flags=None):
    return pltpu.CompilerParams(dimension_semantics=sem, vmem_limit_bytes=vmem, flags=flags)


def _rms(x, g):
    ms = jnp.mean(x * x, axis=-1, keepdims=True)
    return x * lax.rsqrt(ms + EPS) * g


def _half_silu(h):
    return h + h * jnp.tanh(h)


def _silu(x):
    return _half_silu(0.5 * x)


def _dot(a, b):
    return jnp.dot(a, b, preferred_element_type=F32)


def _dot_nt(a, b):
    return lax.dot_general(a, b, (((1,), (1,)), ((), ())), preferred_element_type=F32)


def _dot_tn(a, b):
    return lax.dot_general(a, b, (((0,), (0,)), ((), ())), preferred_element_type=F32)


def _norm_matmul_kernel(x_ref, g_ref, w_ref, cs_ref, o_ref, h_scr):
    @pl.when(pl.program_id(1) == 0)
    def _():
        h_scr[...] = _rms(x_ref[...], g_ref[...]).astype(BF16)

    o_ref[...] = (_dot(h_scr[...], w_ref[...]) * cs_ref[...]).astype(o_ref.dtype)


def _norm_matmul(x, g, w, col_scale, tm=1024, tn=2048):
    m, d = x.shape
    n = w.shape[1]
    return pl.pallas_call(
        _norm_matmul_kernel,
        grid=(m // tm, n // tn),
        in_specs=[
            pl.BlockSpec((tm, d), lambda i, j: (i, 0)),
            pl.BlockSpec((1, d), lambda i, j: (0, 0)),
            pl.BlockSpec((d, tn), lambda i, j: (0, j)),
            pl.BlockSpec((1, tn), lambda i, j: (0, j)),
        ],
        out_specs=pl.BlockSpec((tm, tn), lambda i, j: (i, j)),
        out_shape=jax.ShapeDtypeStruct((m, n), BF16),
        scratch_shapes=[pltpu.VMEM((tm, d), BF16)],
        compiler_params=_cparams(("parallel", "arbitrary")),
        name="norm_matmul",
    )(x, g.reshape(1, d), w, col_scale.reshape(1, n))


def _norm_matmul_res_kernel(x_hbm, g_ref, w_ref, cs_ref, o_ref, xbuf, h_scr, sem):
    i = pl.program_id(0)
    n_res_tile, rows, _ = xbuf.shape[1:]
    halves = N_RES // n_res_tile

    def fetch(step, slot):
        b = step // halves
        r0 = (step % halves) * n_res_tile
        return [pltpu.make_async_copy(x_hbm.at[b, :, r0 + r, :], xbuf.at[slot, r], sem.at[slot, r])
                for r in range(n_res_tile)]

    @pl.when(pl.program_id(1) == 0)
    def _():
        slot = i % 2

        @pl.when(i == 0)
        def _():
            for c in fetch(i, slot):
                c.start()

        @pl.when(i + 1 < pl.num_programs(0))
        def _():
            for c in fetch(i + 1, 1 - slot):
                c.start()

        for r, c in enumerate(fetch(i, slot)):
            c.wait()
            h_scr[r * rows:(r + 1) * rows, :] = _rms(xbuf[slot, r], g_ref[...]).astype(BF16)

    o_ref[...] = (_dot(h_scr[...], w_ref[...]) * cs_ref[...]).astype(o_ref.dtype)


def _norm_matmul_res(x, g, w, col_scale, n_res_tile=8, tn=2560):
    bsz, seq, d = x.shape
    n = w.shape[1]
    rows = seq // N_RES
    tm = n_res_tile * rows
    halves = N_RES // n_res_tile
    return pl.pallas_call(
        _norm_matmul_res_kernel,
        grid=(bsz * halves, n // tn),
        in_specs=[
            pl.BlockSpec(memory_space=pl.ANY),
            pl.BlockSpec((1, d), lambda i, j: (0, 0)),
            pl.BlockSpec((d, tn), lambda i, j: (0, j)),
            pl.BlockSpec((1, tn), lambda i, j: (0, j)),
        ],
        out_specs=pl.BlockSpec((tm, tn), lambda i, j: (i, j)),
        out_shape=jax.ShapeDtypeStruct((bsz * seq, n), BF16),
        scratch_shapes=[pltpu.VMEM((2, n_res_tile, rows, d), F32), pltpu.VMEM((tm, d), BF16),
                        pltpu.SemaphoreType.DMA((2, n_res_tile))],
        compiler_params=_cparams(("arbitrary", "arbitrary")),
        name="norm_matmul_res",
    )(x.reshape(bsz, rows, N_RES, d), g.reshape(1, d), w, col_scale.reshape(1, n))


def _matmul_res_kernel(a_ref, w_ref, r_ref, o_ref):
    o_ref[...] = r_ref[...] + _dot(a_ref[...], w_ref[...])


def _matmul_res(a, w, res, tm=512, tn=2048):
    m, k = a.shape
    n = w.shape[1]
    return pl.pallas_call(
        _matmul_res_kernel,
        grid=(m // tm, n // tn),
        in_specs=[
            pl.BlockSpec((tm, k), lambda i, j: (i, 0)),
            pl.BlockSpec((k, tn), lambda i, j: (0, j)),
            pl.BlockSpec((tm, tn), lambda i, j: (i, j)),
        ],
        out_specs=pl.BlockSpec((tm, tn), lambda i, j: (i, j)),
        out_shape=jax.ShapeDtypeStruct((m, n), F32),
        compiler_params=_cparams(("parallel", "arbitrary")),
        name="matmul_res",
    )(a, w, res)


def _matmul_unres_kernel(a_ref, w_ref, r_hbm, o_hbm, rbuf, obuf, rsem, osem):
    i = pl.program_id(0)
    n_steps = pl.num_programs(0)
    n_res_tile, rows, _ = rbuf.shape[1:]
    parts = N_RES // n_res_tile

    def res_fetch(step, slot):
        b, r0 = step // parts, (step % parts) * n_res_tile
        return [pltpu.make_async_copy(r_hbm.at[b, :, r0 + r, :], rbuf.at[slot, r], rsem.at[slot, r])
                for r in range(n_res_tile)]

    def out_store(step, slot):
        b, r0 = step // parts, (step % parts) * n_res_tile
        return [pltpu.make_async_copy(obuf.at[slot, r], o_hbm.at[b, :, r0 + r, :], osem.at[slot, r])
                for r in range(n_res_tile)]

    slot = i % 2

    @pl.when(i == 0)
    def _():
        for c in res_fetch(i, slot):
            c.start()

    @pl.when(i + 1 < n_steps)
    def _():
        for c in res_fetch(i + 1, 1 - slot):
            c.start()

    y = _dot(a_ref[...], w_ref[...])

    @pl.when(i >= 2)
    def _():
        for c in out_store(i - 2, slot):
            c.wait()

    for r, c in enumerate(res_fetch(i, slot)):
        c.wait()
        obuf[slot, r] = rbuf[slot, r] + y[r * rows:(r + 1) * rows, :]
    for c in out_store(i, slot):
        c.start()

    @pl.when(i == n_steps - 1)
    def _():
        for c in out_store(i - 1, 1 - slot) + out_store(i, slot):
            c.wait()


def _matmul_unres(a, w, res, n_res_tile=4):
    bsz, seq, d = res.shape
    k = a.shape[1]
    rows = seq // N_RES
    tm = n_res_tile * rows
    parts = N_RES // n_res_tile
    assert bsz * parts >= 2
    buf = pltpu.VMEM((2, n_res_tile, rows, d), F32)
    out = pl.pallas_call(
        _matmul_unres_kernel,
        grid=(bsz * parts,),
        in_specs=[
            pl.BlockSpec((tm, k), lambda i: (i, 0)),
            pl.BlockSpec((k, d), lambda i: (0, 0)),
            pl.BlockSpec(memory_space=pl.ANY),
        ],
        out_specs=pl.BlockSpec(memory_space=pl.ANY),
        out_shape=jax.ShapeDtypeStruct((bsz, rows, N_RES, d), F32),
        scratch_shapes=[buf, buf, pltpu.SemaphoreType.DMA((2, n_res_tile)),
                        pltpu.SemaphoreType.DMA((2, n_res_tile))],
        compiler_params=_cparams(("arbitrary",)),
        name="matmul_unres",
    )(a, w, res.reshape(bsz, rows, N_RES, d))
    return out.reshape(bsz * seq, d)


def _swiglu_kernel(x_ref, g_ref, wg_ref, wu_ref, wd_ref, o_ref, h_scr):
    @pl.when(pl.program_id(1) == 0)
    def _():
        h_scr[...] = _rms(x_ref[...], g_ref[...]).astype(BF16)
        o_ref[...] = x_ref[...]

    h = h_scr[...]
    a = _dot(h, wg_ref[...])
    b = _dot(h, wu_ref[...])
    hid = (_silu(a) * b).astype(BF16)
    o_ref[...] += _dot(hid, wd_ref[...])


def _swiglu_res(x, g, wg, wu, wd, tm=1024, tf=512):
    m, d = x.shape
    ff = wg.shape[1]
    return pl.pallas_call(
        _swiglu_kernel,
        grid=(m // tm, ff // tf),
        in_specs=[
            pl.BlockSpec((tm, d), lambda i, j: (i, 0)),
            pl.BlockSpec((1, d), lambda i, j: (0, 0)),
            pl.BlockSpec((d, tf), lambda i, j: (0, j)),
            pl.BlockSpec((d, tf), lambda i, j: (0, j)),
            pl.BlockSpec((tf, d), lambda i, j: (j, 0)),
        ],
        out_specs=pl.BlockSpec((tm, d), lambda i, j: (i, 0)),
        out_shape=jax.ShapeDtypeStruct((m, d), F32),
        scratch_shapes=[pltpu.VMEM((tm, d), BF16)],
        compiler_params=_cparams(("parallel", "arbitrary"), vmem=BIG_VMEM_LIMIT),
        name="swiglu_res",
    )(x, g.reshape(1, d), wg, wu, wd)


def _attn_bias_table():
    blk = HEAD_DIM
    iq = np.arange(blk)[:, None]
    jk = np.arange(2 * blk)[None, :]

    def offs(idx, n_chunk, size):
        return n_chunk * (idx % size) + idx // size

    masks = []
    for n_chunk in (1, 4, 16):
        size = blk // n_chunk
        lq = offs(iq, n_chunk, size)
        masks.append((offs(jk, n_chunk, size) <= lq) & (jk < blk))
        dist = lq - (offs(jk, n_chunk, 2 * size) - blk)
        masks.append((dist >= 0) & (dist <= blk))
    return np.where(np.stack(masks), 0.0, NEG).astype(np.float32)


def _attn_pool_kernel(q_ref, k_ref, v_ref, bias, pw_ref, ps_ref, o_ref,
                      qf, kf, vf, ms, ls, accs):
    h = pl.program_id(1)
    seq = q_ref.shape[0]
    blk = HEAD_DIM
    bf16_rows = 16

    @pl.when(h < A_HEADS)
    def _attention():
        qf[...] = q_ref[...].astype(F32)
        kf[...] = k_ref[...].astype(F32)
        vf[...] = v_ref[...].astype(F32)

        def gather(ref, ref_f32, starts, size):
            if all(v % bf16_rows == 0 for v in [size] + starts):
                return jnp.concatenate([ref[s0:s0 + size, :] for s0 in starts], axis=0)
            return jnp.concatenate([ref_f32[s0:s0 + size, :] for s0 in starts],
                                   axis=0).astype(BF16)

        def scatter(br, starts, size, m, l, acc):
            mb = jnp.broadcast_to(m, acc.shape)
            lb = jnp.broadcast_to(l, acc.shape)
            for c, s0 in enumerate(starts):
                src = slice(c * size, (c + 1) * size)
                ms[br, s0:s0 + size, :] = mb[src]
                ls[br, s0:s0 + size, :] = lb[src]
                accs[br, s0:s0 + size, :] = acc[src]

        kinds = {}
        for br, n_chunk in enumerate((1, 4, 16)):
            size = blk // n_chunk
            for grp in range(N_RES // n_chunk):
                bases = [(grp + (N_RES // n_chunk) * c) * blk for c in range(n_chunk)]
                for n in range(n_chunk):
                    qs = [b0 + n * size for b0 in bases]
                    if n == 0:
                        kinds.setdefault((2 * br, blk), []).append((br, size, qs, qs, size))
                    else:
                        ks = [s0 - size for s0 in qs]
                        kinds.setdefault((2 * br + 1, 2 * blk), []).append(
                            (br, size, qs, ks, 2 * size))

        batch = 8
        for (bias_idx, kw), blocks in kinds.items():
            bi = bias[bias_idx][:, :kw]
            for g0 in range(0, len(blocks), batch):
                group = blocks[g0:g0 + batch]
                s = jnp.concatenate(
                    [_dot_nt(gather(q_ref, qf, qs, size), gather(k_ref, kf, ks, ksize))
                     for _, size, qs, ks, ksize in group], axis=0)
                s = (s.reshape(len(group), blk, kw) + bi[None]).reshape(len(group) * blk, kw)
                m = jnp.max(s, axis=-1, keepdims=True)
                pb = jnp.exp2(s - m).astype(BF16)
                ones = jnp.ones((kw, blk), BF16)
                for g, (br, size, qs, ks, ksize) in enumerate(group):
                    rows = slice(g * blk, (g + 1) * blk)
                    v_one = jnp.concatenate([gather(v_ref, vf, ks, ksize), ones], axis=1)
                    acc = _dot(pb[rows], v_one)
                    scatter(br, qs, size, m[rows], acc[:, blk:], acc[:, :blk])

        slab = 2 * blk
        for s0 in range(0, seq, slab):
            rows = slice(s0, s0 + slab)
            m_all = jnp.maximum(jnp.maximum(ms[0, rows, :], ms[1, rows, :]), ms[2, rows, :])
            num = jnp.zeros((slab, blk), F32)
            den = jnp.zeros((slab, blk), F32)
            for br in range(3):
                w = jnp.exp2(ms[br, rows, :] - m_all)
                num = num + accs[br, rows, :] * w
                den = den + ls[br, rows, :] * w
            o_ref[rows, :] = (num / den).astype(o_ref.dtype)

    for gi, w in enumerate(POOL_WINDOWS):
        @pl.when(h == A_HEADS + gi)
        def _pool(gi=gi, w=w):
            pf = q_ref[...].astype(F32)
            row = lax.broadcasted_iota(jnp.int32, pf.shape, 0)
            l_idx = row % blk
            pos = N_RES * l_idx + row // blk
            s = pf
            k = 1
            while k < w:
                y = pltpu.roll(s, k * blk, axis=0)
                y1 = jnp.where(l_idx == 0, 0.0, pltpu.roll(y, 1, axis=0))
                s = s + jnp.where(row < k * blk, y1, y)
                k *= 2
            cnt = jnp.minimum(pos + 1, w).astype(F32)
            pooled = s / cnt - pf
            y = _dot(pooled.astype(BF16), pw_ref[gi].astype(BF16)) * ps_ref[...]
            o_ref[...] = y.astype(o_ref.dtype)


def _attn_pool(proj, pool_w, pool_scale):
    bsz, seq, _ = proj.shape
    n_steps = A_HEADS + N_POOL
    qmap = lambda b, h: (b, 0, jnp.where(h < A_HEADS, h, h + 2 * A_HEADS))
    kmap = lambda b, h: (b, 0, A_HEADS + jnp.minimum(h, A_HEADS - 1))
    vmap = lambda b, h: (b, 0, 2 * A_HEADS + jnp.minimum(h, A_HEADS - 1))
    f32s = pltpu.VMEM((seq, HEAD_DIM), F32)
    br3 = pltpu.VMEM((3, seq, HEAD_DIM), F32)
    bias = jnp.asarray(_attn_bias_table())
    return pl.pallas_call(
        _attn_pool_kernel,
        grid=(bsz, n_steps),
        in_specs=[
            pl.BlockSpec((None, seq, HEAD_DIM), qmap),
            pl.BlockSpec((None, seq, HEAD_DIM), kmap),
            pl.BlockSpec((None, seq, HEAD_DIM), vmap),
            pl.BlockSpec(bias.shape, lambda b, h: (0, 0, 0)),
            pl.BlockSpec((N_POOL, HEAD_DIM, HEAD_DIM), lambda b, h: (0, 0, 0)),
            pl.BlockSpec((1, HEAD_DIM), lambda b, h: (0, jnp.maximum(h - A_HEADS, 0))),
        ],
        out_specs=pl.BlockSpec((None, seq, HEAD_DIM), lambda b, h: (b, 0, h)),
        out_shape=jax.ShapeDtypeStruct((bsz, seq, n_steps * HEAD_DIM), BF16),
        scratch_shapes=[f32s, f32s, f32s, br3, br3, br3],
        compiler_params=_cparams(("parallel", "arbitrary")),
        name="attn_pool",
    )(proj, proj, proj, bias, pool_w, pool_scale.reshape(1, N_POOL * HEAD_DIM))


def _hgrn_kernel(q_ref, f_ref, i_ref, gt_ref, lb_ref, on_ref, o_ref,
                 qe_scr, o_scr, u_scr, d_scr, s_scr):
    seq = q_ref.shape[0]
    ck = HGRN_CHUNK
    grp = 4 * ck
    n_grp = seq // grp
    n_chunk = seq // ck
    lb = lb_ref[...]

    ri = lax.broadcasted_iota(jnp.int32, (grp, grp), 0)
    ci = lax.broadcasted_iota(jnp.int32, (grp, grp), 1)
    tri_mask = (ri // ck == ci // ck) & (ci <= ri)
    tri = tri_mask.astype(BF16)

    groups = [slice(gi * grp, (gi + 1) * grp) for gi in range(n_grp)]
    q = _half_silu(q_ref[...].astype(F32))
    f = (0.5 + 0.5 * lb) + (0.5 - 0.5 * lb) * jnp.tanh(f_ref[...].astype(F32))
    k = 1.0 - f
    g = jnp.log2(f)
    g_hi = g.astype(BF16)
    g_lo = (g - g_hi.astype(F32)).astype(BF16)
    b = jnp.concatenate([_dot(tri, g_hi[r]) + _dot(tri, g_lo[r]) for r in groups], axis=0)
    b3 = b.reshape(n_chunk, ck, HEAD_DIM)
    b_mid = b3[:, ck // 2 - 1:ck // 2, :]
    b_last = b3[:, ck - 1:ck, :]
    q3 = q.reshape(n_chunk, ck, HEAD_DIM)
    k3 = k.reshape(n_chunk, ck, HEAD_DIM)
    q_mid = q3 * jnp.exp2(b3 - b_mid)
    k_mid = k3 * jnp.exp2(b_mid - b3)
    qs = q_mid.reshape(seq, HEAD_DIM).astype(BF16)
    ks = k_mid.reshape(seq, HEAD_DIM).astype(BF16)
    qe_scr[...] = (q_mid * jnp.exp2(b_mid)).reshape(seq, HEAD_DIM).astype(BF16)
    kd = (k_mid * jnp.exp2(b_last - b_mid)).reshape(seq, HEAD_DIM).astype(BF16)
    d_scr[...] = jnp.exp2(b_last).reshape(n_chunk, HEAD_DIM)
    for r in groups:
        att = jnp.where(tri_mask, _dot_nt(qs[r], ks[r]), 0.0)
        o_scr[r, :] = _dot(att.astype(BF16), i_ref[r, :])
    for c in range(n_chunk):
        cr = slice(c * ck, (c + 1) * ck)
        u_scr[c] = _dot_tn(i_ref[cr, :], kd[cr])

    st = jnp.zeros((HEAD_DIM, HEAD_DIM), F32)
    for c in range(n_chunk):
        s_scr[c] = st.astype(BF16)
        st = st * d_scr[c:c + 1, :] + u_scr[c]

    for gi in range(n_grp):
        rows = slice(gi * grp, (gi + 1) * grp)
        inter = jnp.concatenate(
            [_dot_nt(qe_scr[gi * grp + c * ck:gi * grp + (c + 1) * ck, :], s_scr[gi * 4 + c])
             for c in range(4)], axis=0)
        o = _rms(o_scr[rows, :] + inter, on_ref[...])
        o_ref[rows, :] = (o * _half_silu(gt_ref[rows, :].astype(F32))).astype(o_ref.dtype)


def _hgrn(proj, lb, out_norm):
    bsz, seq, _ = proj.shape
    nh = HGRN_HEADS
    n_chunk = seq // HGRN_CHUNK
    spec = lambda off: pl.BlockSpec((None, seq, HEAD_DIM), lambda b, h: (b, 0, off + h))
    vec = pl.BlockSpec((1, HEAD_DIM), lambda b, h: (0, h))
    return pl.pallas_call(
        _hgrn_kernel,
        grid=(bsz, nh),
        in_specs=[spec(0), spec(nh), spec(2 * nh), spec(3 * nh), vec, vec],
        out_specs=pl.BlockSpec((None, seq, HEAD_DIM), lambda b, h: (b, 0, h)),
        out_shape=jax.ShapeDtypeStruct((bsz, seq, nh * HEAD_DIM), BF16),
        scratch_shapes=[
            pltpu.VMEM((seq, HEAD_DIM), BF16),
            pltpu.VMEM((seq, HEAD_DIM), F32),
            pltpu.VMEM((n_chunk, HEAD_DIM, HEAD_DIM), F32),
            pltpu.VMEM((n_chunk, HEAD_DIM), F32),
            pltpu.VMEM((n_chunk, HEAD_DIM, HEAD_DIM), BF16),
        ],
        compiler_params=_cparams(("parallel", "arbitrary")),
        name="hgrn2",
    )(proj, proj, proj, proj, lb.reshape(1, -1), out_norm.reshape(1, -1))


def _router_kernel(x_ref, g_ref, w_ref, h_ref, meta_ref, cnt_ref, carry):
    i = pl.program_id(0)
    tm = x_ref.shape[0]

    @pl.when(i == 0)
    def _():
        carry[...] = jnp.zeros_like(carry)

    h = _rms(x_ref[...], g_ref[...])
    h_ref[...] = _pack_bf16_pair(h)
    h_hi = h.astype(BF16)
    h_lo = (h - h_hi.astype(F32)).astype(BF16)
    w = w_ref[...]
    w_hi = w.astype(BF16)
    w_lo = (w - w_hi.astype(F32)).astype(BF16)
    prod = _dot(jnp.concatenate([h_hi, h_lo], axis=0), jnp.concatenate([w_hi, w_lo], axis=1))
    logits = (prod[:tm, :LANES] + prod[:tm, LANES:]) + (prod[tm:, :LANES] + prod[tm:, LANES:])
    lane = lax.broadcasted_iota(jnp.int32, logits.shape, 1)
    logits = jnp.where(lane < N_EXPERTS, logits, NEG)
    m1 = jnp.max(logits, axis=-1, keepdims=True)
    i1 = jnp.min(jnp.where(logits == m1, lane, LANES), axis=-1, keepdims=True)
    rest = jnp.where(lane == i1, NEG, logits)
    m2 = jnp.max(rest, axis=-1, keepdims=True)
    i2 = jnp.min(jnp.where(rest == m2, lane, LANES), axis=-1, keepdims=True)
    e = jnp.exp(m2 - m1)
    g1 = 1.0 / (1.0 + e)
    g2 = e / (1.0 + e)
    oh1 = (lane == i1).astype(F32)
    oh2 = (lane == i2).astype(F32)
    cnt = oh1 + oh2
    ri = lax.broadcasted_iota(jnp.int32, (tm, tm), 0)
    ci = lax.broadcasted_iota(jnp.int32, (tm, tm), 1)
    before = (ci < ri).astype(BF16)
    prior = _dot(before, cnt.astype(BF16)) + carry[...]
    r1 = jnp.sum(oh1 * prior, axis=-1, keepdims=True)
    r2 = jnp.sum(oh2 * prior, axis=-1, keepdims=True)
    carry[...] += jnp.sum(cnt, axis=0, keepdims=True)
    cnt_ref[...] = carry[...]
    meta = jnp.where(lane == 0, i1.astype(F32), 0.0)
    meta = jnp.where(lane == 1, i2.astype(F32), meta)
    meta = jnp.where(lane == 2, r1, meta)
    meta = jnp.where(lane == 3, r2, meta)
    meta = jnp.where(lane == 4, g1, meta)
    meta = jnp.where(lane == 5, g2, meta)
    meta_ref[...] = meta


def _router(x, g, router_w, tm=512):
    t, d = x.shape
    w_pad = jnp.pad(router_w, ((0, 0), (0, LANES - router_w.shape[1])))
    return pl.pallas_call(
        _router_kernel,
        grid=(t // tm,),
        in_specs=[
            pl.BlockSpec((tm, d), lambda i: (i, 0)),
            pl.BlockSpec((1, d), lambda i: (0, 0)),
            pl.BlockSpec((d, LANES), lambda i: (0, 0)),
        ],
        out_specs=[
            pl.BlockSpec((tm, d // 2), lambda i: (i, 0)),
            pl.BlockSpec((tm, LANES), lambda i: (i, 0)),
            pl.BlockSpec((1, LANES), lambda i: (0, 0)),
        ],
        out_shape=[
            jax.ShapeDtypeStruct((t, d // 2), jnp.uint32),
            jax.ShapeDtypeStruct((t, LANES), F32),
            jax.ShapeDtypeStruct((1, LANES), F32),
        ],
        scratch_shapes=[pltpu.VMEM((1, LANES), F32)],
        compiler_params=_cparams(("arbitrary",)),
        name="moe_router",
    )(x, g.reshape(1, d), w_pad)


def _row_copy(src, dst, s, d, sem):
    return pltpu.make_async_copy(src.at[pl.ds(s, 1)], dst.at[pl.ds(d, 1)], sem)


def _dispatch_kernel(dest_ref, h_ref, o_hbm, sem):
    tm = h_ref.shape[0]

    def start(t, carry):
        _row_copy(h_ref, o_hbm, t, dest_ref[0, 0, 2 * t], sem).start(priority=0)
        _row_copy(h_ref, o_hbm, t, dest_ref[0, 0, 2 * t + 1], sem).start(priority=1)
        return carry

    lax.fori_loop(0, tm, start, 0, unroll=8)

    def wait(t, carry):
        _row_copy(h_ref, o_hbm, 0, 0, sem).wait()
        _row_copy(h_ref, o_hbm, 0, 0, sem).wait()
        return carry

    lax.fori_loop(0, tm, wait, 0, unroll=8)


def _dispatch(h, dest, n_rows, tm=512):
    t, d = h.shape
    dest3 = dest.reshape(t // tm, 1, 2 * tm)
    return pl.pallas_call(
        _dispatch_kernel,
        grid=(t // tm,),
        in_specs=[
            pl.BlockSpec((1, 1, 2 * tm), lambda i: (i, 0, 0), memory_space=pltpu.SMEM),
            pl.BlockSpec((tm, d), lambda i: (i, 0)),
        ],
        out_specs=pl.BlockSpec(memory_space=pl.ANY),
        out_shape=jax.ShapeDtypeStruct((n_rows, d), h.dtype),
        scratch_shapes=[pltpu.SemaphoreType.DMA(())],
        compiler_params=_cparams(("arbitrary",)),
        name="moe_dispatch",
    )(dest3, h)


def _moe_kernel(be_ref, bb_ref, nv_ref, na_ref, x_ref, wg_ref, wu_ref, wd_ref, o_ref, xb_scr):
    i = pl.program_id(0)
    j = pl.program_id(1)
    n_valid = nv_ref[i]

    @pl.when(i < na_ref[0])
    def _():
        @pl.when(j == 0)
        def _():
            row = lax.broadcasted_iota(jnp.int32, x_ref.shape, 0)
            lo, hi = _unpack_bf16_pair(jnp.where(row < n_valid, x_ref[...], jnp.uint32(0)))
            xb_scr[:, :lo.shape[1]] = lo
            xb_scr[:, lo.shape[1]:] = hi
            o_ref[...] = jnp.zeros_like(o_ref)

        def run(rows):
            xb = xb_scr[rows, :]
            a = _dot(xb, wg_ref[0].astype(BF16))
            b = _dot(xb, wu_ref[0].astype(BF16))
            hid = (_silu(a) * b).astype(BF16)
            o_ref[rows, :] += _dot(hid, wd_ref[0].astype(BF16))

        part = x_ref.shape[0] // MOE_PARTS
        for nq in range(1, MOE_PARTS + 1):
            @pl.when((n_valid > (nq - 1) * part) & (n_valid <= nq * part))
            def _(nq=nq):
                run(slice(0, nq * part))


def _moe_experts(x_rows, wg, wu, wd, blk_e, blk_idx, blk_valid, n_live, n_blocks):
    dp = x_rows.shape[1]
    d = 2 * dp
    dff = wg.shape[2]
    n_ff = dff // MOE_TF
    r = MOE_ROWS

    def ffi(i, j, na):
        return jnp.where(i < na[0], j, n_ff - 1)

    grid_spec = pltpu.PrefetchScalarGridSpec(
        num_scalar_prefetch=4,
        grid=(n_blocks, n_ff),
        in_specs=[
            pl.BlockSpec((r, dp), lambda i, j, be, bb, nv, na: (bb[i], 0)),
            pl.BlockSpec((1, d, MOE_TF), lambda i, j, be, bb, nv, na: (be[i], 0, ffi(i, j, na))),
            pl.BlockSpec((1, d, MOE_TF), lambda i, j, be, bb, nv, na: (be[i], 0, ffi(i, j, na))),
            pl.BlockSpec((1, MOE_TF, d), lambda i, j, be, bb, nv, na: (be[i], ffi(i, j, na), 0)),
        ],
        out_specs=pl.BlockSpec((r, d), lambda i, j, be, bb, nv, na: (bb[i], 0)),
        scratch_shapes=[pltpu.VMEM((r, d), BF16)],
    )
    return pl.pallas_call(
        _moe_kernel,
        grid_spec=grid_spec,
        out_shape=jax.ShapeDtypeStruct((x_rows.shape[0], d), F32),
        compiler_params=_cparams(("arbitrary", "arbitrary"), vmem=BIG_VMEM_LIMIT),
        name="moe_experts",
    )(blk_e, blk_idx, blk_valid, n_live, x_rows, wg, wu, wd)


def _combine_kernel(dest_ref, next_ref, x_ref, meta_ref, nf_ref, y_hbm, o_ref, buf, sem):
    i = pl.program_id(0)
    tm = x_ref.shape[0]
    slot = i % 2

    def gather(idx_ref, s):
        def start(t, carry):
            _row_copy(y_hbm, buf.at[s, 0], idx_ref[0, 0, 2 * t], t, sem.at[s]).start(priority=0)
            _row_copy(y_hbm, buf.at[s, 1], idx_ref[0, 0, 2 * t + 1], t, sem.at[s]).start(priority=1)
            return carry

        lax.fori_loop(0, tm, start, 0, unroll=8)

    @pl.when(i == 0)
    def _():
        gather(dest_ref, slot)

    @pl.when(i + 1 < pl.num_programs(0))
    def _():
        gather(next_ref, 1 - slot)

    def wait(t, carry):
        _row_copy(y_hbm, buf.at[slot, 0], 0, 0, sem.at[slot]).wait()
        _row_copy(y_hbm, buf.at[slot, 1], 0, 0, sem.at[slot]).wait()
        return carry

    lax.fori_loop(0, tm, wait, 0, unroll=8)

    meta = meta_ref[...]
    g1 = meta[:, 4:5]
    g2 = meta[:, 5:6]
    x = x_ref[...] + (g1 * buf[slot, 0] + g2 * buf[slot, 1])
    o_ref[...] = _rms(x, nf_ref[...])


def _combine(x, meta, norm_final, y_rows, dest, tm=512):
    t, d = x.shape
    n_steps = t // tm
    dest3 = dest.reshape(n_steps, 1, 2 * tm)
    return pl.pallas_call(
        _combine_kernel,
        grid=(n_steps,),
        in_specs=[
            pl.BlockSpec((1, 1, 2 * tm), lambda i: (i, 0, 0), memory_space=pltpu.SMEM),
            pl.BlockSpec((1, 1, 2 * tm), lambda i: (jnp.minimum(i + 1, n_steps - 1), 0, 0),
                         memory_space=pltpu.SMEM),
            pl.BlockSpec((tm, d), lambda i: (i, 0)),
            pl.BlockSpec((tm, LANES), lambda i: (i, 0)),
            pl.BlockSpec((1, d), lambda i: (0, 0)),
            pl.BlockSpec(memory_space=pl.ANY),
        ],
        out_specs=pl.BlockSpec((tm, d), lambda i: (i, 0)),
        out_shape=jax.ShapeDtypeStruct((t, d), F32),
        scratch_shapes=[pltpu.VMEM((2, 2, tm, d), F32), pltpu.SemaphoreType.DMA((2,))],
        compiler_params=_cparams(("arbitrary",)),
        name="moe_combine",
    )(dest3, dest3, x, meta, norm_final.reshape(1, d), y_rows)


def _moe_block_table(counts, t):
    r = MOE_ROWS
    n_blocks = (2 * t) // r + N_EXPERTS
    nblk = (counts + r - 1) // r
    ends = jnp.cumsum(nblk)
    starts = ends - nblk
    n_live = ends[-1]
    step = jnp.minimum(jnp.arange(n_blocks, dtype=jnp.int32), n_live - 1)
    blk_e = jnp.sum(step[:, None] >= ends[None, :], axis=1).astype(jnp.int32)
    local = step - starts[blk_e]
    blk_valid = jnp.clip(counts[blk_e] - local * r, 0, r)
    return (starts * r, blk_e, step, blk_valid.astype(jnp.int32),
            n_live.reshape(1).astype(jnp.int32), n_blocks)


def kernel(x, norm_mix, norm_ffn, w_in_even, pool_w, pool_scale, w_out_even, ffn_w_gate,
           ffn_w_up, ffn_w_down, w_in_odd, lower_bound_logits, hgrn_out_norm, w_out_odd,
           router_w, moe_w_gate, moe_w_up, moe_w_down, norm_final):
    bsz, seq, d = x.shape
    t = bsz * seq

    lb_all = jnp.cumsum(jax.nn.softmax(lower_bound_logits.astype(F32), axis=0), axis=0)
    lb_all = lb_all - lb_all[:1]

    n_in = w_in_even.shape[2]
    q_scale = math.log2(math.e) / math.sqrt(HEAD_DIM)
    col_scale = jnp.where(jnp.arange(n_in) < A_WIDTH, q_scale, 1.0).astype(F32)
    proj = _norm_matmul_res(x, norm_mix[0], w_in_even[0].astype(BF16), col_scale)
    mix = _attn_pool(proj.reshape(bsz, seq, -1), pool_w[0], pool_scale[0])
    xt = _matmul_unres(mix.reshape(t, -1), w_out_even[0].astype(BF16), x)
    xt = _swiglu_res(xt, norm_ffn[0], ffn_w_gate[0].astype(BF16), ffn_w_up[0].astype(BF16),
                     ffn_w_down[0].astype(BF16))

    wid = HGRN_HEADS * HEAD_DIM
    col = jnp.arange(4 * wid) // wid
    proj = _norm_matmul(xt, norm_mix[1], w_in_odd[0].astype(BF16),
                        jnp.where(col == 2, 1.0, 0.5).astype(F32))
    o = _hgrn(proj.reshape(bsz, seq, -1), lb_all[1], hgrn_out_norm[0])
    xt = _matmul_res(o.reshape(t, -1), w_out_odd[0].astype(BF16), xt)

    h, meta, cnt = _router(xt, norm_ffn[1], router_w[0])
    counts = cnt[0, :N_EXPERTS].astype(jnp.int32)
    row0, blk_e, blk_idx, blk_valid, n_live, n_blocks = _moe_block_table(counts, t)
    dest = (row0[meta[:, 0:2].astype(jnp.int32)] + meta[:, 2:4].astype(jnp.int32)).reshape(-1)
    x_rows = _dispatch(h, dest, n_blocks * MOE_ROWS)
    y_rows = _moe_experts(x_rows, moe_w_gate[0], moe_w_up[0], moe_w_down[0],
                          blk_e, blk_idx, blk_valid, n_live, n_blocks)
    out = _combine(xt, meta, norm_final, y_rows, dest)
    return out.reshape(bsz, seq, d)
```

```python
import math

import jax
import jax.numpy as jnp
import numpy as np
from jax import lax
from jax.experimental import pallas as pl
from jax.experimental.pallas import tpu as pltpu

F32 = jnp.float32
BF16 = jnp.bfloat16

D_MODEL = 2048
HEAD_DIM = 128
A_HEADS = 12
A_WIDTH = A_HEADS * HEAD_DIM
N_POOL = 4
POOL_WINDOWS = (2, 4, 8, 16)
HGRN_HEADS = 16
HGRN_CHUNK = 64
N_EXPERTS = 8
EPS = 1e-6
NEG = -1e30

LANES = 128
VMEM_LIMIT = 56 * 1024 * 1024

MOE_ROWS = 1024
MOE_PARTS = 4
MOE_TF = 512
BIG_VMEM_LIMIT = 62 * 1024 * 1024
HI16 = 0xFFFF0000


def _pack_bf16_pair(x):
    n = x.shape[1] // 2
    bits = lax.bitcast_convert_type(x.astype(BF16).astype(F32), jnp.uint32)
    return (bits[:, n:] & jnp.uint32(HI16)) | (bits[:, :n] >> 16)


def _unpack_bf16_pair(p):
    lo = lax.bitcast_convert_type(p << 16, F32).astype(BF16)
    hi = lax.bitcast_convert_type(p & jnp.uint32(HI16), F32).astype(BF16)
    return lo, hi

N_RES = 16


def _cparams(sem, vmem=VMEM_LIMIT, flags=None):
    return pltpu.CompilerParams(dimension_semantics=sem, vmem_limit_bytes=vmem, flags=flags)


def _rms(x, g):
    ms = jnp.mean(x * x, axis=-1, keepdims=True)
    return x * lax.rsqrt(ms + EPS) * g


def _half_silu(h):
    return h + h * jnp.tanh(h)


def _silu(x):
    return _half_silu(0.5 * x)


def _dot(a, b):
    return jnp.dot(a, b, preferred_element_type=F32)


def _dot_nt(a, b):
    return lax.dot_general(a, b, (((1,), (1,)), ((), ())), preferred_element_type=F32)


def _dot_tn(a, b):
    return lax.dot_general(a, b, (((0,), (0,)), ((), ())), preferred_element_type=F32)


def _norm_matmul_kernel(x_ref, g_ref, w_ref, cs_ref, o_ref, h_scr):
    @pl.when(pl.program_id(1) == 0)
    def _():
        h_scr[...] = _rms(x_ref[...], g_ref[...]).astype(BF16)

    o_ref[...] = (_dot(h_scr[...], w_ref[...]) * cs_ref[...]).astype(o_ref.dtype)


def _norm_matmul(x, g, w, col_scale, tm=1024, tn=2048):
    m, d = x.shape
    n = w.shape[1]
    return pl.pallas_call(
        _norm_matmul_kernel,
        grid=(m // tm, n // tn),
        in_specs=[
            pl.BlockSpec((tm, d), lambda i, j: (i, 0)),
            pl.BlockSpec((1, d), lambda i, j: (0, 0)),
            pl.BlockSpec((d, tn), lambda i, j: (0, j)),
            pl.BlockSpec((1, tn), lambda i, j: (0, j)),
        ],
        out_specs=pl.BlockSpec((tm, tn), lambda i, j: (i, j)),
        out_shape=jax.ShapeDtypeStruct((m, n), BF16),
        scratch_shapes=[pltpu.VMEM((tm, d), BF16)],
        compiler_params=_cparams(("parallel", "arbitrary")),
        name="norm_matmul",
    )(x, g.reshape(1, d), w, col_scale.reshape(1, n))


def _norm_matmul_res_kernel(x_hbm, g_ref, w_ref, cs_ref, o_ref, xbuf, h_scr, sem):
    i = pl.program_id(0)
    n_res_tile, rows, _ = xbuf.shape[1:]
    halves = N_RES // n_res_tile

    def fetch(step, slot):
        b = step // halves
        r0 = (step % halves) * n_res_tile
        return [pltpu.make_async_copy(x_hbm.at[b, :, r0 + r, :], xbuf.at[slot, r], sem.at[slot, r])
                for r in range(n_res_tile)]

    @pl.when(pl.program_id(1) == 0)
    def _():
        slot = i % 2

        @pl.when(i == 0)
        def _():
            for c in fetch(i, slot):
                c.start()

        @pl.when(i + 1 < pl.num_programs(0))
        def _():
            for c in fetch(i + 1, 1 - slot):
                c.start()

        for r, c in enumerate(fetch(i, slot)):
            c.wait()
            h_scr[r * rows:(r + 1) * rows, :] = _rms(xbuf[slot, r], g_ref[...]).astype(BF16)

    o_ref[...] = (_dot(h_scr[...], w_ref[...]) * cs_ref[...]).astype(o_ref.dtype)


def _norm_matmul_res(x, g, w, col_scale, n_res_tile=8, tn=2560):
    bsz, seq, d = x.shape
    n = w.shape[1]
    rows = seq // N_RES
    tm = n_res_tile * rows
    halves = N_RES // n_res_tile
    return pl.pallas_call(
        _norm_matmul_res_kernel,
        grid=(bsz * halves, n // tn),
        in_specs=[
            pl.BlockSpec(memory_space=pl.ANY),
            pl.BlockSpec((1, d), lambda i, j: (0, 0)),
            pl.BlockSpec((d, tn), lambda i, j: (0, j)),
            pl.BlockSpec((1, tn), lambda i, j: (0, j)),
        ],
        out_specs=pl.BlockSpec((tm, tn), lambda i, j: (i, j)),
        out_shape=jax.ShapeDtypeStruct((bsz * seq, n), BF16),
        scratch_shapes=[pltpu.VMEM((2, n_res_tile, rows, d), F32), pltpu.VMEM((tm, d), BF16),
                        pltpu.SemaphoreType.DMA((2, n_res_tile))],
        compiler_params=_cparams(("arbitrary", "arbitrary")),
        name="norm_matmul_res",
    )(x.reshape(bsz, rows, N_RES, d), g.reshape(1, d), w, col_scale.reshape(1, n))


def _matmul_res_kernel(a_ref, w_ref, r_ref, o_ref):
    o_ref[...] = r_ref[...] + _dot(a_ref[...], w_ref[...])


def _matmul_res(a, w, res, tm=512, tn=2048):
    m, k = a.shape
    n = w.shape[1]
    return pl.pallas_call(
        _matmul_res_kernel,
        grid=(m // tm, n // tn),
        in_specs=[
            pl.BlockSpec((tm, k), lambda i, j: (i, 0)),
            pl.BlockSpec((k, tn), lambda i, j: (0, j)),
            pl.BlockSpec((tm, tn), lambda i, j: (i, j)),
        ],
        out_specs=pl.BlockSpec((tm, tn), lambda i, j: (i, j)),
        out_shape=jax.ShapeDtypeStruct((m, n), F32),
        compiler_params=_cparams(("parallel", "arbitrary")),
        name="matmul_res",
    )(a, w, res)


def _matmul_unres_kernel(a_ref, w_ref, r_hbm, o_hbm, rbuf, obuf, rsem, osem):
    i = pl.program_id(0)
    n_steps = pl.num_programs(0)
    n_res_tile, rows, _ = rbuf.shape[1:]
    parts = N_RES // n_res_tile

    def res_fetch(step, slot):
        b, r0 = step // parts, (step % parts) * n_res_tile
        return [pltpu.make_async_copy(r_hbm.at[b, :, r0 + r, :], rbuf.at[slot, r], rsem.at[slot, r])
                for r in range(n_res_tile)]

    def out_store(step, slot):
        b, r0 = step // parts, (step % parts) * n_res_tile
        return [pltpu.make_async_copy(obuf.at[slot, r], o_hbm.at[b, :, r0 + r, :], osem.at[slot, r])
                for r in range(n_res_tile)]

    slot = i % 2

    @pl.when(i == 0)
    def _():
        for c in res_fetch(i, slot):
            c.start()

    @pl.when(i + 1 < n_steps)
    def _():
        for c in res_fetch(i + 1, 1 - slot):
            c.start()

    y = _dot(a_ref[...], w_ref[...])

    @pl.when(i >= 2)
    def _():
        for c in out_store(i - 2, slot):
            c.wait()

    for r, c in enumerate(res_fetch(i, slot)):
        c.wait()
        obuf[slot, r] = rbuf[slot, r] + y[r * rows:(r + 1) * rows, :]
    for c in out_store(i, slot):
        c.start()

    @pl.when(i == n_steps - 1)
    def _():
        for c in out_store(i - 1, 1 - slot) + out_store(i, slot):
            c.wait()


def _matmul_unres(a, w, res, n_res_tile=4):
    bsz, seq, d = res.shape
    k = a.shape[1]
    rows = seq // N_RES
    tm = n_res_tile * rows
    parts = N_RES // n_res_tile
    assert bsz * parts >= 2
    buf = pltpu.VMEM((2, n_res_tile, rows, d), F32)
    out = pl.pallas_call(
        _matmul_unres_kernel,
        grid=(bsz * parts,),
        in_specs=[
            pl.BlockSpec((tm, k), lambda i: (i, 0)),
            pl.BlockSpec((k, d), lambda i: (0, 0)),
            pl.BlockSpec(memory_space=pl.ANY),
        ],
        out_specs=pl.BlockSpec(memory_space=pl.ANY),
        out_shape=jax.ShapeDtypeStruct((bsz, rows, N_RES, d), F32),
        scratch_shapes=[buf, buf, pltpu.SemaphoreType.DMA((2, n_res_tile)),
                        pltpu.SemaphoreType.DMA((2, n_res_tile))],
        compiler_params=_cparams(("arbitrary",)),
        name="matmul_unres",
    )(a, w, res.reshape(bsz, rows, N_RES, d))
    return out.reshape(bsz * seq, d)


def _swiglu_kernel(x_ref, g_ref, wg_ref, wu_ref, wd_ref, o_ref, h_scr):
    @pl.when(pl.program_id(1) == 0)
    def _():
        h_scr[...] = _rms(x_ref[...], g_ref[...]).astype(BF16)
        o_ref[...] = x_ref[...]

    h = h_scr[...]
    a = _dot(h, wg_ref[...])
    b = _dot(h, wu_ref[...])
    hid = (_silu(a) * b).astype(BF16)
    o_ref[...] += _dot(hid, wd_ref[...])


def _swiglu_res(x, g, wg, wu, wd, tm=1024, tf=512):
    m, d = x.shape
    ff = wg.shape[1]
    return pl.pallas_call(
        _swiglu_kernel,
        grid=(m // tm, ff // tf),
        in_specs=[
            pl.BlockSpec((tm, d), lambda i, j: (i, 0)),
            pl.BlockSpec((1, d), lambda i, j: (0, 0)),
            pl.BlockSpec((d, tf), lambda i, j: (0, j)),
            pl.BlockSpec((d, tf), lambda i, j: (0, j)),
            pl.BlockSpec((tf, d), lambda i, j: (j, 0)),
        ],
        out_specs=pl.BlockSpec((tm, d), lambda i, j: (i, 0)),
        out_shape=jax.ShapeDtypeStruct((m, d), F32),
        scratch_shapes=[pltpu.VMEM((tm, d), BF16)],
        compiler_params=_cparams(("parallel", "arbitrary"), vmem=BIG_VMEM_LIMIT),
        name="swiglu_res",
    )(x, g.reshape(1, d), wg, wu, wd)


def _attn_bias_table():
    blk = HEAD_DIM
    iq = np.arange(blk)[:, None]
    jk = np.arange(2 * blk)[None, :]

    def offs(idx, n_chunk, size):
        return n_chunk * (idx % size) + idx // size

    masks = []
    for n_chunk in (1, 4, 16):
        size = blk // n_chunk
        lq = offs(iq, n_chunk, size)
        masks.append((offs(jk, n_chunk, size) <= lq) & (jk < blk))
        dist = lq - (offs(jk, n_chunk, 2 * size) - blk)
        masks.append((dist >= 0) & (dist <= blk))
    return np.where(np.stack(masks), 0.0, NEG).astype(np.float32)


def _attn_pool_kernel(q_ref, k_ref, v_ref, bias, pw_ref, ps_ref, o_ref,
                      qf, kf, vf, ms, ls, accs):
    h = pl.program_id(1)
    seq = q_ref.shape[0]
    blk = HEAD_DIM
    bf16_rows = 16

    @pl.when(h < A_HEADS)
    def _attention():
        qf[...] = q_ref[...].astype(F32)
        kf[...] = k_ref[...].astype(F32)
        vf[...] = v_ref[...].astype(F32)

        def gather(ref, ref_f32, starts, size):
            if all(v % bf16_rows == 0 for v in [size] + starts):
                return jnp.concatenate([ref[s0:s0 + size, :] for s0 in starts], axis=0)
            return jnp.concatenate([ref_f32[s0:s0 + size, :] for s0 in starts],
                                   axis=0).astype(BF16)

        def scatter(br, starts, size, m, l, acc):
            mb = jnp.broadcast_to(m, acc.shape)
            lb = jnp.broadcast_to(l, acc.shape)
            for c, s0 in enumerate(starts):
                src = slice(c * size, (c + 1) * size)
                ms[br, s0:s0 + size, :] = mb[src]
                ls[br, s0:s0 + size, :] = lb[src]
                accs[br, s0:s0 + size, :] = acc[src]

        kinds = {}
        for br, n_chunk in enumerate((1, 4, 16)):
            size = blk // n_chunk
            for grp in range(N_RES // n_chunk):
                bases = [(grp + (N_RES // n_chunk) * c) * blk for c in range(n_chunk)]
                for n in range(n_chunk):
                    qs = [b0 + n * size for b0 in bases]
                    if n == 0:
                        kinds.setdefault((2 * br, blk), []).append((br, size, qs, qs, size))
                    else:
                        ks = [s0 - size for s0 in qs]
                        kinds.setdefault((2 * br + 1, 2 * blk), []).append(
                            (br, size, qs, ks, 2 * size))

        batch = 16
        for (bias_idx, kw), blocks in kinds.items():
            bi = bias[bias_idx][:, :kw]
            for g0 in range(0, len(blocks), batch):
                group = blocks[g0:g0 + batch]
                s = jnp.concatenate(
                    [_dot_nt(gather(q_ref, qf, qs, size), gather(k_ref, kf, ks, ksize))
                     for _, size, qs, ks, ksize in group], axis=0)
                s = (s.reshape(len(group), blk, kw) + bi[None]).reshape(len(group) * blk, kw)
                m = jnp.max(s, axis=-1, keepdims=True)
                pb = jnp.exp2(s - m).astype(BF16)
                ones = jnp.ones((kw, blk), BF16)
                for g, (br, size, qs, ks, ksize) in enumerate(group):
                    rows = slice(g * blk, (g + 1) * blk)
                    v_one = jnp.concatenate([gather(v_ref, vf, ks, ksize), ones], axis=1)
                    acc = _dot(pb[rows], v_one)
                    scatter(br, qs, size, m[rows], acc[:, blk:], acc[:, :blk])

        slab = 2 * blk
        for s0 in range(0, seq, slab):
            rows = slice(s0, s0 + slab)
            m_all = jnp.maximum(jnp.maximum(ms[0, rows, :], ms[1, rows, :]), ms[2, rows, :])
            num = jnp.zeros((slab, blk), F32)
            den = jnp.zeros((slab, blk), F32)
            for br in range(3):
                w = jnp.exp2(ms[br, rows, :] - m_all)
                num = num + accs[br, rows, :] * w
                den = den + ls[br, rows, :] * w
            o_ref[rows, :] = (num / den).astype(o_ref.dtype)

    for gi, w in enumerate(POOL_WINDOWS):
        @pl.when(h == A_HEADS + gi)
        def _pool(gi=gi, w=w):
            pf = q_ref[...].astype(F32)
            row = lax.broadcasted_iota(jnp.int32, pf.shape, 0)
            l_idx = row % blk
            pos = N_RES * l_idx + row // blk
            s = pf
            k = 1
            while k < w:
                y = pltpu.roll(s, k * blk, axis=0)
                y1 = jnp.where(l_idx == 0, 0.0, pltpu.roll(y, 1, axis=0))
                s = s + jnp.where(row < k * blk, y1, y)
                k *= 2
            cnt = jnp.minimum(pos + 1, w).astype(F32)
            pooled = s / cnt - pf
            y = _dot(pooled.astype(BF16), pw_ref[gi].astype(BF16)) * ps_ref[...]
            o_ref[...] = y.astype(o_ref.dtype)


def _attn_pool(proj, pool_w, pool_scale):
    bsz, seq, _ = proj.shape
    n_steps = A_HEADS + N_POOL
    qmap = lambda b, h: (b, 0, jnp.where(h < A_HEADS, h, h + 2 * A_HEADS))
    kmap = lambda b, h: (b, 0, A_HEADS + jnp.minimum(h, A_HEADS - 1))
    vmap = lambda b, h: (b, 0, 2 * A_HEADS + jnp.minimum(h, A_HEADS - 1))
    f32s = pltpu.VMEM((seq, HEAD_DIM), F32)
    br3 = pltpu.VMEM((3, seq, HEAD_DIM), F32)
    bias = jnp.asarray(_attn_bias_table())
    return pl.pallas_call(
        _attn_pool_kernel,
        grid=(bsz, n_steps),
        in_specs=[
            pl.BlockSpec((None, seq, HEAD_DIM), qmap),
            pl.BlockSpec((None, seq, HEAD_DIM), kmap),
            pl.BlockSpec((None, seq, HEAD_DIM), vmap),
            pl.BlockSpec(bias.shape, lambda b, h: (0, 0, 0)),
            pl.BlockSpec((N_POOL, HEAD_DIM, HEAD_DIM), lambda b, h: (0, 0, 0)),
            pl.BlockSpec((1, HEAD_DIM), lambda b, h: (0, jnp.maximum(h - A_HEADS, 0))),
        ],
        out_specs=pl.BlockSpec((None, seq, HEAD_DIM), lambda b, h: (b, 0, h)),
        out_shape=jax.ShapeDtypeStruct((bsz, seq, n_steps * HEAD_DIM), BF16),
        scratch_shapes=[f32s, f32s, f32s, br3, br3, br3],
        compiler_params=_cparams(("parallel", "arbitrary")),
        name="attn_pool",
    )(proj, proj, proj, bias, pool_w, pool_scale.reshape(1, N_POOL * HEAD_DIM))


def _hgrn_kernel(q_ref, f_ref, i_ref, gt_ref, lb_ref, on_ref, o_ref,
                 qe_scr, o_scr, u_scr, d_scr, s_scr):
    seq = q_ref.shape[0]
    ck = HGRN_CHUNK
    grp = 4 * ck
    n_grp = seq // grp
    n_chunk = seq // ck
    lb = lb_ref[...]

    ri = lax.broadcasted_iota(jnp.int32, (grp, grp), 0)
    ci = lax.broadcasted_iota(jnp.int32, (grp, grp), 1)
    tri_mask = (ri // ck == ci // ck) & (ci <= ri)
    tri = tri_mask.astype(BF16)

    groups = [slice(gi * grp, (gi + 1) * grp) for gi in range(n_grp)]
    q = _half_silu(q_ref[...].astype(F32))
    f = (0.5 + 0.5 * lb) + (0.5 - 0.5 * lb) * jnp.tanh(f_ref[...].astype(F32))
    k = 1.0 - f
    g = jnp.log2(f)
    g_hi = g.astype(BF16)
    g_lo = (g - g_hi.astype(F32)).astype(BF16)
    b = jnp.concatenate([_dot(tri, g_hi[r]) + _dot(tri, g_lo[r]) for r in groups], axis=0)
    b3 = b.reshape(n_chunk, ck, HEAD_DIM)
    b_mid = b3[:, ck // 2 - 1:ck // 2, :]
    b_last = b3[:, ck - 1:ck, :]
    q3 = q.reshape(n_chunk, ck, HEAD_DIM)
    k3 = k.reshape(n_chunk, ck, HEAD_DIM)
    q_mid = q3 * jnp.exp2(b3 - b_mid)
    k_mid = k3 * jnp.exp2(b_mid - b3)
    qs = q_mid.reshape(seq, HEAD_DIM).astype(BF16)
    ks = k_mid.reshape(seq, HEAD_DIM).astype(BF16)
    qe_scr[...] = (q_mid * jnp.exp2(b_mid)).reshape(seq, HEAD_DIM).astype(BF16)
    kd = (k_mid * jnp.exp2(b_last - b_mid)).reshape(seq, HEAD_DIM).astype(BF16)
    d_scr[...] = jnp.exp2(b_last).reshape(n_chunk, HEAD_DIM)
    for r in groups:
        att = jnp.where(tri_mask, _dot_nt(qs[r], ks[r]), 0.0)
        o_scr[r, :] = _dot(att.astype(BF16), i_ref[r, :])
    for c in range(n_chunk):
        cr = slice(c * ck, (c + 1) * ck)
        u_scr[c] = _dot_tn(i_ref[cr, :], kd[cr])

    st = jnp.zeros((HEAD_DIM, HEAD_DIM), F32)
    for c in range(n_chunk):
        s_scr[c] = st.astype(BF16)
        st = st * d_scr[c:c + 1, :] + u_scr[c]

    for gi in range(n_grp):
        rows = slice(gi * grp, (gi + 1) * grp)
        inter = jnp.concatenate(
            [_dot_nt(qe_scr[gi * grp + c * ck:gi * grp + (c + 1) * ck, :], s_scr[gi * 4 + c])
             for c in range(4)], axis=0)
        o = _rms(o_scr[rows, :] + inter, on_ref[...])
        o_ref[rows, :] = (o * _half_silu(gt_ref[rows, :].astype(F32))).astype(o_ref.dtype)


def _hgrn(proj, lb, out_norm):
    bsz, seq, _ = proj.shape
    nh = HGRN_HEADS
    n_chunk = seq // HGRN_CHUNK
    spec = lambda off: pl.BlockSpec((None, seq, HEAD_DIM), lambda b, h: (b, 0, off + h))
    vec = pl.BlockSpec((1, HEAD_DIM), lambda b, h: (0, h))
    return pl.pallas_call(
        _hgrn_kernel,
        grid=(bsz, nh),
        in_specs=[spec(0), spec(nh), spec(2 * nh), spec(3 * nh), vec, vec],
        out_specs=pl.BlockSpec((None, seq, HEAD_DIM), lambda b, h: (b, 0, h)),
        out_shape=jax.ShapeDtypeStruct((bsz, seq, nh * HEAD_DIM), BF16),
        scratch_shapes=[
            pltpu.VMEM((seq, HEAD_DIM), BF16),
            pltpu.VMEM((seq, HEAD_DIM), F32),
            pltpu.VMEM((n_chunk, HEAD_DIM, HEAD_DIM), F32),
            pltpu.VMEM((n_chunk, HEAD_DIM), F32),
            pltpu.VMEM((n_chunk, HEAD_DIM, HEAD_DIM), BF16),
        ],
        compiler_params=_cparams(("parallel", "arbitrary")),
        name="hgrn2",
    )(proj, proj, proj, proj, lb.reshape(1, -1), out_norm.reshape(1, -1))


def _router_kernel(x_ref, g_ref, w_ref, h_ref, meta_ref, cnt_ref, carry):
    i = pl.program_id(0)
    tm = x_ref.shape[0]

    @pl.when(i == 0)
    def _():
        carry[...] = jnp.zeros_like(carry)

    h = _rms(x_ref[...], g_ref[...])
    h_ref[...] = _pack_bf16_pair(h)
    h_hi = h.astype(BF16)
    h_lo = (h - h_hi.astype(F32)).astype(BF16)
    w = w_ref[...]
    w_hi = w.astype(BF16)
    w_lo = (w - w_hi.astype(F32)).astype(BF16)
    prod = _dot(jnp.concatenate([h_hi, h_lo], axis=0), jnp.concatenate([w_hi, w_lo], axis=1))
    logits = (prod[:tm, :LANES] + prod[:tm, LANES:]) + (prod[tm:, :LANES] + prod[tm:, LANES:])
    lane = lax.broadcasted_iota(jnp.int32, logits.shape, 1)
    logits = jnp.where(lane < N_EXPERTS, logits, NEG)
    m1 = jnp.max(logits, axis=-1, keepdims=True)
    i1 = jnp.min(jnp.where(logits == m1, lane, LANES), axis=-1, keepdims=True)
    rest = jnp.where(lane == i1, NEG, logits)
    m2 = jnp.max(rest, axis=-1, keepdims=True)
    i2 = jnp.min(jnp.where(rest == m2, lane, LANES), axis=-1, keepdims=True)
    e = jnp.exp(m2 - m1)
    g1 = 1.0 / (1.0 + e)
    g2 = e / (1.0 + e)
    oh1 = (lane == i1).astype(F32)
    oh2 = (lane == i2).astype(F32)
    cnt = oh1 + oh2
    ri = lax.broadcasted_iota(jnp.int32, (tm, tm), 0)
    ci = lax.broadcasted_iota(jnp.int32, (tm, tm), 1)
    before = (ci < ri).astype(BF16)
    prior = _dot(before, cnt.astype(BF16)) + carry[...]
    r1 = jnp.sum(oh1 * prior, axis=-1, keepdims=True)
    r2 = jnp.sum(oh2 * prior, axis=-1, keepdims=True)
    carry[...] += jnp.sum(cnt, axis=0, keepdims=True)
    cnt_ref[...] = carry[...]
    meta = jnp.where(lane == 0, i1.astype(F32), 0.0)
    meta = jnp.where(lane == 1, i2.astype(F32), meta)
    meta = jnp.where(lane == 2, r1, meta)
    meta = jnp.where(lane == 3, r2, meta)
    meta = jnp.where(lane == 4, g1, meta)
    meta = jnp.where(lane == 5, g2, meta)
    meta_ref[...] = meta


def _router(x, g, router_w, tm=512):
    t, d = x.shape
    w_pad = jnp.pad(router_w, ((0, 0), (0, LANES - router_w.shape[1])))
    return pl.pallas_call(
        _router_kernel,
        grid=(t // tm,),
        in_specs=[
            pl.BlockSpec((tm, d), lambda i: (i, 0)),
            pl.BlockSpec((1, d), lambda i: (0, 0)),
            pl.BlockSpec((d, LANES), lambda i: (0, 0)),
        ],
        out_specs=[
            pl.BlockSpec((tm, d // 2), lambda i: (i, 0)),
            pl.BlockSpec((tm, LANES), lambda i: (i, 0)),
            pl.BlockSpec((1, LANES), lambda i: (0, 0)),
        ],
        out_shape=[
            jax.ShapeDtypeStruct((t, d // 2), jnp.uint32),
            jax.ShapeDtypeStruct((t, LANES), F32),
            jax.ShapeDtypeStruct((1, LANES), F32),
        ],
        scratch_shapes=[pltpu.VMEM((1, LANES), F32)],
        compiler_params=_cparams(("arbitrary",)),
        name="moe_router",
    )(x, g.reshape(1, d), w_pad)


def _row_copy(src, dst, s, d, sem):
    return pltpu.make_async_copy(src.at[pl.ds(s, 1)], dst.at[pl.ds(d, 1)], sem)


def _dispatch_kernel(dest_ref, h_ref, o_hbm, sem):
    tm = h_ref.shape[0]

    def start(t, carry):
        _row_copy(h_ref, o_hbm, t, dest_ref[0, 0, 2 * t], sem).start(priority=0)
        _row_copy(h_ref, o_hbm, t, dest_ref[0, 0, 2 * t + 1], sem).start(priority=1)
        return carry

    lax.fori_loop(0, tm, start, 0, unroll=8)

    def wait(t, carry):
        _row_copy(h_ref, o_hbm, 0, 0, sem).wait()
        _row_copy(h_ref, o_hbm, 0, 0, sem).wait()
        return carry

    lax.fori_loop(0, tm, wait, 0, unroll=8)


def _dispatch(h, dest, n_rows, tm=1024):
    t, d = h.shape
    dest3 = dest.reshape(t // tm, 1, 2 * tm)
    return pl.pallas_call(
        _dispatch_kernel,
        grid=(t // tm,),
        in_specs=[
            pl.BlockSpec((1, 1, 2 * tm), lambda i: (i, 0, 0), memory_space=pltpu.SMEM),
            pl.BlockSpec((tm, d), lambda i: (i, 0)),
        ],
        out_specs=pl.BlockSpec(memory_space=pl.ANY),
        out_shape=jax.ShapeDtypeStruct((n_rows, d), h.dtype),
        scratch_shapes=[pltpu.SemaphoreType.DMA(())],
        compiler_params=_cparams(("arbitrary",)),
        name="moe_dispatch",
    )(dest3, h)


def _moe_kernel(be_ref, bb_ref, nv_ref, na_ref, x_ref, wg_ref, wu_ref, wd_ref, o_ref, xb_scr):
    i = pl.program_id(0)
    j = pl.program_id(1)
    n_valid = nv_ref[i]

    @pl.when(i < na_ref[0])
    def _():
        @pl.when(j == 0)
        def _():
            row = lax.broadcasted_iota(jnp.int32, x_ref.shape, 0)
            lo, hi = _unpack_bf16_pair(jnp.where(row < n_valid, x_ref[...], jnp.uint32(0)))
            xb_scr[:, :lo.shape[1]] = lo
            xb_scr[:, lo.shape[1]:] = hi
            o_ref[...] = jnp.zeros_like(o_ref)

        def run(rows):
            xb = xb_scr[rows, :]
            a = _dot(xb, wg_ref[0].astype(BF16))
            b = _dot(xb, wu_ref[0].astype(BF16))
            hid = (_silu(a) * b).astype(BF16)
            o_ref[rows, :] += _dot(hid, wd_ref[0].astype(BF16))

        part = x_ref.shape[0] // MOE_PARTS
        for nq in range(1, MOE_PARTS + 1):
            @pl.when((n_valid > (nq - 1) * part) & (n_valid <= nq * part))
            def _(nq=nq):
                run(slice(0, nq * part))


def _moe_experts(x_rows, wg, wu, wd, blk_e, blk_idx, blk_valid, n_live, n_blocks):
    dp = x_rows.shape[1]
    d = 2 * dp
    dff = wg.shape[2]
    n_ff = dff // MOE_TF
    r = MOE_ROWS

    def ffi(i, j, na):
        return jnp.where(i < na[0], j, n_ff - 1)

    grid_spec = pltpu.PrefetchScalarGridSpec(
        num_scalar_prefetch=4,
        grid=(n_blocks, n_ff),
        in_specs=[
            pl.BlockSpec((r, dp), lambda i, j, be, bb, nv, na: (bb[i], 0)),
            pl.BlockSpec((1, d, MOE_TF), lambda i, j, be, bb, nv, na: (be[i], 0, ffi(i, j, na))),
            pl.BlockSpec((1, d, MOE_TF), lambda i, j, be, bb, nv, na: (be[i], 0, ffi(i, j, na))),
            pl.BlockSpec((1, MOE_TF, d), lambda i, j, be, bb, nv, na: (be[i], ffi(i, j, na), 0)),
        ],
        out_specs=pl.BlockSpec((r, d), lambda i, j, be, bb, nv, na: (bb[i], 0)),
        scratch_shapes=[pltpu.VMEM((r, d), BF16)],
    )
    return pl.pallas_call(
        _moe_kernel,
        grid_spec=grid_spec,
        out_shape=jax.ShapeDtypeStruct((x_rows.shape[0], d), F32),
        compiler_params=_cparams(("arbitrary", "arbitrary"), vmem=BIG_VMEM_LIMIT),
        name="moe_experts",
    )(blk_e, blk_idx, blk_valid, n_live, x_rows, wg, wu, wd)


def _combine_kernel(dest_ref, next_ref, x_ref, meta_ref, nf_ref, y_hbm, o_ref, buf, sem):
    i = pl.program_id(0)
    tm = x_ref.shape[0]
    slot = i % 2

    def gather(idx_ref, s):
        def start(t, carry):
            _row_copy(y_hbm, buf.at[s, 0], idx_ref[0, 0, 2 * t], t, sem.at[s]).start(priority=0)
            _row_copy(y_hbm, buf.at[s, 1], idx_ref[0, 0, 2 * t + 1], t, sem.at[s]).start(priority=1)
            return carry

        lax.fori_loop(0, tm, start, 0, unroll=8)

    @pl.when(i == 0)
    def _():
        gather(dest_ref, slot)

    @pl.when(i + 1 < pl.num_programs(0))
    def _():
        gather(next_ref, 1 - slot)

    def wait(t, carry):
        _row_copy(y_hbm, buf.at[slot, 0], 0, 0, sem.at[slot]).wait()
        _row_copy(y_hbm, buf.at[slot, 1], 0, 0, sem.at[slot]).wait()
        return carry

    lax.fori_loop(0, tm, wait, 0, unroll=8)

    meta = meta_ref[...]
    g1 = meta[:, 4:5]
    g2 = meta[:, 5:6]
    x = x_ref[...] + (g1 * buf[slot, 0] + g2 * buf[slot, 1])
    o_ref[...] = _rms(x, nf_ref[...])


def _combine(x, meta, norm_final, y_rows, dest, tm=512):
    t, d = x.shape
    n_steps = t // tm
    dest3 = dest.reshape(n_steps, 1, 2 * tm)
    return pl.pallas_call(
        _combine_kernel,
        grid=(n_steps,),
        in_specs=[
            pl.BlockSpec((1, 1, 2 * tm), lambda i: (i, 0, 0), memory_space=pltpu.SMEM),
            pl.BlockSpec((1, 1, 2 * tm), lambda i: (jnp.minimum(i + 1, n_steps - 1), 0, 0),
                         memory_space=pltpu.SMEM),
            pl.BlockSpec((tm, d), lambda i: (i, 0)),
            pl.BlockSpec((tm, LANES), lambda i: (i, 0)),
            pl.BlockSpec((1, d), lambda i: (0, 0)),
            pl.BlockSpec(memory_space=pl.ANY),
        ],
        out_specs=pl.BlockSpec((tm, d), lambda i: (i, 0)),
        out_shape=jax.ShapeDtypeStruct((t, d), F32),
        scratch_shapes=[pltpu.VMEM((2, 2, tm, d), F32), pltpu.SemaphoreType.DMA((2,))],
        compiler_params=_cparams(("arbitrary",)),
        name="moe_combine",
    )(dest3, dest3, x, meta, norm_final.reshape(1, d), y_rows)


def _moe_block_table(counts, t):
    r = MOE_ROWS
    n_blocks = (2 * t) // r + N_EXPERTS
    nblk = (counts + r - 1) // r
    ends = jnp.cumsum(nblk)
    starts = ends - nblk
    n_live = ends[-1]
    step = jnp.minimum(jnp.arange(n_blocks, dtype=jnp.int32), n_live - 1)
    blk_e = jnp.sum(step[:, None] >= ends[None, :], axis=1).astype(jnp.int32)
    local = step - starts[blk_e]
    blk_valid = jnp.clip(counts[blk_e] - local * r, 0, r)
    return (starts * r, blk_e, step, blk_valid.astype(jnp.int32),
            n_live.reshape(1).astype(jnp.int32), n_blocks)


def kernel(x, norm_mix, norm_ffn, w_in_even, pool_w, pool_scale, w_out_even, ffn_w_gate,
           ffn_w_up, ffn_w_down, w_in_odd, lower_bound_logits, hgrn_out_norm, w_out_odd,
           router_w, moe_w_gate, moe_w_up, moe_w_down, norm_final):
    bsz, seq, d = x.shape
    t = bsz * seq

    lb_all = jnp.cumsum(jax.nn.softmax(lower_bound_logits.astype(F32), axis=0), axis=0)
    lb_all = lb_all - lb_all[:1]

    n_in = w_in_even.shape[2]
    q_scale = math.log2(math.e) / math.sqrt(HEAD_DIM)
    col_scale = jnp.where(jnp.arange(n_in) < A_WIDTH, q_scale, 1.0).astype(F32)
    proj = _norm_matmul_res(x, norm_mix[0], w_in_even[0].astype(BF16), col_scale)
    mix = _attn_pool(proj.reshape(bsz, seq, -1), pool_w[0], pool_scale[0])
    xt = _matmul_unres(mix.reshape(t, -1), w_out_even[0].astype(BF16), x)
    xt = _swiglu_res(xt, norm_ffn[0], ffn_w_gate[0].astype(BF16), ffn_w_up[0].astype(BF16),
                     ffn_w_down[0].astype(BF16))

    wid = HGRN_HEADS * HEAD_DIM
    col = jnp.arange(4 * wid) // wid
    proj = _norm_matmul(xt, norm_mix[1], w_in_odd[0].astype(BF16),
                        jnp.where(col == 2, 1.0, 0.5).astype(F32))
    o = _hgrn(proj.reshape(bsz, seq, -1), lb_all[1], hgrn_out_norm[0])
    xt = _matmul_res(o.reshape(t, -1), w_out_odd[0].astype(BF16), xt)

    h, meta, cnt = _router(xt, norm_ffn[1], router_w[0])
    counts = cnt[0, :N_EXPERTS].astype(jnp.int32)
    row0, blk_e, blk_idx, blk_valid, n_live, n_blocks = _moe_block_table(counts, t)
    dest = (row0[meta[:, 0:2].astype(jnp.int32)] + meta[:, 2:4].astype(jnp.int32)).reshape(-1)
    x_rows = _dispatch(h, dest, n_blocks * MOE_ROWS)
    y_rows = _moe_experts(x_rows, moe_w_gate[0], moe_w_up[0], moe_w_down[0],
                          blk_e, blk_idx, blk_valid, n_live, n_blocks)
    out = _combine(xt, meta, norm_final, y_rows, dest)
    return out.reshape(bsz, seq, d)
```

```python
import math

import jax
import jax.numpy as jnp
import numpy as np
from jax import lax
from jax.experimental import pallas as pl
from jax.experimental.pallas import tpu as pltpu

F32 = jnp.float32
BF16 = jnp.bfloat16

D_MODEL = 2048
HEAD_DIM = 128
A_HEADS = 12
A_WIDTH = A_HEADS * HEAD_DIM
N_POOL = 4
POOL_WINDOWS = (2, 4, 8, 16)
HGRN_HEADS = 16
HGRN_CHUNK = 64
N_EXPERTS = 8
EPS = 1e-6
NEG = -1e30

LANES = 128
VMEM_LIMIT = 56 * 1024 * 1024

MOE_ROWS = 1024
MOE_PARTS = 4
MOE_TF = 512
BIG_VMEM_LIMIT = 62 * 1024 * 1024
HI16 = 0xFFFF0000


def _pack_bf16_pair(x):
    n = x.shape[1] // 2
    bits = lax.bitcast_convert_type(x.astype(BF16).astype(F32), jnp.uint32)
    return (bits[:, n:] & jnp.uint32(HI16)) | (bits[:, :n] >> 16)


def _unpack_bf16_pair(p):
    lo = lax.bitcast_convert_type(p << 16, F32).astype(BF16)
    hi = lax.bitcast_convert_type(p & jnp.uint32(HI16), F32).astype(BF16)
    return lo, hi

N_RES = 16


def _cparams(sem, vmem=VMEM_LIMIT, flags=None):
    return pltpu.CompilerParams(dimension_semantics=sem, vmem_limit_bytes=vmem, flags=flags)


def _rms(x, g):
    ms = jnp.mean(x * x, axis=-1, keepdims=True)
    return x * lax.rsqrt(ms + EPS) * g


def _half_silu(h):
    return h + h * jnp.tanh(h)


def _silu(x):
    return _half_silu(0.5 * x)


def _dot(a, b):
    return jnp.dot(a, b, preferred_element_type=F32)


def _dot_nt(a, b):
    return lax.dot_general(a, b, (((1,), (1,)), ((), ())), preferred_element_type=F32)


def _dot_tn(a, b):
    return lax.dot_general(a, b, (((0,), (0,)), ((), ())), preferred_element_type=F32)


def _norm_matmul_kernel(x_ref, g_ref, w_ref, cs_ref, o_ref, h_scr):
    @pl.when(pl.program_id(1) == 0)
    def _():
        h_scr[...] = _rms(x_ref[...], g_ref[...]).astype(BF16)

    o_ref[...] = (_dot(h_scr[...], w_ref[...]) * cs_ref[...]).astype(o_ref.dtype)


def _norm_matmul(x, g, w, col_scale, tm=1024, tn=2048):
    m, d = x.shape
    n = w.shape[1]
    return pl.pallas_call(
        _norm_matmul_kernel,
        grid=(m // tm, n // tn),
        in_specs=[
            pl.BlockSpec((tm, d), lambda i, j: (i, 0)),
            pl.BlockSpec((1, d), lambda i, j: (0, 0)),
            pl.BlockSpec((d, tn), lambda i, j: (0, j)),
            pl.BlockSpec((1, tn), lambda i, j: (0, j)),
        ],
        out_specs=pl.BlockSpec((tm, tn), lambda i, j: (i, j)),
        out_shape=jax.ShapeDtypeStruct((m, n), BF16),
        scratch_shapes=[pltpu.VMEM((tm, d), BF16)],
        compiler_params=_cparams(("parallel", "arbitrary")),
        name="norm_matmul",
    )(x, g.reshape(1, d), w, col_scale.reshape(1, n))


def _norm_matmul_res_kernel(x_hbm, g_ref, w_ref, cs_ref, o_ref, xbuf, h_scr, sem):
    i = pl.program_id(0)
    n_res_tile, rows, _ = xbuf.shape[1:]
    halves = N_RES // n_res_tile

    def fetch(step, slot):
        b = step // halves
        r0 = (step % halves) * n_res_tile
        return [pltpu.make_async_copy(x_hbm.at[b, :, r0 + r, :], xbuf.at[slot, r], sem.at[slot, r])
                for r in range(n_res_tile)]

    @pl.when(pl.program_id(1) == 0)
    def _():
        slot = i % 2

        @pl.when(i == 0)
        def _():
            for c in fetch(i, slot):
                c.start()

        @pl.when(i + 1 < pl.num_programs(0))
        def _():
            for c in fetch(i + 1, 1 - slot):
                c.start()

        for r, c in enumerate(fetch(i, slot)):
            c.wait()
            h_scr[r * rows:(r + 1) * rows, :] = _rms(xbuf[slot, r], g_ref[...]).astype(BF16)

    o_ref[...] = (_dot(h_scr[...], w_ref[...]) * cs_ref[...]).astype(o_ref.dtype)


def _norm_matmul_res(x, g, w, col_scale, n_res_tile=8, tn=2560):
    bsz, seq, d = x.shape
    n = w.shape[1]
    rows = seq // N_RES
    tm = n_res_tile * rows
    halves = N_RES // n_res_tile
    return pl.pallas_call(
        _norm_matmul_res_kernel,
        grid=(bsz * halves, n // tn),
        in_specs=[
            pl.BlockSpec(memory_space=pl.ANY),
            pl.BlockSpec((1, d), lambda i, j: (0, 0)),
            pl.BlockSpec((d, tn), lambda i, j: (0, j)),
            pl.BlockSpec((1, tn), lambda i, j: (0, j)),
        ],
        out_specs=pl.BlockSpec((tm, tn), lambda i, j: (i, j)),
        out_shape=jax.ShapeDtypeStruct((bsz * seq, n), BF16),
        scratch_shapes=[pltpu.VMEM((2, n_res_tile, rows, d), F32), pltpu.VMEM((tm, d), BF16),
                        pltpu.SemaphoreType.DMA((2, n_res_tile))],
        compiler_params=_cparams(("arbitrary", "arbitrary")),
        name="norm_matmul_res",
    )(x.reshape(bsz, rows, N_RES, d), g.reshape(1, d), w, col_scale.reshape(1, n))


def _matmul_res_kernel(a_ref, w_ref, r_ref, o_ref):
    o_ref[...] = r_ref[...] + _dot(a_ref[...], w_ref[...])


def _matmul_res(a, w, res, tm=512, tn=2048):
    m, k = a.shape
    n = w.shape[1]
    return pl.pallas_call(
        _matmul_res_kernel,
        grid=(m // tm, n // tn),
        in_specs=[
            pl.BlockSpec((tm, k), lambda i, j: (i, 0)),
            pl.BlockSpec((k, tn), lambda i, j: (0, j)),
            pl.BlockSpec((tm, tn), lambda i, j: (i, j)),
        ],
        out_specs=pl.BlockSpec((tm, tn), lambda i, j: (i, j)),
        out_shape=jax.ShapeDtypeStruct((m, n), F32),
        compiler_params=_cparams(("parallel", "arbitrary")),
        name="matmul_res",
    )(a, w, res)


def _matmul_unres_kernel(a_ref, w_ref, r_hbm, o_hbm, rbuf, obuf, rsem, osem):
    i = pl.program_id(0)
    n_steps = pl.num_programs(0)
    n_res_tile, rows, _ = rbuf.shape[1:]
    parts = N_RES // n_res_tile

    def res_fetch(step, slot):
        b, r0 = step // parts, (step % parts) * n_res_tile
        return [pltpu.make_async_copy(r_hbm.at[b, :, r0 + r, :], rbuf.at[slot, r], rsem.at[slot, r])
                for r in range(n_res_tile)]

    def out_store(step, slot):
        b, r0 = step // parts, (step % parts) * n_res_tile
        return [pltpu.make_async_copy(obuf.at[slot, r], o_hbm.at[b, :, r0 + r, :], osem.at[slot, r])
                for r in range(n_res_tile)]

    slot = i % 2

    @pl.when(i == 0)
    def _():
        for c in res_fetch(i, slot):
            c.start()

    @pl.when(i + 1 < n_steps)
    def _():
        for c in res_fetch(i + 1, 1 - slot):
            c.start()

    y = _dot(a_ref[...], w_ref[...])

    @pl.when(i >= 2)
    def _():
        for c in out_store(i - 2, slot):
            c.wait()

    for r, c in enumerate(res_fetch(i, slot)):
        c.wait()
        obuf[slot, r] = rbuf[slot, r] + y[r * rows:(r + 1) * rows, :]
    for c in out_store(i, slot):
        c.start()

    @pl.when(i == n_steps - 1)
    def _():
        for c in out_store(i - 1, 1 - slot) + out_store(i, slot):
            c.wait()


def _matmul_unres(a, w, res, n_res_tile=4):
    bsz, seq, d = res.shape
    k = a.shape[1]
    rows = seq // N_RES
    tm = n_res_tile * rows
    parts = N_RES // n_res_tile
    assert bsz * parts >= 2
    buf = pltpu.VMEM((2, n_res_tile, rows, d), F32)
    out = pl.pallas_call(
        _matmul_unres_kernel,
        grid=(bsz * parts,),
        in_specs=[
            pl.BlockSpec((tm, k), lambda i: (i, 0)),
            pl.BlockSpec((k, d), lambda i: (0, 0)),
            pl.BlockSpec(memory_space=pl.ANY),
        ],
        out_specs=pl.BlockSpec(memory_space=pl.ANY),
        out_shape=jax.ShapeDtypeStruct((bsz, rows, N_RES, d), F32),
        scratch_shapes=[buf, buf, pltpu.SemaphoreType.DMA((2, n_res_tile)),
                        pltpu.SemaphoreType.DMA((2, n_res_tile))],
        compiler_params=_cparams(("arbitrary",)),
        name="matmul_unres",
    )(a, w, res.reshape(bsz, rows, N_RES, d))
    return out.reshape(bsz * seq, d)


def _swiglu_kernel(x_ref, g_ref, wg_ref, wu_ref, wd_ref, o_ref, h_scr):
    @pl.when(pl.program_id(1) == 0)
    def _():
        h_scr[...] = _rms(x_ref[...], g_ref[...]).astype(BF16)
        o_ref[...] = x_ref[...]

    h = h_scr[...]
    a = _dot(h, wg_ref[...])
    b = _dot(h, wu_ref[...])
    hid = (_silu(a) * b).astype(BF16)
    o_ref[...] += _dot(hid, wd_ref[...])


def _swiglu_res(x, g, wg, wu, wd, tm=1024, tf=512):
    m, d = x.shape
    ff = wg.shape[1]
    return pl.pallas_call(
        _swiglu_kernel,
        grid=(m // tm, ff // tf),
        in_specs=[
            pl.BlockSpec((tm, d), lambda i, j: (i, 0)),
            pl.BlockSpec((1, d), lambda i, j: (0, 0)),
            pl.BlockSpec((d, tf), lambda i, j: (0, j)),
            pl.BlockSpec((d, tf), lambda i, j: (0, j)),
            pl.BlockSpec((tf, d), lambda i, j: (j, 0)),
        ],
        out_specs=pl.BlockSpec((tm, d), lambda i, j: (i, 0)),
        out_shape=jax.ShapeDtypeStruct((m, d), F32),
        scratch_shapes=[pltpu.VMEM((tm, d), BF16)],
        compiler_params=_cparams(("parallel", "arbitrary"), vmem=BIG_VMEM_LIMIT),
        name="swiglu_res",
    )(x, g.reshape(1, d), wg, wu, wd)


def _attn_bias_table():
    blk = HEAD_DIM
    iq = np.arange(blk)[:, None]
    jk = np.arange(2 * blk)[None, :]

    def offs(idx, n_chunk, size):
        return n_chunk * (idx % size) + idx // size

    masks = []
    for n_chunk in (1, 4, 16):
        size = blk // n_chunk
        lq = offs(iq, n_chunk, size)
        masks.append((offs(jk, n_chunk, size) <= lq) & (jk < blk))
        dist = lq - (offs(jk, n_chunk, 2 * size) - blk)
        masks.append((dist >= 0) & (dist <= blk))
    return np.where(np.stack(masks), 0.0, NEG).astype(np.float32)


def _attn_pool_kernel(q_ref, k_ref, v_ref, bias, pw_ref, ps_ref, o_ref,
                      qf, kf, vf, ms, ls, accs):
    h = pl.program_id(1)
    per = q_ref.shape[1] // HEAD_DIM
    lanes = [pl.ds(u * HEAD_DIM, HEAD_DIM) for u in range(per)]

    @pl.when(h < A_HEADS // per)
    def _():
        for ln in lanes:
            _attend(q_ref.at[:, ln], k_ref.at[:, ln], v_ref.at[:, ln], bias, o_ref.at[:, ln],
                    qf, kf, vf, ms, ls, accs)

    for pp in range(N_POOL // per):
        @pl.when(h == A_HEADS // per + pp)
        def _(pp=pp):
            for u, ln in enumerate(lanes):
                gi = pp * per + u
                _pool(q_ref.at[:, ln], pw_ref.at[gi], ps_ref.at[:, ln], o_ref.at[:, ln],
                      POOL_WINDOWS[gi])


def _attend(q_ref, k_ref, v_ref, bias, o_ref, qf, kf, vf, ms, ls, accs):
    seq = q_ref.shape[0]
    blk = HEAD_DIM
    bf16_rows = 16

    qf[...] = q_ref[...].astype(F32)
    kf[...] = k_ref[...].astype(F32)
    vf[...] = v_ref[...].astype(F32)

    def gather(ref, ref_f32, starts, size):
        if all(v % bf16_rows == 0 for v in [size] + starts):
            return jnp.concatenate([ref[s0:s0 + size, :] for s0 in starts], axis=0)
        return jnp.concatenate([ref_f32[s0:s0 + size, :] for s0 in starts], axis=0).astype(BF16)

    def scatter(br, starts, size, m, l, acc):
        mb = jnp.broadcast_to(m, acc.shape)
        for c, s0 in enumerate(starts):
            src = slice(c * size, (c + 1) * size)
            ms[br, s0:s0 + size, :] = mb[src]
            ls[br, s0:s0 + size, :] = l[src]
            accs[br, s0:s0 + size, :] = acc[src]

    kinds = {}
    for br, n_chunk in enumerate((1, 4, 16)):
        size = blk // n_chunk
        for grp in range(N_RES // n_chunk):
            bases = [(grp + (N_RES // n_chunk) * c) * blk for c in range(n_chunk)]
            for n in range(n_chunk):
                qs = [b0 + n * size for b0 in bases]
                if n == 0:
                    kinds.setdefault((2 * br, blk), []).append((br, size, qs, qs, size))
                else:
                    ks = [s0 - size for s0 in qs]
                    kinds.setdefault((2 * br + 1, 2 * blk), []).append(
                        (br, size, qs, ks, 2 * size))

    batch = 16
    for (bias_idx, kw), blocks in kinds.items():
        bi = bias[bias_idx][:, :kw]
        for g0 in range(0, len(blocks), batch):
            group = blocks[g0:g0 + batch]
            s = jnp.concatenate(
                [_dot_nt(gather(q_ref, qf, qs, size), gather(k_ref, kf, ks, ksize))
                 for _, size, qs, ks, ksize in group], axis=0)
            s = (s.reshape(len(group), blk, kw) + bi[None]).reshape(len(group) * blk, kw)
            m = jnp.max(s, axis=-1, keepdims=True)
            pb = jnp.exp2(s - m).astype(BF16)
            ones = jnp.ones((kw, blk), BF16)
            for g, (br, size, qs, ks, ksize) in enumerate(group):
                rows = slice(g * blk, (g + 1) * blk)
                v_one = jnp.concatenate([gather(v_ref, vf, ks, ksize), ones], axis=1)
                acc = _dot(pb[rows], v_one)
                scatter(br, qs, size, m[rows], acc[:, blk:], acc[:, :blk])

    slab = 2 * blk
    for s0 in range(0, seq, slab):
        rows = slice(s0, s0 + slab)
        m_all = jnp.maximum(jnp.maximum(ms[0, rows, :], ms[1, rows, :]), ms[2, rows, :])
        num = jnp.zeros((slab, blk), F32)
        den = jnp.zeros((slab, blk), F32)
        for br in range(3):
            w = jnp.exp2(ms[br, rows, :] - m_all)
            num = num + accs[br, rows, :] * w
            den = den + ls[br, rows, :] * w
        o_ref[rows, :] = (num / den).astype(o_ref.dtype)


def _pool(p_ref, pw_ref, ps_ref, o_ref, w):
    blk = HEAD_DIM
    pf = p_ref[...].astype(F32)
    row = lax.broadcasted_iota(jnp.int32, pf.shape, 0)
    l_idx = row % blk
    pos = N_RES * l_idx + row // blk
    s = pf
    k = 1
    while k < w:
        y = pltpu.roll(s, k * blk, axis=0)
        y1 = jnp.where(l_idx == 0, 0.0, pltpu.roll(y, 1, axis=0))
        s = s + jnp.where(row < k * blk, y1, y)
        k *= 2
    cnt = jnp.minimum(pos + 1, w).astype(F32)
    pooled = s / cnt - pf
    y = _dot(pooled.astype(BF16), pw_ref[...].astype(BF16)) * ps_ref[...]
    o_ref[...] = y.astype(o_ref.dtype)


def _attn_pool(proj, pool_w, pool_scale, per=2):
    bsz, seq, _ = proj.shape
    na = A_HEADS // per
    n_steps = na + N_POOL // per
    wide = per * HEAD_DIM
    qmap = lambda b, h: (b, 0, jnp.where(h < na, h, h + 2 * na))
    kmap = lambda b, h: (b, 0, na + jnp.minimum(h, na - 1))
    vmap = lambda b, h: (b, 0, 2 * na + jnp.minimum(h, na - 1))
    f32s = pltpu.VMEM((seq, HEAD_DIM), F32)
    br3 = pltpu.VMEM((3, seq, HEAD_DIM), F32)
    bias = jnp.asarray(_attn_bias_table())
    return pl.pallas_call(
        _attn_pool_kernel,
        grid=(bsz, n_steps),
        in_specs=[
            pl.BlockSpec((None, seq, wide), qmap),
            pl.BlockSpec((None, seq, wide), kmap),
            pl.BlockSpec((None, seq, wide), vmap),
            pl.BlockSpec(bias.shape, lambda b, h: (0, 0, 0)),
            pl.BlockSpec((N_POOL, HEAD_DIM, HEAD_DIM), lambda b, h: (0, 0, 0)),
            pl.BlockSpec((1, wide), lambda b, h: (0, jnp.maximum(h - na, 0))),
        ],
        out_specs=pl.BlockSpec((None, seq, wide), lambda b, h: (b, 0, h)),
        out_shape=jax.ShapeDtypeStruct((bsz, seq, n_steps * wide), BF16),
        scratch_shapes=[f32s, f32s, f32s, br3, br3, br3],
        compiler_params=_cparams(("parallel", "arbitrary")),
        name="attn_pool",
    )(proj, proj, proj, bias, pool_w, pool_scale.reshape(1, N_POOL * HEAD_DIM))


def _hgrn_kernel(q_ref, f_ref, i_ref, gt_ref, lb_ref, on_ref, o_ref, *scratch):
    for u in range(q_ref.shape[1] // HEAD_DIM):
        ln = pl.ds(u * HEAD_DIM, HEAD_DIM)
        _hgrn_head(*(r.at[:, ln] for r in (q_ref, f_ref, i_ref, gt_ref, lb_ref, on_ref, o_ref)),
                   *scratch)


def _hgrn_head(q_ref, f_ref, i_ref, gt_ref, lb_ref, on_ref, o_ref,
               qe_scr, o_scr, u_scr, d_scr, s_scr):
    seq = q_ref.shape[0]
    ck = HGRN_CHUNK
    grp = 4 * ck
    n_grp = seq // grp
    n_chunk = seq // ck
    lb = lb_ref[...]

    ri = lax.broadcasted_iota(jnp.int32, (grp, grp), 0)
    ci = lax.broadcasted_iota(jnp.int32, (grp, grp), 1)
    tri_mask = (ri // ck == ci // ck) & (ci <= ri)
    tri = tri_mask.astype(BF16)

    groups = [slice(gi * grp, (gi + 1) * grp) for gi in range(n_grp)]
    q = _half_silu(q_ref[...].astype(F32))
    f = (0.5 + 0.5 * lb) + (0.5 - 0.5 * lb) * jnp.tanh(f_ref[...].astype(F32))
    k = 1.0 - f
    g = jnp.log2(f)
    g_hi = g.astype(BF16)
    g_lo = (g - g_hi.astype(F32)).astype(BF16)
    b = jnp.concatenate([_dot(tri, g_hi[r]) + _dot(tri, g_lo[r]) for r in groups], axis=0)
    b3 = b.reshape(n_chunk, ck, HEAD_DIM)
    b_mid = b3[:, ck // 2 - 1:ck // 2, :]
    b_last = b3[:, ck - 1:ck, :]
    q3 = q.reshape(n_chunk, ck, HEAD_DIM)
    k3 = k.reshape(n_chunk, ck, HEAD_DIM)
    q_mid = q3 * jnp.exp2(b3 - b_mid)
    k_mid = k3 * jnp.exp2(b_mid - b3)
    qs = q_mid.reshape(seq, HEAD_DIM).astype(BF16)
    ks = k_mid.reshape(seq, HEAD_DIM).astype(BF16)
    qe_scr[...] = (q_mid * jnp.exp2(b_mid)).reshape(seq, HEAD_DIM).astype(BF16)
    kd = (k_mid * jnp.exp2(b_last - b_mid)).reshape(seq, HEAD_DIM).astype(BF16)
    d_scr[...] = jnp.exp2(b_last).reshape(n_chunk, HEAD_DIM)
    for r in groups:
        att = jnp.where(tri_mask, _dot_nt(qs[r], ks[r]), 0.0)
        o_scr[r, :] = _dot(att.astype(BF16), i_ref[r, :])
    for c in range(n_chunk):
        cr = slice(c * ck, (c + 1) * ck)
        u_scr[c] = _dot_tn(i_ref[cr, :], kd[cr])

    st = jnp.zeros((HEAD_DIM, HEAD_DIM), F32)
    for c in range(n_chunk):
        s_scr[c] = st.astype(BF16)
        st = st * d_scr[c:c + 1, :] + u_scr[c]

    for gi in range(n_grp):
        rows = slice(gi * grp, (gi + 1) * grp)
        inter = jnp.concatenate(
            [_dot_nt(qe_scr[gi * grp + c * ck:gi * grp + (c + 1) * ck, :], s_scr[gi * 4 + c])
             for c in range(4)], axis=0)
        o = _rms(o_scr[rows, :] + inter, on_ref[...])
        o_ref[rows, :] = (o * _half_silu(gt_ref[rows, :].astype(F32))).astype(o_ref.dtype)


def _hgrn(proj, lb, out_norm, per=2):
    bsz, seq, _ = proj.shape
    nh = HGRN_HEADS // per
    wide = per * HEAD_DIM
    n_chunk = seq // HGRN_CHUNK
    spec = lambda off: pl.BlockSpec((None, seq, wide), lambda b, h: (b, 0, off + h))
    vec = pl.BlockSpec((1, wide), lambda b, h: (0, h))
    return pl.pallas_call(
        _hgrn_kernel,
        grid=(bsz, nh),
        in_specs=[spec(0), spec(nh), spec(2 * nh), spec(3 * nh), vec, vec],
        out_specs=pl.BlockSpec((None, seq, wide), lambda b, h: (b, 0, h)),
        out_shape=jax.ShapeDtypeStruct((bsz, seq, nh * wide), BF16),
        scratch_shapes=[
            pltpu.VMEM((seq, HEAD_DIM), BF16),
            pltpu.VMEM((seq, HEAD_DIM), F32),
            pltpu.VMEM((n_chunk, HEAD_DIM, HEAD_DIM), F32),
            pltpu.VMEM((n_chunk, HEAD_DIM), F32),
            pltpu.VMEM((n_chunk, HEAD_DIM, HEAD_DIM), BF16),
        ],
        compiler_params=_cparams(("parallel", "arbitrary")),
        name="hgrn2",
    )(proj, proj, proj, proj, lb.reshape(1, -1), out_norm.reshape(1, -1))


def _router_kernel(x_ref, g_ref, w_ref, h_ref, meta_ref, cnt_ref, carry):
    i = pl.program_id(0)
    tm = x_ref.shape[0]

    @pl.when(i == 0)
    def _():
        carry[...] = jnp.zeros_like(carry)

    h = _rms(x_ref[...], g_ref[...])
    h_ref[...] = _pack_bf16_pair(h)
    h_hi = h.astype(BF16)
    h_lo = (h - h_hi.astype(F32)).astype(BF16)
    w = w_ref[...]
    w_hi = w.astype(BF16)
    w_lo = (w - w_hi.astype(F32)).astype(BF16)
    prod = _dot(jnp.concatenate([h_hi, h_lo], axis=0), jnp.concatenate([w_hi, w_lo], axis=1))
    logits = (prod[:tm, :LANES] + prod[:tm, LANES:]) + (prod[tm:, :LANES] + prod[tm:, LANES:])
    lane = lax.broadcasted_iota(jnp.int32, logits.shape, 1)
    logits = jnp.where(lane < N_EXPERTS, logits, NEG)
    m1 = jnp.max(logits, axis=-1, keepdims=True)
    i1 = jnp.min(jnp.where(logits == m1, lane, LANES), axis=-1, keepdims=True)
    rest = jnp.where(lane == i1, NEG, logits)
    m2 = jnp.max(rest, axis=-1, keepdims=True)
    i2 = jnp.min(jnp.where(rest == m2, lane, LANES), axis=-1, keepdims=True)
    e = jnp.exp(m2 - m1)
    g1 = 1.0 / (1.0 + e)
    g2 = e / (1.0 + e)
    oh1 = (lane == i1).astype(F32)
    oh2 = (lane == i2).astype(F32)
    cnt = oh1 + oh2
    ri = lax.broadcasted_iota(jnp.int32, (tm, tm), 0)
    ci = lax.broadcasted_iota(jnp.int32, (tm, tm), 1)
    before = (ci < ri).astype(BF16)
    prior = _dot(before, cnt.astype(BF16)) + carry[...]
    r1 = jnp.sum(oh1 * prior, axis=-1, keepdims=True)
    r2 = jnp.sum(oh2 * prior, axis=-1, keepdims=True)
    carry[...] += jnp.sum(cnt, axis=0, keepdims=True)
    cnt_ref[...] = carry[...]
    meta = jnp.where(lane == 0, i1.astype(F32), 0.0)
    meta = jnp.where(lane == 1, i2.astype(F32), meta)
    meta = jnp.where(lane == 2, r1, meta)
    meta = jnp.where(lane == 3, r2, meta)
    meta = jnp.where(lane == 4, g1, meta)
    meta = jnp.where(lane == 5, g2, meta)
    meta_ref[...] = meta


def _router(x, g, router_w, tm=512):
    t, d = x.shape
    w_pad = jnp.pad(router_w, ((0, 0), (0, LANES - router_w.shape[1])))
    return pl.pallas_call(
        _router_kernel,
        grid=(t // tm,),
        in_specs=[
            pl.BlockSpec((tm, d), lambda i: (i, 0)),
            pl.BlockSpec((1, d), lambda i: (0, 0)),
            pl.BlockSpec((d, LANES), lambda i: (0, 0)),
        ],
        out_specs=[
            pl.BlockSpec((tm, d // 2), lambda i: (i, 0)),
            pl.BlockSpec((tm, LANES), lambda i: (i, 0)),
            pl.BlockSpec((1, LANES), lambda i: (0, 0)),
        ],
        out_shape=[
            jax.ShapeDtypeStruct((t, d // 2), jnp.uint32),
            jax.ShapeDtypeStruct((t, LANES), F32),
            jax.ShapeDtypeStruct((1, LANES), F32),
        ],
        scratch_shapes=[pltpu.VMEM((1, LANES), F32)],
        compiler_params=_cparams(("arbitrary",)),
        name="moe_router",
    )(x, g.reshape(1, d), w_pad)


def _row_copy(src, dst, s, d, sem):
    return pltpu.make_async_copy(src.at[pl.ds(s, 1)], dst.at[pl.ds(d, 1)], sem)


def _dispatch_kernel(dest_ref, h_ref, o_hbm, sem):
    tm = h_ref.shape[0]

    def start(t, carry):
        _row_copy(h_ref, o_hbm, t, dest_ref[0, 0, 2 * t], sem).start(priority=0)
        _row_copy(h_ref, o_hbm, t, dest_ref[0, 0, 2 * t + 1], sem).start(priority=1)
        return carry

    lax.fori_loop(0, tm, start, 0, unroll=8)

    def wait(t, carry):
        _row_copy(h_ref, o_hbm, 0, 0, sem).wait()
        _row_copy(h_ref, o_hbm, 0, 0, sem).wait()
        return carry

    lax.fori_loop(0, tm, wait, 0, unroll=8)


def _dispatch(h, dest, n_rows, tm=1024):
    t, d = h.shape
    dest3 = dest.reshape(t // tm, 1, 2 * tm)
    return pl.pallas_call(
        _dispatch_kernel,
        grid=(t // tm,),
        in_specs=[
            pl.BlockSpec((1, 1, 2 * tm), lambda i: (i, 0, 0), memory_space=pltpu.SMEM),
            pl.BlockSpec((tm, d), lambda i: (i, 0)),
        ],
        out_specs=pl.BlockSpec(memory_space=pl.ANY),
        out_shape=jax.ShapeDtypeStruct((n_rows, d), h.dtype),
        scratch_shapes=[pltpu.SemaphoreType.DMA(())],
        compiler_params=_cparams(("arbitrary",)),
        name="moe_dispatch",
    )(dest3, h)


def _moe_kernel(be_ref, bb_ref, nv_ref, na_ref, x_ref, wg_ref, wu_ref, wd_ref, o_ref, xb_scr):
    i = pl.program_id(0)
    j = pl.program_id(1)
    n_valid = nv_ref[i]

    @pl.when(i < na_ref[0])
    def _():
        @pl.when(j == 0)
        def _():
            row = lax.broadcasted_iota(jnp.int32, x_ref.shape, 0)
            lo, hi = _unpack_bf16_pair(jnp.where(row < n_valid, x_ref[...], jnp.uint32(0)))
            xb_scr[:, :lo.shape[1]] = lo
            xb_scr[:, lo.shape[1]:] = hi
            o_ref[...] = jnp.zeros_like(o_ref)

        def run(rows):
            xb = xb_scr[rows, :]
            a = _dot(xb, wg_ref[0].astype(BF16))
            b = _dot(xb, wu_ref[0].astype(BF16))
            hid = (_silu(a) * b).astype(BF16)
            o_ref[rows, :] += _dot(hid, wd_ref[0].astype(BF16))

        part = x_ref.shape[0] // MOE_PARTS
        for nq in range(1, MOE_PARTS + 1):
            @pl.when((n_valid > (nq - 1) * part) & (n_valid <= nq * part))
            def _(nq=nq):
                run(slice(0, nq * part))


def _moe_experts(x_rows, wg, wu, wd, blk_e, blk_idx, blk_valid, n_live, n_blocks):
    dp = x_rows.shape[1]
    d = 2 * dp
    dff = wg.shape[2]
    n_ff = dff // MOE_TF
    r = MOE_ROWS

    def ffi(i, j, na):
        return jnp.where(i < na[0], j, n_ff - 1)

    grid_spec = pltpu.PrefetchScalarGridSpec(
        num_scalar_prefetch=4,
        grid=(n_blocks, n_ff),
        in_specs=[
            pl.BlockSpec((r, dp), lambda i, j, be, bb, nv, na: (bb[i], 0)),
            pl.BlockSpec((1, d, MOE_TF), lambda i, j, be, bb, nv, na: (be[i], 0, ffi(i, j, na))),
            pl.BlockSpec((1, d, MOE_TF), lambda i, j, be, bb, nv, na: (be[i], 0, ffi(i, j, na))),
            pl.BlockSpec((1, MOE_TF, d), lambda i, j, be, bb, nv, na: (be[i], ffi(i, j, na), 0)),
        ],
        out_specs=pl.BlockSpec((r, d), lambda i, j, be, bb, nv, na: (bb[i], 0)),
        scratch_shapes=[pltpu.VMEM((r, d), BF16)],
    )
    return pl.pallas_call(
        _moe_kernel,
        grid_spec=grid_spec,
        out_shape=jax.ShapeDtypeStruct((x_rows.shape[0], d), F32),
        compiler_params=_cparams(("arbitrary", "arbitrary"), vmem=BIG_VMEM_LIMIT),
        name="moe_experts",
    )(blk_e, blk_idx, blk_valid, n_live, x_rows, wg, wu, wd)


def _combine_kernel(dest_ref, next_ref, x_ref, meta_ref, nf_ref, y_hbm, o_ref, buf, sem):
    i = pl.program_id(0)
    tm = x_ref.shape[0]
    slot = i % 2

    def gather(idx_ref, s):
        def start(t, carry):
            _row_copy(y_hbm, buf.at[s, 0], idx_ref[0, 0, 2 * t], t, sem.at[s]).start(priority=0)
            _row_copy(y_hbm, buf.at[s, 1], idx_ref[0, 0, 2 * t + 1], t, sem.at[s]).start(priority=1)
            return carry

        lax.fori_loop(0, tm, start, 0, unroll=8)

    @pl.when(i == 0)
    def _():
        gather(dest_ref, slot)

    @pl.when(i + 1 < pl.num_programs(0))
    def _():
        gather(next_ref, 1 - slot)

    def wait(t, carry):
        _row_copy(y_hbm, buf.at[slot, 0], 0, 0, sem.at[slot]).wait()
        _row_copy(y_hbm, buf.at[slot, 1], 0, 0, sem.at[slot]).wait()
        return carry

    lax.fori_loop(0, tm, wait, 0, unroll=8)

    meta = meta_ref[...]
    g1 = meta[:, 4:5]
    g2 = meta[:, 5:6]
    x = x_ref[...] + (g1 * buf[slot, 0] + g2 * buf[slot, 1])
    o_ref[...] = _rms(x, nf_ref[...])


def _combine(x, meta, norm_final, y_rows, dest, tm=512):
    t, d = x.shape
    n_steps = t // tm
    dest3 = dest.reshape(n_steps, 1, 2 * tm)
    return pl.pallas_call(
        _combine_kernel,
        grid=(n_steps,),
        in_specs=[
            pl.BlockSpec((1, 1, 2 * tm), lambda i: (i, 0, 0), memory_space=pltpu.SMEM),
            pl.BlockSpec((1, 1, 2 * tm), lambda i: (jnp.minimum(i + 1, n_steps - 1), 0, 0),
                         memory_space=pltpu.SMEM),
            pl.BlockSpec((tm, d), lambda i: (i, 0)),
            pl.BlockSpec((tm, LANES), lambda i: (i, 0)),
            pl.BlockSpec((1, d), lambda i: (0, 0)),
            pl.BlockSpec(memory_space=pl.ANY),
        ],
        out_specs=pl.BlockSpec((tm, d), lambda i: (i, 0)),
        out_shape=jax.ShapeDtypeStruct((t, d), F32),
        scratch_shapes=[pltpu.VMEM((2, 2, tm, d), F32), pltpu.SemaphoreType.DMA((2,))],
        compiler_params=_cparams(("arbitrary",)),
        name="moe_combine",
    )(dest3, dest3, x, meta, norm_final.reshape(1, d), y_rows)


def _moe_block_table(counts, t):
    r = MOE_ROWS
    n_blocks = (2 * t) // r + N_EXPERTS
    nblk = (counts + r - 1) // r
    ends = jnp.cumsum(nblk)
    starts = ends - nblk
    n_live = ends[-1]
    step = jnp.minimum(jnp.arange(n_blocks, dtype=jnp.int32), n_live - 1)
    blk_e = jnp.sum(step[:, None] >= ends[None, :], axis=1).astype(jnp.int32)
    local = step - starts[blk_e]
    blk_valid = jnp.clip(counts[blk_e] - local * r, 0, r)
    return (starts * r, blk_e, step, blk_valid.astype(jnp.int32),
            n_live.reshape(1).astype(jnp.int32), n_blocks)


def kernel(x, norm_mix, norm_ffn, w_in_even, pool_w, pool_scale, w_out_even, ffn_w_gate,
           ffn_w_up, ffn_w_down, w_in_odd, lower_bound_logits, hgrn_out_norm, w_out_odd,
           router_w, moe_w_gate, moe_w_up, moe_w_down, norm_final):
    bsz, seq, d = x.shape
    t = bsz * seq

    lb_all = jnp.cumsum(jax.nn.softmax(lower_bound_logits.astype(F32), axis=0), axis=0)
    lb_all = lb_all - lb_all[:1]

    n_in = w_in_even.shape[2]
    q_scale = math.log2(math.e) / math.sqrt(HEAD_DIM)
    col_scale = jnp.where(jnp.arange(n_in) < A_WIDTH, q_scale, 1.0).astype(F32)
    proj = _norm_matmul_res(x, norm_mix[0], w_in_even[0].astype(BF16), col_scale)
    mix = _attn_pool(proj.reshape(bsz, seq, -1), pool_w[0], pool_scale[0])
    xt = _matmul_unres(mix.reshape(t, -1), w_out_even[0].astype(BF16), x)
    xt = _swiglu_res(xt, norm_ffn[0], ffn_w_gate[0].astype(BF16), ffn_w_up[0].astype(BF16),
                     ffn_w_down[0].astype(BF16))

    wid = HGRN_HEADS * HEAD_DIM
    col = jnp.arange(4 * wid) // wid
    proj = _norm_matmul(xt, norm_mix[1], w_in_odd[0].astype(BF16),
                        jnp.where(col == 2, 1.0, 0.5).astype(F32))
    o = _hgrn(proj.reshape(bsz, seq, -1), lb_all[1], hgrn_out_norm[0])
    xt = _matmul_res(o.reshape(t, -1), w_out_odd[0].astype(BF16), xt)

    h, meta, cnt = _router(xt, norm_ffn[1], router_w[0])
    counts = cnt[0, :N_EXPERTS].astype(jnp.int32)
    row0, blk_e, blk_idx, blk_valid, n_live, n_blocks = _moe_block_table(counts, t)
    dest = (row0[meta[:, 0:2].astype(jnp.int32)] + meta[:, 2:4].astype(jnp.int32)).reshape(-1)
    x_rows = _dispatch(h, dest, n_blocks * MOE_ROWS)
    y_rows = _moe_experts(x_rows, moe_w_gate[0], moe_w_up[0], moe_w_down[0],
                          blk_e, blk_idx, blk_valid, n_live, n_blocks)
    out = _combine(xt, meta, norm_final, y_rows, dest)
    return out.reshape(bsz, seq, d)
```

```python
import math

import jax
import jax.numpy as jnp
import numpy as np
from jax import lax
from jax.experimental import pallas as pl
from jax.experimental.pallas import tpu as pltpu

F32 = jnp.float32
BF16 = jnp.bfloat16

HEAD_DIM = 128
A_HEADS = 12
A_WIDTH = A_HEADS * HEAD_DIM
N_POOL = 4
POOL_WINDOWS = (2, 4, 8, 16)
HGRN_HEADS = 16
HGRN_CHUNK = 64
N_EXPERTS = 8
EPS = 1e-6
NEG = -1e30

LANES = 128
VMEM_LIMIT = 56 * 1024 * 1024

MOE_ROWS = 1024
MOE_PARTS = 4
MOE_TF = 512
BIG_VMEM_LIMIT = 62 * 1024 * 1024
HI16 = 0xFFFF0000


def _pack_bf16_pair(x):
    n = x.shape[1] // 2
    bits = lax.bitcast_convert_type(x.astype(BF16).astype(F32), jnp.uint32)
    return (bits[:, n:] & jnp.uint32(HI16)) | (bits[:, :n] >> 16)


def _unpack_bf16_pair(p):
    lo = lax.bitcast_convert_type(p << 16, F32).astype(BF16)
    hi = lax.bitcast_convert_type(p & jnp.uint32(HI16), F32).astype(BF16)
    return lo, hi

N_RES = 16


def _cparams(sem, vmem=VMEM_LIMIT, flags=None):
    return pltpu.CompilerParams(dimension_semantics=sem, vmem_limit_bytes=vmem, flags=flags)


def _rms(x, g):
    ms = jnp.mean(x * x, axis=-1, keepdims=True)
    return x * lax.rsqrt(ms + EPS) * g


def _half_silu(h):
    return h + h * jnp.tanh(h)


def _silu(x):
    return _half_silu(0.5 * x)


def _dot(a, b):
    return jnp.dot(a, b, preferred_element_type=F32)


def _dot_nt(a, b):
    return lax.dot_general(a, b, (((1,), (1,)), ((), ())), preferred_element_type=F32)


def _dot_tn(a, b):
    return lax.dot_general(a, b, (((0,), (0,)), ((), ())), preferred_element_type=F32)


def _norm_matmul_kernel(x_ref, g_ref, w_ref, cs_ref, o_ref, h_scr):
    @pl.when(pl.program_id(1) == 0)
    def _():
        h_scr[...] = _rms(x_ref[...], g_ref[...]).astype(BF16)

    o_ref[...] = (_dot(h_scr[...], w_ref[...]) * cs_ref[...]).astype(o_ref.dtype)


def _norm_matmul(x, g, w, col_scale, tm=1024, tn=2048):
    m, d = x.shape
    n = w.shape[1]
    return pl.pallas_call(
        _norm_matmul_kernel,
        grid=(m // tm, n // tn),
        in_specs=[
            pl.BlockSpec((tm, d), lambda i, j: (i, 0)),
            pl.BlockSpec((1, d), lambda i, j: (0, 0)),
            pl.BlockSpec((d, tn), lambda i, j: (0, j)),
            pl.BlockSpec((1, tn), lambda i, j: (0, j)),
        ],
        out_specs=pl.BlockSpec((tm, tn), lambda i, j: (i, j)),
        out_shape=jax.ShapeDtypeStruct((m, n), BF16),
        scratch_shapes=[pltpu.VMEM((tm, d), BF16)],
        compiler_params=_cparams(("parallel", "arbitrary")),
        name="norm_matmul",
    )(x, g.reshape(1, d), w, col_scale.reshape(1, n))


def _norm_matmul_res_kernel(x_hbm, g_ref, w_ref, cs_ref, o_ref, xbuf, h_scr, sem):
    i = pl.program_id(0)
    n_res_tile, rows, _ = xbuf.shape[1:]
    halves = N_RES // n_res_tile

    def fetch(step, slot):
        b = step // halves
        r0 = (step % halves) * n_res_tile
        return [pltpu.make_async_copy(x_hbm.at[b, :, r0 + r, :], xbuf.at[slot, r], sem.at[slot, r])
                for r in range(n_res_tile)]

    @pl.when(pl.program_id(1) == 0)
    def _():
        slot = i % 2

        @pl.when(i == 0)
        def _():
            for c in fetch(i, slot):
                c.start()

        @pl.when(i + 1 < pl.num_programs(0))
        def _():
            for c in fetch(i + 1, 1 - slot):
                c.start()

        for r, c in enumerate(fetch(i, slot)):
            c.wait()
            h_scr[r * rows:(r + 1) * rows, :] = _rms(xbuf[slot, r], g_ref[...]).astype(BF16)

    o_ref[...] = (_dot(h_scr[...], w_ref[...]) * cs_ref[...]).astype(o_ref.dtype)


def _norm_matmul_res(x, g, w, col_scale, n_res_tile=8, tn=2560):
    bsz, seq, d = x.shape
    n = w.shape[1]
    rows = seq // N_RES
    tm = n_res_tile * rows
    halves = N_RES // n_res_tile
    return pl.pallas_call(
        _norm_matmul_res_kernel,
        grid=(bsz * halves, n // tn),
        in_specs=[
            pl.BlockSpec(memory_space=pl.ANY),
            pl.BlockSpec((1, d), lambda i, j: (0, 0)),
            pl.BlockSpec((d, tn), lambda i, j: (0, j)),
            pl.BlockSpec((1, tn), lambda i, j: (0, j)),
        ],
        out_specs=pl.BlockSpec((tm, tn), lambda i, j: (i, j)),
        out_shape=jax.ShapeDtypeStruct((bsz * seq, n), BF16),
        scratch_shapes=[pltpu.VMEM((2, n_res_tile, rows, d), F32), pltpu.VMEM((tm, d), BF16),
                        pltpu.SemaphoreType.DMA((2, n_res_tile))],
        compiler_params=_cparams(("arbitrary", "arbitrary")),
        name="norm_matmul_res",
    )(x.reshape(bsz, rows, N_RES, d), g.reshape(1, d), w, col_scale.reshape(1, n))


OUT_CHUNK = 512


def _matmul_res_kernel(a_ref, w_ref, r_ref, o_ref):
    for c0 in range(0, o_ref.shape[1], OUT_CHUNK):
        cols = slice(c0, c0 + OUT_CHUNK)
        o_ref[:, cols] = r_ref[:, cols] + _dot(a_ref[...], w_ref[:, cols].astype(BF16))


def _matmul_res(a, w, res, tm=512, tn=2048):
    m, k = a.shape
    n = w.shape[1]
    return pl.pallas_call(
        _matmul_res_kernel,
        grid=(m // tm, n // tn),
        in_specs=[
            pl.BlockSpec((tm, k), lambda i, j: (i, 0)),
            pl.BlockSpec((k, tn), lambda i, j: (0, j)),
            pl.BlockSpec((tm, tn), lambda i, j: (i, j)),
        ],
        out_specs=pl.BlockSpec((tm, tn), lambda i, j: (i, j)),
        out_shape=jax.ShapeDtypeStruct((m, n), F32),
        compiler_params=_cparams(("parallel", "arbitrary")),
        name="matmul_res",
    )(a, w, res)


def _matmul_unres_kernel(a_ref, w_ref, r_hbm, o_hbm, rbuf, obuf, rsem, osem):
    i = pl.program_id(0)
    n_steps = pl.num_programs(0)
    n_res_tile, rows, _ = rbuf.shape[1:]
    parts = N_RES // n_res_tile

    def res_fetch(step, slot):
        b, r0 = step // parts, (step % parts) * n_res_tile
        return [pltpu.make_async_copy(r_hbm.at[b, :, r0 + r, :], rbuf.at[slot, r], rsem.at[slot, r])
                for r in range(n_res_tile)]

    def out_store(step, slot):
        b, r0 = step // parts, (step % parts) * n_res_tile
        return [pltpu.make_async_copy(obuf.at[slot, r], o_hbm.at[b, :, r0 + r, :], osem.at[slot, r])
                for r in range(n_res_tile)]

    slot = i % 2

    @pl.when(i == 0)
    def _():
        for c in res_fetch(i, slot):
            c.start()

    @pl.when(i + 1 < n_steps)
    def _():
        for c in res_fetch(i + 1, 1 - slot):
            c.start()

    @pl.when(i >= 2)
    def _():
        for c in out_store(i - 2, slot):
            c.wait()

    for c in res_fetch(i, slot):
        c.wait()
    for c0 in range(0, w_ref.shape[1], OUT_CHUNK):
        cols = slice(c0, c0 + OUT_CHUNK)
        y = _dot(a_ref[...], w_ref[:, cols].astype(BF16))
        for r in range(n_res_tile):
            obuf[slot, r, :, cols] = rbuf[slot, r, :, cols] + y[r * rows:(r + 1) * rows, :]
    for c in out_store(i, slot):
        c.start()

    @pl.when(i == n_steps - 1)
    def _():
        for c in out_store(i - 1, 1 - slot) + out_store(i, slot):
            c.wait()


def _matmul_unres(a, w, res, n_res_tile=4):
    bsz, seq, d = res.shape
    k = a.shape[1]
    rows = seq // N_RES
    tm = n_res_tile * rows
    parts = N_RES // n_res_tile
    assert bsz * parts >= 2
    buf = pltpu.VMEM((2, n_res_tile, rows, d), F32)
    out = pl.pallas_call(
        _matmul_unres_kernel,
        grid=(bsz * parts,),
        in_specs=[
            pl.BlockSpec((tm, k), lambda i: (i, 0)),
            pl.BlockSpec((k, d), lambda i: (0, 0)),
            pl.BlockSpec(memory_space=pl.ANY),
        ],
        out_specs=pl.BlockSpec(memory_space=pl.ANY),
        out_shape=jax.ShapeDtypeStruct((bsz, rows, N_RES, d), F32),
        scratch_shapes=[buf, buf, pltpu.SemaphoreType.DMA((2, n_res_tile)),
                        pltpu.SemaphoreType.DMA((2, n_res_tile))],
        compiler_params=_cparams(("arbitrary",)),
        name="matmul_unres",
    )(a, w, res.reshape(bsz, rows, N_RES, d))
    return out.reshape(bsz * seq, d)


def _swiglu_kernel(x_ref, g_ref, wg_ref, wu_ref, wd_ref, o_ref, h_scr):
    @pl.when(pl.program_id(1) == 0)
    def _():
        h_scr[...] = _rms(x_ref[...], g_ref[...]).astype(BF16)
        o_ref[...] = x_ref[...]

    h = h_scr[...]
    a = _dot(h, wg_ref[...])
    b = _dot(h, wu_ref[...])
    hid = (_silu(a) * b).astype(BF16)
    o_ref[...] += _dot(hid, wd_ref[...])


def _swiglu_res(x, g, wg, wu, wd, tm=1024, tf=512):
    m, d = x.shape
    ff = wg.shape[1]
    return pl.pallas_call(
        _swiglu_kernel,
        grid=(m // tm, ff // tf),
        in_specs=[
            pl.BlockSpec((tm, d), lambda i, j: (i, 0)),
            pl.BlockSpec((1, d), lambda i, j: (0, 0)),
            pl.BlockSpec((d, tf), lambda i, j: (0, j)),
            pl.BlockSpec((d, tf), lambda i, j: (0, j)),
            pl.BlockSpec((tf, d), lambda i, j: (j, 0)),
        ],
        out_specs=pl.BlockSpec((tm, d), lambda i, j: (i, 0)),
        out_shape=jax.ShapeDtypeStruct((m, d), F32),
        scratch_shapes=[pltpu.VMEM((tm, d), BF16)],
        compiler_params=_cparams(("parallel", "arbitrary"), vmem=BIG_VMEM_LIMIT),
        name="swiglu_res",
    )(x, g.reshape(1, d), wg, wu, wd)


def _attn_bias_table():
    blk = HEAD_DIM
    iq = np.arange(blk)[:, None]
    jk = np.arange(2 * blk)[None, :]

    def offs(idx, n_chunk, size):
        return n_chunk * (idx % size) + idx // size

    masks = []
    for n_chunk in (1, 4, 16):
        size = blk // n_chunk
        lq = offs(iq, n_chunk, size)
        masks.append((offs(jk, n_chunk, size) <= lq) & (jk < blk))
        dist = lq - (offs(jk, n_chunk, 2 * size) - blk)
        masks.append((dist >= 0) & (dist <= blk))
    return np.where(np.stack(masks), 0.0, NEG).astype(np.float32)


def _attn_pool_kernel(q_ref, k_ref, v_ref, bias, pw_ref, ps_ref, o_ref,
                      qf, kf, vf, ms, ls, accs):
    h = pl.program_id(1)
    per = q_ref.shape[1] // HEAD_DIM
    lanes = [pl.ds(u * HEAD_DIM, HEAD_DIM) for u in range(per)]

    @pl.when(h < A_HEADS // per)
    def _():
        for ln in lanes:
            _attend(q_ref.at[:, ln], k_ref.at[:, ln], v_ref.at[:, ln], bias, o_ref.at[:, ln],
                    qf, kf, vf, ms, ls, accs)

    for pp in range(N_POOL // per):
        @pl.when(h == A_HEADS // per + pp)
        def _(pp=pp):
            for u, ln in enumerate(lanes):
                gi = pp * per + u
                _pool(q_ref.at[:, ln], pw_ref.at[gi], ps_ref.at[:, ln], o_ref.at[:, ln],
                      POOL_WINDOWS[gi])


def _attend(q_ref, k_ref, v_ref, bias, o_ref, qf, kf, vf, ms, ls, accs):
    seq = q_ref.shape[0]
    blk = HEAD_DIM
    bf16_rows = 16

    qf[...] = q_ref[...].astype(F32)
    kf[...] = k_ref[...].astype(F32)
    vf[...] = v_ref[...].astype(F32)

    def gather(ref, ref_f32, starts, size):
        if all(v % bf16_rows == 0 for v in [size] + starts):
            return jnp.concatenate([ref[s0:s0 + size, :] for s0 in starts], axis=0)
        return jnp.concatenate([ref_f32[s0:s0 + size, :] for s0 in starts], axis=0).astype(BF16)

    def scatter(br, starts, size, m, l, acc):
        mb = jnp.broadcast_to(m, acc.shape)
        for c, s0 in enumerate(starts):
            src = slice(c * size, (c + 1) * size)
            ms[br, s0:s0 + size, :] = mb[src]
            ls[br, s0:s0 + size, :] = l[src]
            accs[br, s0:s0 + size, :] = acc[src]

    kinds = {}
    for br, n_chunk in enumerate((1, 4, 16)):
        size = blk // n_chunk
        for grp in range(N_RES // n_chunk):
            bases = [(grp + (N_RES // n_chunk) * c) * blk for c in range(n_chunk)]
            for n in range(n_chunk):
                qs = [b0 + n * size for b0 in bases]
                if n == 0:
                    kinds.setdefault((2 * br, blk), []).append((br, size, qs, qs, size))
                else:
                    ks = [s0 - size for s0 in qs]
                    kinds.setdefault((2 * br + 1, 2 * blk), []).append(
                        (br, size, qs, ks, 2 * size))

    batch = 16
    for (bias_idx, kw), blocks in kinds.items():
        bi = bias[bias_idx][:, :kw]
        for g0 in range(0, len(blocks), batch):
            group = blocks[g0:g0 + batch]
            s = jnp.concatenate(
                [_dot_nt(gather(q_ref, qf, qs, size), gather(k_ref, kf, ks, ksize))
                 for _, size, qs, ks, ksize in group], axis=0)
            s = (s.reshape(len(group), blk, kw) + bi[None]).reshape(len(group) * blk, kw)
            m = jnp.max(s, axis=-1, keepdims=True)
            pb = jnp.exp2(s - m).astype(BF16)
            ones = jnp.ones((kw, blk), BF16)
            for g, (br, size, qs, ks, ksize) in enumerate(group):
                rows = slice(g * blk, (g + 1) * blk)
                v_one = jnp.concatenate([gather(v_ref, vf, ks, ksize), ones], axis=1)
                acc = _dot(pb[rows], v_one)
                scatter(br, qs, size, m[rows], acc[:, blk:], acc[:, :blk])

    slab = 2 * blk
    for s0 in range(0, seq, slab):
        rows = slice(s0, s0 + slab)
        m_all = jnp.maximum(jnp.maximum(ms[0, rows, :], ms[1, rows, :]), ms[2, rows, :])
        num = jnp.zeros((slab, blk), F32)
        den = jnp.zeros((slab, blk), F32)
        for br in range(3):
            w = jnp.exp2(ms[br, rows, :] - m_all)
            num = num + accs[br, rows, :] * w
            den = den + ls[br, rows, :] * w
        o_ref[rows, :] = (num / den).astype(o_ref.dtype)


def _pool(p_ref, pw_ref, ps_ref, o_ref, w):
    blk = HEAD_DIM
    pf = p_ref[...].astype(F32)
    row = lax.broadcasted_iota(jnp.int32, pf.shape, 0)
    l_idx = row % blk
    pos = N_RES * l_idx + row // blk
    s = pf
    k = 1
    while k < w:
        y = pltpu.roll(s, k * blk, axis=0)
        y1 = jnp.where(l_idx == 0, 0.0, pltpu.roll(y, 1, axis=0))
        s = s + jnp.where(row < k * blk, y1, y)
        k *= 2
    cnt = jnp.minimum(pos + 1, w).astype(F32)
    pooled = s / cnt - pf
    y = _dot(pooled.astype(BF16), pw_ref[...].astype(BF16)) * ps_ref[...]
    o_ref[...] = y.astype(o_ref.dtype)


def _attn_pool(proj, pool_w, pool_scale, per=2):
    bsz, seq, _ = proj.shape
    na = A_HEADS // per
    n_steps = na + N_POOL // per
    wide = per * HEAD_DIM
    qmap = lambda b, h: (b, 0, jnp.where(h < na, h, h + 2 * na))
    kmap = lambda b, h: (b, 0, na + jnp.minimum(h, na - 1))
    vmap = lambda b, h: (b, 0, 2 * na + jnp.minimum(h, na - 1))
    f32s = pltpu.VMEM((seq, HEAD_DIM), F32)
    br3 = pltpu.VMEM((3, seq, HEAD_DIM), F32)
    bias = jnp.asarray(_attn_bias_table())
    return pl.pallas_call(
        _attn_pool_kernel,
        grid=(bsz, n_steps),
        in_specs=[
            pl.BlockSpec((None, seq, wide), qmap),
            pl.BlockSpec((None, seq, wide), kmap),
            pl.BlockSpec((None, seq, wide), vmap),
            pl.BlockSpec(bias.shape, lambda b, h: (0, 0, 0)),
            pl.BlockSpec((N_POOL, HEAD_DIM, HEAD_DIM), lambda b, h: (0, 0, 0)),
            pl.BlockSpec((1, wide), lambda b, h: (0, jnp.maximum(h - na, 0))),
        ],
        out_specs=pl.BlockSpec((None, seq, wide), lambda b, h: (b, 0, h)),
        out_shape=jax.ShapeDtypeStruct((bsz, seq, n_steps * wide), BF16),
        scratch_shapes=[f32s, f32s, f32s, br3, br3, br3],
        compiler_params=_cparams(("parallel", "arbitrary")),
        name="attn_pool",
    )(proj, proj, proj, bias, pool_w, pool_scale.reshape(1, N_POOL * HEAD_DIM))


def _hgrn_kernel(q_ref, f_ref, i_ref, gt_ref, lb_ref, on_ref, o_ref, *scratch):
    for u in range(q_ref.shape[1] // HEAD_DIM):
        ln = pl.ds(u * HEAD_DIM, HEAD_DIM)
        _hgrn_head(*(r.at[:, ln] for r in (q_ref, f_ref, i_ref, gt_ref, lb_ref, on_ref, o_ref)),
                   *scratch)


def _hgrn_head(q_ref, f_ref, i_ref, gt_ref, lb_ref, on_ref, o_ref,
               qe_scr, o_scr, u_scr, d_scr, s_scr):
    seq = q_ref.shape[0]
    ck = HGRN_CHUNK
    grp = 4 * ck
    n_grp = seq // grp
    n_chunk = seq // ck
    lb = lb_ref[...]

    ri = lax.broadcasted_iota(jnp.int32, (grp, grp), 0)
    ci = lax.broadcasted_iota(jnp.int32, (grp, grp), 1)
    tri_mask = (ri // ck == ci // ck) & (ci <= ri)
    tri = tri_mask.astype(BF16)

    groups = [slice(gi * grp, (gi + 1) * grp) for gi in range(n_grp)]
    q = _half_silu(q_ref[...].astype(F32))
    f = (0.5 + 0.5 * lb) + (0.5 - 0.5 * lb) * jnp.tanh(f_ref[...].astype(F32))
    k = 1.0 - f
    g = jnp.log2(f)
    g_hi = g.astype(BF16)
    g_lo = (g - g_hi.astype(F32)).astype(BF16)
    b = jnp.concatenate([_dot(tri, g_hi[r]) + _dot(tri, g_lo[r]) for r in groups], axis=0)
    b3 = b.reshape(n_chunk, ck, HEAD_DIM)
    b_mid = b3[:, ck // 2 - 1:ck // 2, :]
    b_last = b3[:, ck - 1:ck, :]
    q3 = q.reshape(n_chunk, ck, HEAD_DIM)
    k3 = k.reshape(n_chunk, ck, HEAD_DIM)
    q_mid = q3 * jnp.exp2(b3 - b_mid)
    k_mid = k3 * jnp.exp2(b_mid - b3)
    qs = q_mid.reshape(seq, HEAD_DIM).astype(BF16)
    ks = k_mid.reshape(seq, HEAD_DIM).astype(BF16)
    qe_scr[...] = (q_mid * jnp.exp2(b_mid)).reshape(seq, HEAD_DIM).astype(BF16)
    kd = (k_mid * jnp.exp2(b_last - b_mid)).reshape(seq, HEAD_DIM).astype(BF16)
    d_scr[...] = jnp.exp2(b_last).reshape(n_chunk, HEAD_DIM)
    for r in groups:
        att = jnp.where(tri_mask, _dot_nt(qs[r], ks[r]), 0.0)
        o_scr[r, :] = _dot(att.astype(BF16), i_ref[r, :])
    for c in range(n_chunk):
        cr = slice(c * ck, (c + 1) * ck)
        u_scr[c] = _dot_tn(i_ref[cr, :], kd[cr])

    st = jnp.zeros((HEAD_DIM, HEAD_DIM), F32)
    for c in range(n_chunk):
        s_scr[c] = st.astype(BF16)
        st = st * d_scr[c:c + 1, :] + u_scr[c]

    for gi in range(n_grp):
        rows = slice(gi * grp, (gi + 1) * grp)
        inter = jnp.concatenate(
            [_dot_nt(qe_scr[gi * grp + c * ck:gi * grp + (c + 1) * ck, :], s_scr[gi * 4 + c])
             for c in range(4)], axis=0)
        o = _rms(o_scr[rows, :] + inter, on_ref[...])
        o_ref[rows, :] = (o * _half_silu(gt_ref[rows, :].astype(F32))).astype(o_ref.dtype)


def _hgrn(proj, lb, out_norm, per=2):
    bsz, seq, _ = proj.shape
    nh = HGRN_HEADS // per
    wide = per * HEAD_DIM
    n_chunk = seq // HGRN_CHUNK
    spec = lambda off: pl.BlockSpec((None, seq, wide), lambda b, h: (b, 0, off + h))
    vec = pl.BlockSpec((1, wide), lambda b, h: (0, h))
    return pl.pallas_call(
        _hgrn_kernel,
        grid=(bsz, nh),
        in_specs=[spec(0), spec(nh), spec(2 * nh), spec(3 * nh), vec, vec],
        out_specs=pl.BlockSpec((None, seq, wide), lambda b, h: (b, 0, h)),
        out_shape=jax.ShapeDtypeStruct((bsz, seq, nh * wide), BF16),
        scratch_shapes=[
            pltpu.VMEM((seq, HEAD_DIM), BF16),
            pltpu.VMEM((seq, HEAD_DIM), F32),
            pltpu.VMEM((n_chunk, HEAD_DIM, HEAD_DIM), F32),
            pltpu.VMEM((n_chunk, HEAD_DIM), F32),
            pltpu.VMEM((n_chunk, HEAD_DIM, HEAD_DIM), BF16),
        ],
        compiler_params=_cparams(("parallel", "arbitrary")),
        name="hgrn2",
    )(proj, proj, proj, proj, lb.reshape(1, -1), out_norm.reshape(1, -1))


def _router_kernel(x_ref, g_ref, w_ref, h_ref, meta_ref, cnt_ref, carry):
    i = pl.program_id(0)
    tm = x_ref.shape[0]

    @pl.when(i == 0)
    def _():
        carry[...] = jnp.zeros_like(carry)

    h = _rms(x_ref[...], g_ref[...])
    h_ref[...] = _pack_bf16_pair(h)
    h_hi = h.astype(BF16)
    h_lo = (h - h_hi.astype(F32)).astype(BF16)
    w = w_ref[...]
    w_hi = w.astype(BF16)
    w_lo = (w - w_hi.astype(F32)).astype(BF16)
    prod = _dot(jnp.concatenate([h_hi, h_lo], axis=0), jnp.concatenate([w_hi, w_lo], axis=1))
    logits = (prod[:tm, :LANES] + prod[:tm, LANES:]) + (prod[tm:, :LANES] + prod[tm:, LANES:])
    lane = lax.broadcasted_iota(jnp.int32, logits.shape, 1)
    logits = jnp.where(lane < N_EXPERTS, logits, NEG)
    m1 = jnp.max(logits, axis=-1, keepdims=True)
    i1 = jnp.min(jnp.where(logits == m1, lane, LANES), axis=-1, keepdims=True)
    rest = jnp.where(lane == i1, NEG, logits)
    m2 = jnp.max(rest, axis=-1, keepdims=True)
    i2 = jnp.min(jnp.where(rest == m2, lane, LANES), axis=-1, keepdims=True)
    e = jnp.exp(m2 - m1)
    g1 = 1.0 / (1.0 + e)
    g2 = e / (1.0 + e)
    oh1 = (lane == i1).astype(F32)
    oh2 = (lane == i2).astype(F32)
    cnt = oh1 + oh2
    ri = lax.broadcasted_iota(jnp.int32, (tm, tm), 0)
    ci = lax.broadcasted_iota(jnp.int32, (tm, tm), 1)
    before = (ci < ri).astype(BF16)
    prior = _dot(before, cnt.astype(BF16)) + carry[...]
    r1 = jnp.sum(oh1 * prior, axis=-1, keepdims=True)
    r2 = jnp.sum(oh2 * prior, axis=-1, keepdims=True)
    carry[...] += jnp.sum(cnt, axis=0, keepdims=True)
    cnt_ref[...] = carry[...]
    meta = jnp.where(lane == 0, i1.astype(F32), 0.0)
    meta = jnp.where(lane == 1, i2.astype(F32), meta)
    meta = jnp.where(lane == 2, r1, meta)
    meta = jnp.where(lane == 3, r2, meta)
    meta = jnp.where(lane == 4, g1, meta)
    meta = jnp.where(lane == 5, g2, meta)
    meta_ref[...] = meta


def _router(x, g, router_w, tm=512):
    t, d = x.shape
    w_pad = jnp.pad(router_w, ((0, 0), (0, LANES - router_w.shape[1])))
    return pl.pallas_call(
        _router_kernel,
        grid=(t // tm,),
        in_specs=[
            pl.BlockSpec((tm, d), lambda i: (i, 0)),
            pl.BlockSpec((1, d), lambda i: (0, 0)),
            pl.BlockSpec((d, LANES), lambda i: (0, 0)),
        ],
        out_specs=[
            pl.BlockSpec((tm, d // 2), lambda i: (i, 0)),
            pl.BlockSpec((tm, LANES), lambda i: (i, 0)),
            pl.BlockSpec((1, LANES), lambda i: (0, 0)),
        ],
        out_shape=[
            jax.ShapeDtypeStruct((t, d // 2), jnp.uint32),
            jax.ShapeDtypeStruct((t, LANES), F32),
            jax.ShapeDtypeStruct((1, LANES), F32),
        ],
        scratch_shapes=[pltpu.VMEM((1, LANES), F32)],
        compiler_params=_cparams(("arbitrary",)),
        name="moe_router",
    )(x, g.reshape(1, d), w_pad)


def _row_copy(src, dst, s, d, sem):
    return pltpu.make_async_copy(src.at[pl.ds(s, 1)], dst.at[pl.ds(d, 1)], sem)


def _dispatch_kernel(dest_ref, h_ref, o_hbm, sem):
    tm = h_ref.shape[0]

    def start(t, carry):
        _row_copy(h_ref, o_hbm, t, dest_ref[0, 0, 2 * t], sem).start(priority=0)
        _row_copy(h_ref, o_hbm, t, dest_ref[0, 0, 2 * t + 1], sem).start(priority=1)
        return carry

    lax.fori_loop(0, tm, start, 0, unroll=8)

    def wait(t, carry):
        _row_copy(h_ref, o_hbm, 0, 0, sem).wait()
        _row_copy(h_ref, o_hbm, 0, 0, sem).wait()
        return carry

    lax.fori_loop(0, tm, wait, 0, unroll=8)


def _dispatch(h, dest, n_rows, tm=2048):
    t, d = h.shape
    dest3 = dest.reshape(t // tm, 1, 2 * tm)
    return pl.pallas_call(
        _dispatch_kernel,
        grid=(t // tm,),
        in_specs=[
            pl.BlockSpec((1, 1, 2 * tm), lambda i: (i, 0, 0), memory_space=pltpu.SMEM),
            pl.BlockSpec((tm, d), lambda i: (i, 0)),
        ],
        out_specs=pl.BlockSpec(memory_space=pl.ANY),
        out_shape=jax.ShapeDtypeStruct((n_rows, d), h.dtype),
        scratch_shapes=[pltpu.SemaphoreType.DMA(())],
        compiler_params=_cparams(("arbitrary",)),
        name="moe_dispatch",
    )(dest3, h)


def _moe_kernel(be_ref, bb_ref, nv_ref, na_ref, x_ref, wg_ref, wu_ref, wd_ref, o_ref, xb_scr):
    i = pl.program_id(0)
    j = pl.program_id(1)
    n_valid = nv_ref[i]

    @pl.when(i < na_ref[0])
    def _():
        @pl.when(j == 0)
        def _():
            row = lax.broadcasted_iota(jnp.int32, x_ref.shape, 0)
            lo, hi = _unpack_bf16_pair(jnp.where(row < n_valid, x_ref[...], jnp.uint32(0)))
            xb_scr[:, :lo.shape[1]] = lo
            xb_scr[:, lo.shape[1]:] = hi
            o_ref[...] = jnp.zeros_like(o_ref)

        def run(rows):
            xb = xb_scr[rows, :]
            a = _dot(xb, wg_ref[0].astype(BF16))
            b = _dot(xb, wu_ref[0].astype(BF16))
            hid = (_silu(a) * b).astype(BF16)
            o_ref[rows, :] += _dot(hid, wd_ref[0].astype(BF16))

        part = x_ref.shape[0] // MOE_PARTS
        for nq in range(1, MOE_PARTS + 1):
            @pl.when((n_valid > (nq - 1) * part) & (n_valid <= nq * part))
            def _(nq=nq):
                run(slice(0, nq * part))


def _moe_experts(x_rows, wg, wu, wd, blk_e, blk_idx, blk_valid, n_live, n_blocks):
    dp = x_rows.shape[1]
    d = 2 * dp
    dff = wg.shape[2]
    n_ff = dff // MOE_TF
    r = MOE_ROWS

    def ffi(i, j, na):
        return jnp.where(i < na[0], j, n_ff - 1)

    grid_spec = pltpu.PrefetchScalarGridSpec(
        num_scalar_prefetch=4,
        grid=(n_blocks, n_ff),
        in_specs=[
            pl.BlockSpec((r, dp), lambda i, j, be, bb, nv, na: (bb[i], 0)),
            pl.BlockSpec((1, d, MOE_TF), lambda i, j, be, bb, nv, na: (be[i], 0, ffi(i, j, na))),
            pl.BlockSpec((1, d, MOE_TF), lambda i, j, be, bb, nv, na: (be[i], 0, ffi(i, j, na))),
            pl.BlockSpec((1, MOE_TF, d), lambda i, j, be, bb, nv, na: (be[i], ffi(i, j, na), 0)),
        ],
        out_specs=pl.BlockSpec((r, d), lambda i, j, be, bb, nv, na: (bb[i], 0)),
        scratch_shapes=[pltpu.VMEM((r, d), BF16)],
    )
    return pl.pallas_call(
        _moe_kernel,
        grid_spec=grid_spec,
        out_shape=jax.ShapeDtypeStruct((x_rows.shape[0], d), F32),
        compiler_params=_cparams(("arbitrary", "arbitrary"), vmem=BIG_VMEM_LIMIT),
        name="moe_experts",
    )(blk_e, blk_idx, blk_valid, n_live, x_rows, wg, wu, wd)


def _combine_kernel(dest_ref, next_ref, x_ref, meta_ref, nf_ref, y_hbm, o_ref, buf, sem):
    i = pl.program_id(0)
    tm = x_ref.shape[0]
    slot = i % 2

    def gather(idx_ref, s):
        def start(t, carry):
            _row_copy(y_hbm, buf.at[s, 0], idx_ref[0, 0, 2 * t], t, sem.at[s]).start(priority=0)
            _row_copy(y_hbm, buf.at[s, 1], idx_ref[0, 0, 2 * t + 1], t, sem.at[s]).start(priority=1)
            return carry

        lax.fori_loop(0, tm, start, 0, unroll=8)

    @pl.when(i == 0)
    def _():
        gather(dest_ref, slot)

    @pl.when(i + 1 < pl.num_programs(0))
    def _():
        gather(next_ref, 1 - slot)

    def wait(t, carry):
        _row_copy(y_hbm, buf.at[slot, 0], 0, 0, sem.at[slot]).wait()
        _row_copy(y_hbm, buf.at[slot, 1], 0, 0, sem.at[slot]).wait()
        return carry

    lax.fori_loop(0, tm, wait, 0, unroll=8)

    meta = meta_ref[...]
    g1 = meta[:, 4:5]
    g2 = meta[:, 5:6]
    x = x_ref[...] + (g1 * buf[slot, 0] + g2 * buf[slot, 1])
    o_ref[...] = _rms(x, nf_ref[...])


def _combine(x, meta, norm_final, y_rows, dest, tm=512):
    t, d = x.shape
    n_steps = t // tm
    dest3 = dest.reshape(n_steps, 1, 2 * tm)
    return pl.pallas_call(
        _combine_kernel,
        grid=(n_steps,),
        in_specs=[
            pl.BlockSpec((1, 1, 2 * tm), lambda i: (i, 0, 0), memory_space=pltpu.SMEM),
            pl.BlockSpec((1, 1, 2 * tm), lambda i: (jnp.minimum(i + 1, n_steps - 1), 0, 0),
                         memory_space=pltpu.SMEM),
            pl.BlockSpec((tm, d), lambda i: (i, 0)),
            pl.BlockSpec((tm, LANES), lambda i: (i, 0)),
            pl.BlockSpec((1, d), lambda i: (0, 0)),
            pl.BlockSpec(memory_space=pl.ANY),
        ],
        out_specs=pl.BlockSpec((tm, d), lambda i: (i, 0)),
        out_shape=jax.ShapeDtypeStruct((t, d), F32),
        scratch_shapes=[pltpu.VMEM((2, 2, tm, d), F32), pltpu.SemaphoreType.DMA((2,))],
        compiler_params=_cparams(("arbitrary",)),
        name="moe_combine",
    )(dest3, dest3, x, meta, norm_final.reshape(1, d), y_rows)


def _moe_block_table(counts, t):
    r = MOE_ROWS
    n_blocks = (2 * t) // r + N_EXPERTS
    nblk = (counts + r - 1) // r
    ends = jnp.cumsum(nblk)
    starts = ends - nblk
    n_live = ends[-1]
    step = jnp.minimum(jnp.arange(n_blocks, dtype=jnp.int32), n_live - 1)
    blk_e = jnp.sum(step[:, None] >= ends[None, :], axis=1).astype(jnp.int32)
    local = step - starts[blk_e]
    blk_valid = jnp.clip(counts[blk_e] - local * r, 0, r)
    return (starts * r, blk_e, step, blk_valid.astype(jnp.int32),
            n_live.reshape(1).astype(jnp.int32), n_blocks)


def kernel(x, norm_mix, norm_ffn, w_in_even, pool_w, pool_scale, w_out_even, ffn_w_gate,
           ffn_w_up, ffn_w_down, w_in_odd, lower_bound_logits, hgrn_out_norm, w_out_odd,
           router_w, moe_w_gate, moe_w_up, moe_w_down, norm_final):
    bsz, seq, d = x.shape
    t = bsz * seq

    lb_all = jnp.cumsum(jax.nn.softmax(lower_bound_logits.astype(F32), axis=0), axis=0)
    lb_all = lb_all - lb_all[:1]

    n_in = w_in_even.shape[2]
    q_scale = math.log2(math.e) / math.sqrt(HEAD_DIM)
    col_scale = jnp.where(jnp.arange(n_in) < A_WIDTH, q_scale, 1.0).astype(F32)
    proj = _norm_matmul_res(x, norm_mix[0], w_in_even[0].astype(BF16), col_scale)
    mix = _attn_pool(proj.reshape(bsz, seq, -1), pool_w[0], pool_scale[0])
    xt = _matmul_unres(mix.reshape(t, -1), w_out_even[0], x)
    xt = _swiglu_res(xt, norm_ffn[0], ffn_w_gate[0].astype(BF16), ffn_w_up[0].astype(BF16),
                     ffn_w_down[0].astype(BF16))

    wid = HGRN_HEADS * HEAD_DIM
    col = jnp.arange(4 * wid) // wid
    proj = _norm_matmul(xt, norm_mix[1], w_in_odd[0].astype(BF16),
                        jnp.where(col == 2, 1.0, 0.5).astype(F32))
    o = _hgrn(proj.reshape(bsz, seq, -1), lb_all[1], hgrn_out_norm[0])
    xt = _matmul_res(o.reshape(t, -1), w_out_odd[0], xt)

    h, meta, cnt = _router(xt, norm_ffn[1], router_w[0])
    counts = cnt[0, :N_EXPERTS].astype(jnp.int32)
    row0, blk_e, blk_idx, blk_valid, n_live, n_blocks = _moe_block_table(counts, t)
    dest = (row0[meta[:, 0:2].astype(jnp.int32)] + meta[:, 2:4].astype(jnp.int32)).reshape(-1)
    x_rows = _dispatch(h, dest, n_blocks * MOE_ROWS)
    y_rows = _moe_experts(x_rows, moe_w_gate[0], moe_w_up[0], moe_w_down[0],
                          blk_e, blk_idx, blk_valid, n_live, n_blocks)
    out = _combine(xt, meta, norm_final, y_rows, dest)
    return out.reshape(bsz, seq, d)
```

```python
import math

import jax
import jax.numpy as jnp
import numpy as np
from jax import lax
from jax.experimental import pallas as pl
from jax.experimental.pallas import tpu as pltpu

F32 = jnp.float32
BF16 = jnp.bfloat16

HEAD_DIM = 128
A_HEADS = 12
A_WIDTH = A_HEADS * HEAD_DIM
N_POOL = 4
POOL_WINDOWS = (2, 4, 8, 16)
HGRN_HEADS = 16
HGRN_CHUNK = 64
N_EXPERTS = 8
EPS = 1e-6
NEG = -1e30

LANES = 128
VMEM_LIMIT = 56 * 1024 * 1024

MOE_ROWS = 1024
MOE_PARTS = 4
MOE_TF = 512
BIG_VMEM_LIMIT = 62 * 1024 * 1024
HI16 = 0xFFFF0000


def _pack_bf16_pair(x):
    n = x.shape[1] // 2
    bits = lax.bitcast_convert_type(x.astype(BF16).astype(F32), jnp.uint32)
    return (bits[:, n:] & jnp.uint32(HI16)) | (bits[:, :n] >> 16)


def _unpack_bf16_pair(p):
    lo = lax.bitcast_convert_type(p << 16, F32).astype(BF16)
    hi = lax.bitcast_convert_type(p & jnp.uint32(HI16), F32).astype(BF16)
    return lo, hi

N_RES = 16


def _cparams(sem, vmem=VMEM_LIMIT, flags=None):
    return pltpu.CompilerParams(dimension_semantics=sem, vmem_limit_bytes=vmem, flags=flags)


def _rms(x, g):
    ms = jnp.mean(x * x, axis=-1, keepdims=True)
    return x * lax.rsqrt(ms + EPS) * g


def _half_silu(h):
    return h + h * jnp.tanh(h)


def _silu(x):
    return _half_silu(0.5 * x)


def _dot(a, b):
    return jnp.dot(a, b, preferred_element_type=F32)


def _dot_nt(a, b):
    return lax.dot_general(a, b, (((1,), (1,)), ((), ())), preferred_element_type=F32)


def _dot_tn(a, b):
    return lax.dot_general(a, b, (((0,), (0,)), ((), ())), preferred_element_type=F32)


def _norm_matmul_kernel(x_ref, g_ref, w_ref, cs_ref, o_ref, h_scr):
    @pl.when(pl.program_id(1) == 0)
    def _():
        h_scr[...] = _rms(x_ref[...], g_ref[...]).astype(BF16)

    o_ref[...] = (_dot(h_scr[...], w_ref[...]) * cs_ref[...]).astype(o_ref.dtype)


def _norm_matmul(x, g, w, col_scale, tm=1024, tn=2048):
    m, d = x.shape
    n = w.shape[1]
    return pl.pallas_call(
        _norm_matmul_kernel,
        grid=(m // tm, n // tn),
        in_specs=[
            pl.BlockSpec((tm, d), lambda i, j: (i, 0)),
            pl.BlockSpec((1, d), lambda i, j: (0, 0)),
            pl.BlockSpec((d, tn), lambda i, j: (0, j)),
            pl.BlockSpec((1, tn), lambda i, j: (0, j)),
        ],
        out_specs=pl.BlockSpec((tm, tn), lambda i, j: (i, j)),
        out_shape=jax.ShapeDtypeStruct((m, n), BF16),
        scratch_shapes=[pltpu.VMEM((tm, d), BF16)],
        compiler_params=_cparams(("parallel", "arbitrary")),
        name="norm_matmul",
    )(x, g.reshape(1, d), w, col_scale.reshape(1, n))


def _norm_matmul_res_kernel(x_hbm, g_ref, w_ref, cs_ref, o_ref, xbuf, h_scr, sem):
    i = pl.program_id(0)
    n_res_tile, rows, _ = xbuf.shape[1:]
    halves = N_RES // n_res_tile

    def fetch(step, slot):
        b = step // halves
        r0 = (step % halves) * n_res_tile
        return [pltpu.make_async_copy(x_hbm.at[b, :, r0 + r, :], xbuf.at[slot, r], sem.at[slot, r])
                for r in range(n_res_tile)]

    @pl.when(pl.program_id(1) == 0)
    def _():
        slot = i % 2

        @pl.when(i == 0)
        def _():
            for c in fetch(i, slot):
                c.start()

        @pl.when(i + 1 < pl.num_programs(0))
        def _():
            for c in fetch(i + 1, 1 - slot):
                c.start()

        for r, c in enumerate(fetch(i, slot)):
            c.wait()
            h_scr[r * rows:(r + 1) * rows, :] = _rms(xbuf[slot, r], g_ref[...]).astype(BF16)

    o_ref[...] = (_dot(h_scr[...], w_ref[...]) * cs_ref[...]).astype(o_ref.dtype)


def _norm_matmul_res(x, g, w, col_scale, n_res_tile=8, tn=2560):
    bsz, seq, d = x.shape
    n = w.shape[1]
    rows = seq // N_RES
    tm = n_res_tile * rows
    halves = N_RES // n_res_tile
    return pl.pallas_call(
        _norm_matmul_res_kernel,
        grid=(bsz * halves, n // tn),
        in_specs=[
            pl.BlockSpec(memory_space=pl.ANY),
            pl.BlockSpec((1, d), lambda i, j: (0, 0)),
            pl.BlockSpec((d, tn), lambda i, j: (0, j)),
            pl.BlockSpec((1, tn), lambda i, j: (0, j)),
        ],
        out_specs=pl.BlockSpec((tm, tn), lambda i, j: (i, j)),
        out_shape=jax.ShapeDtypeStruct((bsz * seq, n), BF16),
        scratch_shapes=[pltpu.VMEM((2, n_res_tile, rows, d), F32), pltpu.VMEM((tm, d), BF16),
                        pltpu.SemaphoreType.DMA((2, n_res_tile))],
        compiler_params=_cparams(("arbitrary", "arbitrary")),
        name="norm_matmul_res",
    )(x.reshape(bsz, rows, N_RES, d), g.reshape(1, d), w, col_scale.reshape(1, n))


OUT_CHUNK = 512


def _matmul_res_kernel(a_ref, w_ref, r_ref, o_ref):
    for c0 in range(0, o_ref.shape[1], OUT_CHUNK):
        cols = slice(c0, c0 + OUT_CHUNK)
        o_ref[:, cols] = r_ref[:, cols] + _dot(a_ref[...], w_ref[:, cols].astype(BF16))


def _matmul_res(a, w, res, tm=512, tn=2048):
    m, k = a.shape
    n = w.shape[1]
    return pl.pallas_call(
        _matmul_res_kernel,
        grid=(m // tm, n // tn),
        in_specs=[
            pl.BlockSpec((tm, k), lambda i, j: (i, 0)),
            pl.BlockSpec((k, tn), lambda i, j: (0, j)),
            pl.BlockSpec((tm, tn), lambda i, j: (i, j)),
        ],
        out_specs=pl.BlockSpec((tm, tn), lambda i, j: (i, j)),
        out_shape=jax.ShapeDtypeStruct((m, n), F32),
        compiler_params=_cparams(("parallel", "arbitrary")),
        name="matmul_res",
    )(a, w, res)


def _matmul_unres_kernel(a_ref, w_ref, r_hbm, o_hbm, rbuf, obuf, rsem, osem):
    i = pl.program_id(0)
    n_steps = pl.num_programs(0)
    n_res_tile, rows, _ = rbuf.shape[1:]
    parts = N_RES // n_res_tile

    def res_fetch(step, slot):
        b, r0 = step // parts, (step % parts) * n_res_tile
        return [pltpu.make_async_copy(r_hbm.at[b, :, r0 + r, :], rbuf.at[slot, r], rsem.at[slot, r])
                for r in range(n_res_tile)]

    def out_store(step, slot):
        b, r0 = step // parts, (step % parts) * n_res_tile
        return [pltpu.make_async_copy(obuf.at[slot, r], o_hbm.at[b, :, r0 + r, :], osem.at[slot, r])
                for r in range(n_res_tile)]

    slot = i % 2

    @pl.when(i == 0)
    def _():
        for c in res_fetch(i, slot):
            c.start()

    @pl.when(i + 1 < n_steps)
    def _():
        for c in res_fetch(i + 1, 1 - slot):
            c.start()

    @pl.when(i >= 2)
    def _():
        for c in out_store(i - 2, slot):
            c.wait()

    for c in res_fetch(i, slot):
        c.wait()
    for c0 in range(0, w_ref.shape[1], OUT_CHUNK):
        cols = slice(c0, c0 + OUT_CHUNK)
        y = _dot(a_ref[...], w_ref[:, cols].astype(BF16))
        for r in range(n_res_tile):
            obuf[slot, r, :, cols] = rbuf[slot, r, :, cols] + y[r * rows:(r + 1) * rows, :]
    for c in out_store(i, slot):
        c.start()

    @pl.when(i == n_steps - 1)
    def _():
        for c in out_store(i - 1, 1 - slot) + out_store(i, slot):
            c.wait()


def _matmul_unres(a, w, res, n_res_tile=4):
    bsz, seq, d = res.shape
    k = a.shape[1]
    rows = seq // N_RES
    tm = n_res_tile * rows
    parts = N_RES // n_res_tile
    assert bsz * parts >= 2
    buf = pltpu.VMEM((2, n_res_tile, rows, d), F32)
    out = pl.pallas_call(
        _matmul_unres_kernel,
        grid=(bsz * parts,),
        in_specs=[
            pl.BlockSpec((tm, k), lambda i: (i, 0)),
            pl.BlockSpec((k, d), lambda i: (0, 0)),
            pl.BlockSpec(memory_space=pl.ANY),
        ],
        out_specs=pl.BlockSpec(memory_space=pl.ANY),
        out_shape=jax.ShapeDtypeStruct((bsz, rows, N_RES, d), F32),
        scratch_shapes=[buf, buf, pltpu.SemaphoreType.DMA((2, n_res_tile)),
                        pltpu.SemaphoreType.DMA((2, n_res_tile))],
        compiler_params=_cparams(("arbitrary",)),
        name="matmul_unres",
    )(a, w, res.reshape(bsz, rows, N_RES, d))
    return out.reshape(bsz * seq, d)


def _swiglu_kernel(x_ref, g_ref, wg_ref, wu_ref, wd_ref, o_ref, h_scr):
    @pl.when(pl.program_id(1) == 0)
    def _():
        h_scr[...] = _rms(x_ref[...], g_ref[...]).astype(BF16)
        o_ref[...] = x_ref[...]

    h = h_scr[...]
    a = _dot(h, wg_ref[...])
    b = _dot(h, wu_ref[...])
    hid = (_silu(a) * b).astype(BF16)
    o_ref[...] += _dot(hid, wd_ref[...])


def _swiglu_res(x, g, wg, wu, wd, tm=1024, tf=512):
    m, d = x.shape
    ff = wg.shape[1]
    return pl.pallas_call(
        _swiglu_kernel,
        grid=(m // tm, ff // tf),
        in_specs=[
            pl.BlockSpec((tm, d), lambda i, j: (i, 0)),
            pl.BlockSpec((1, d), lambda i, j: (0, 0)),
            pl.BlockSpec((d, tf), lambda i, j: (0, j)),
            pl.BlockSpec((d, tf), lambda i, j: (0, j)),
            pl.BlockSpec((tf, d), lambda i, j: (j, 0)),
        ],
        out_specs=pl.BlockSpec((tm, d), lambda i, j: (i, 0)),
        out_shape=jax.ShapeDtypeStruct((m, d), F32),
        scratch_shapes=[pltpu.VMEM((tm, d), BF16)],
        compiler_params=_cparams(("parallel", "arbitrary"), vmem=BIG_VMEM_LIMIT),
        name="swiglu_res",
    )(x, g.reshape(1, d), wg, wu, wd)


def _attn_bias_table():
    blk = HEAD_DIM
    iq = np.arange(blk)[:, None]
    jk = np.arange(2 * blk)[None, :]

    def offs(idx, n_chunk, size):
        return n_chunk * (idx % size) + idx // size

    masks = []
    for n_chunk in (1, 4, 16):
        size = blk // n_chunk
        lq = offs(iq, n_chunk, size)
        masks.append((offs(jk, n_chunk, size) <= lq) & (jk < blk))
        dist = lq - (offs(jk, n_chunk, 2 * size) - blk)
        masks.append((dist >= 0) & (dist <= blk))
    return np.where(np.stack(masks), 0.0, NEG).astype(np.float32)


def _attn_pool_kernel(q_ref, k_ref, v_ref, bias, pw_ref, ps_ref, o_ref,
                      qf, kf, vf, ms, ls, accs):
    h = pl.program_id(1)
    per = q_ref.shape[1] // HEAD_DIM
    lanes = [pl.ds(u * HEAD_DIM, HEAD_DIM) for u in range(per)]

    @pl.when(h < A_HEADS // per)
    def _():
        for ln in lanes:
            _attend(q_ref.at[:, ln], k_ref.at[:, ln], v_ref.at[:, ln], bias, o_ref.at[:, ln],
                    qf, kf, vf, ms, ls, accs)

    for pp in range(N_POOL // per):
        @pl.when(h == A_HEADS // per + pp)
        def _(pp=pp):
            for u, ln in enumerate(lanes):
                gi = pp * per + u
                _pool(q_ref.at[:, ln], pw_ref.at[gi], ps_ref.at[:, ln], o_ref.at[:, ln],
                      POOL_WINDOWS[gi])


def _attend(q_ref, k_ref, v_ref, bias, o_ref, qf, kf, vf, ms, ls, accs):
    seq = q_ref.shape[0]
    blk = HEAD_DIM
    bf16_rows = 16

    qf[...] = q_ref[...].astype(F32)
    kf[...] = k_ref[...].astype(F32)
    vf[...] = v_ref[...].astype(F32)

    def gather(ref, ref_f32, starts, size):
        if all(v % bf16_rows == 0 for v in [size] + starts):
            return jnp.concatenate([ref[s0:s0 + size, :] for s0 in starts], axis=0)
        return jnp.concatenate([ref_f32[s0:s0 + size, :] for s0 in starts], axis=0).astype(BF16)

    def scatter(br, starts, size, m, l, acc):
        mb = jnp.broadcast_to(m, acc.shape)
        for c, s0 in enumerate(starts):
            src = slice(c * size, (c + 1) * size)
            ms[br, s0:s0 + size, :] = mb[src]
            ls[br, s0:s0 + size, :] = l[src]
            accs[br, s0:s0 + size, :] = acc[src]

    kinds = {}
    for br, n_chunk in enumerate((1, 4, 16)):
        size = blk // n_chunk
        for grp in range(N_RES // n_chunk):
            bases = [(grp + (N_RES // n_chunk) * c) * blk for c in range(n_chunk)]
            for n in range(n_chunk):
                qs = [b0 + n * size for b0 in bases]
                if n == 0:
                    kinds.setdefault((2 * br, blk), []).append((br, size, qs, qs, size))
                else:
                    ks = [s0 - size for s0 in qs]
                    kinds.setdefault((2 * br + 1, 2 * blk), []).append(
                        (br, size, qs, ks, 2 * size))

    batch = 16
    for (bias_idx, kw), blocks in kinds.items():
        bi = bias[bias_idx][:, :kw]
        for g0 in range(0, len(blocks), batch):
            group = blocks[g0:g0 + batch]
            s = jnp.concatenate(
                [_dot_nt(gather(q_ref, qf, qs, size), gather(k_ref, kf, ks, ksize))
                 for _, size, qs, ks, ksize in group], axis=0)
            s = (s.reshape(len(group), blk, kw) + bi[None]).reshape(len(group) * blk, kw)
            m = jnp.max(s, axis=-1, keepdims=True)
            pb = jnp.exp2(s - m).astype(BF16)
            ones = jnp.ones((kw, blk), BF16)
            for g, (br, size, qs, ks, ksize) in enumerate(group):
                rows = slice(g * blk, (g + 1) * blk)
                v_one = jnp.concatenate([gather(v_ref, vf, ks, ksize), ones], axis=1)
                acc = _dot(pb[rows], v_one)
                scatter(br, qs, size, m[rows], acc[:, blk:], acc[:, :blk])

    slab = 2 * blk
    for s0 in range(0, seq, slab):
        rows = slice(s0, s0 + slab)
        m_all = jnp.maximum(jnp.maximum(ms[0, rows, :], ms[1, rows, :]), ms[2, rows, :])
        num = jnp.zeros((slab, blk), F32)
        den = jnp.zeros((slab, blk), F32)
        for br in range(3):
            w = jnp.exp2(ms[br, rows, :] - m_all)
            num = num + accs[br, rows, :] * w
            den = den + ls[br, rows, :] * w
        o_ref[rows, :] = (num / den).astype(o_ref.dtype)


def _pool(p_ref, pw_ref, ps_ref, o_ref, w):
    blk = HEAD_DIM
    pf = p_ref[...].astype(F32)
    row = lax.broadcasted_iota(jnp.int32, pf.shape, 0)
    l_idx = row % blk
    pos = N_RES * l_idx + row // blk
    s = pf
    k = 1
    while k < w:
        y = pltpu.roll(s, k * blk, axis=0)
        y1 = jnp.where(l_idx == 0, 0.0, pltpu.roll(y, 1, axis=0))
        s = s + jnp.where(row < k * blk, y1, y)
        k *= 2
    cnt = jnp.minimum(pos + 1, w).astype(F32)
    pooled = s / cnt - pf
    y = _dot(pooled.astype(BF16), pw_ref[...].astype(BF16)) * ps_ref[...]
    o_ref[...] = y.astype(o_ref.dtype)


def _attn_pool(proj, pool_w, pool_scale, per=2):
    bsz, seq, _ = proj.shape
    na = A_HEADS // per
    n_steps = na + N_POOL // per
    wide = per * HEAD_DIM
    qmap = lambda b, h: (b, 0, jnp.where(h < na, h, h + 2 * na))
    kmap = lambda b, h: (b, 0, na + jnp.minimum(h, na - 1))
    vmap = lambda b, h: (b, 0, 2 * na + jnp.minimum(h, na - 1))
    f32s = pltpu.VMEM((seq, HEAD_DIM), F32)
    br3 = pltpu.VMEM((3, seq, HEAD_DIM), F32)
    bias = jnp.asarray(_attn_bias_table())
    return pl.pallas_call(
        _attn_pool_kernel,
        grid=(bsz, n_steps),
        in_specs=[
            pl.BlockSpec((None, seq, wide), qmap),
            pl.BlockSpec((None, seq, wide), kmap),
            pl.BlockSpec((None, seq, wide), vmap),
            pl.BlockSpec(bias.shape, lambda b, h: (0, 0, 0)),
            pl.BlockSpec((N_POOL, HEAD_DIM, HEAD_DIM), lambda b, h: (0, 0, 0)),
            pl.BlockSpec((1, wide), lambda b, h: (0, jnp.maximum(h - na, 0))),
        ],
        out_specs=pl.BlockSpec((None, seq, wide), lambda b, h: (b, 0, h)),
        out_shape=jax.ShapeDtypeStruct((bsz, seq, n_steps * wide), BF16),
        scratch_shapes=[f32s, f32s, f32s, br3, br3, br3],
        compiler_params=_cparams(("parallel", "arbitrary")),
        name="attn_pool",
    )(proj, proj, proj, bias, pool_w, pool_scale.reshape(1, N_POOL * HEAD_DIM))


def _hgrn_kernel(q_ref, f_ref, i_ref, gt_ref, lb_ref, on_ref, o_ref, *scratch):
    for u in range(q_ref.shape[1] // HEAD_DIM):
        ln = pl.ds(u * HEAD_DIM, HEAD_DIM)
        _hgrn_head(*(r.at[:, ln] for r in (q_ref, f_ref, i_ref, gt_ref, lb_ref, on_ref, o_ref)),
                   *scratch)


def _hgrn_head(q_ref, f_ref, i_ref, gt_ref, lb_ref, on_ref, o_ref,
               qe_scr, o_scr, u_scr, d_scr, s_scr):
    seq = q_ref.shape[0]
    ck = HGRN_CHUNK
    grp = 4 * ck
    n_grp = seq // grp
    n_chunk = seq // ck
    lb = lb_ref[...]

    ri = lax.broadcasted_iota(jnp.int32, (grp, grp), 0)
    ci = lax.broadcasted_iota(jnp.int32, (grp, grp), 1)
    tri_mask = (ri // ck == ci // ck) & (ci <= ri)
    tri = tri_mask.astype(BF16)

    groups = [slice(gi * grp, (gi + 1) * grp) for gi in range(n_grp)]
    q = _half_silu(q_ref[...].astype(F32))
    f = (0.5 + 0.5 * lb) + (0.5 - 0.5 * lb) * jnp.tanh(f_ref[...].astype(F32))
    k = 1.0 - f
    g = jnp.log2(f)
    g_hi = g.astype(BF16)
    g_lo = (g - g_hi.astype(F32)).astype(BF16)
    b = jnp.concatenate([_dot(tri, g_hi[r]) + _dot(tri, g_lo[r]) for r in groups], axis=0)
    b3 = b.reshape(n_chunk, ck, HEAD_DIM)
    b_mid = b3[:, ck // 2 - 1:ck // 2, :]
    b_last = b3[:, ck - 1:ck, :]
    q3 = q.reshape(n_chunk, ck, HEAD_DIM)
    k3 = k.reshape(n_chunk, ck, HEAD_DIM)
    q_mid = q3 * jnp.exp2(b3 - b_mid)
    k_mid = k3 * jnp.exp2(b_mid - b3)
    qs = q_mid.reshape(seq, HEAD_DIM).astype(BF16)
    ks = k_mid.reshape(seq, HEAD_DIM).astype(BF16)
    qe_scr[...] = (q_mid * jnp.exp2(b_mid)).reshape(seq, HEAD_DIM).astype(BF16)
    kd = (k_mid * jnp.exp2(b_last - b_mid)).reshape(seq, HEAD_DIM).astype(BF16)
    d_scr[...] = jnp.exp2(b_last).reshape(n_chunk, HEAD_DIM)
    for r in groups:
        att = jnp.where(tri_mask, _dot_nt(qs[r], ks[r]), 0.0)
        o_scr[r, :] = _dot(att.astype(BF16), i_ref[r, :])
    for c in range(n_chunk):
        cr = slice(c * ck, (c + 1) * ck)
        u_scr[c] = _dot_tn(i_ref[cr, :], kd[cr])

    st = jnp.zeros((HEAD_DIM, HEAD_DIM), F32)
    for c in range(n_chunk):
        s_scr[c] = st.astype(BF16)
        st = st * d_scr[c:c + 1, :] + u_scr[c]

    for gi in range(n_grp):
        rows = slice(gi * grp, (gi + 1) * grp)
        inter = jnp.concatenate(
            [_dot_nt(qe_scr[gi * grp + c * ck:gi * grp + (c + 1) * ck, :], s_scr[gi * 4 + c])
             for c in range(4)], axis=0)
        o = _rms(o_scr[rows, :] + inter, on_ref[...])
        o_ref[rows, :] = (o * _half_silu(gt_ref[rows, :].astype(F32))).astype(o_ref.dtype)


def _hgrn(proj, lb, out_norm, per=2):
    bsz, seq, _ = proj.shape
    nh = HGRN_HEADS // per
    wide = per * HEAD_DIM
    n_chunk = seq // HGRN_CHUNK
    spec = lambda off: pl.BlockSpec((None, seq, wide), lambda b, h: (b, 0, off + h))
    vec = pl.BlockSpec((1, wide), lambda b, h: (0, h))
    return pl.pallas_call(
        _hgrn_kernel,
        grid=(bsz, nh),
        in_specs=[spec(0), spec(nh), spec(2 * nh), spec(3 * nh), vec, vec],
        out_specs=pl.BlockSpec((None, seq, wide), lambda b, h: (b, 0, h)),
        out_shape=jax.ShapeDtypeStruct((bsz, seq, nh * wide), BF16),
        scratch_shapes=[
            pltpu.VMEM((seq, HEAD_DIM), BF16),
            pltpu.VMEM((seq, HEAD_DIM), F32),
            pltpu.VMEM((n_chunk, HEAD_DIM, HEAD_DIM), F32),
            pltpu.VMEM((n_chunk, HEAD_DIM), F32),
            pltpu.VMEM((n_chunk, HEAD_DIM, HEAD_DIM), BF16),
        ],
        compiler_params=_cparams(("parallel", "arbitrary")),
        name="hgrn2",
    )(proj, proj, proj, proj, lb.reshape(1, -1), out_norm.reshape(1, -1))


def _router_kernel(x_ref, g_ref, w_ref, h_ref, meta_ref, cnt_ref, carry):
    i = pl.program_id(0)
    tm = x_ref.shape[0]

    @pl.when(i == 0)
    def _():
        carry[...] = jnp.zeros_like(carry)

    h = _rms(x_ref[...], g_ref[...])
    h_ref[...] = _pack_bf16_pair(h)
    h_hi = h.astype(BF16)
    h_lo = (h - h_hi.astype(F32)).astype(BF16)
    w = w_ref[...]
    w_hi = w.astype(BF16)
    w_lo = (w - w_hi.astype(F32)).astype(BF16)
    prod = _dot(jnp.concatenate([h_hi, h_lo], axis=0), jnp.concatenate([w_hi, w_lo], axis=1))
    logits = (prod[:tm, :LANES] + prod[:tm, LANES:]) + (prod[tm:, :LANES] + prod[tm:, LANES:])
    lane = lax.broadcasted_iota(jnp.int32, logits.shape, 1)
    logits = jnp.where(lane < N_EXPERTS, logits, NEG)
    m1 = jnp.max(logits, axis=-1, keepdims=True)
    i1 = jnp.min(jnp.where(logits == m1, lane, LANES), axis=-1, keepdims=True)
    rest = jnp.where(lane == i1, NEG, logits)
    m2 = jnp.max(rest, axis=-1, keepdims=True)
    i2 = jnp.min(jnp.where(rest == m2, lane, LANES), axis=-1, keepdims=True)
    e = jnp.exp(m2 - m1)
    g1 = 1.0 / (1.0 + e)
    g2 = e / (1.0 + e)
    oh1 = (lane == i1).astype(F32)
    oh2 = (lane == i2).astype(F32)
    cnt = oh1 + oh2
    ri = lax.broadcasted_iota(jnp.int32, (tm, tm), 0)
    ci = lax.broadcasted_iota(jnp.int32, (tm, tm), 1)
    before = (ci < ri).astype(BF16)
    prior = _dot(before, cnt.astype(BF16)) + carry[...]
    r1 = jnp.sum(oh1 * prior, axis=-1, keepdims=True)
    r2 = jnp.sum(oh2 * prior, axis=-1, keepdims=True)
    carry[...] += jnp.sum(cnt, axis=0, keepdims=True)
    cnt_ref[...] = carry[...]
    meta = jnp.where(lane == 0, i1.astype(F32), 0.0)
    meta = jnp.where(lane == 1, i2.astype(F32), meta)
    meta = jnp.where(lane == 2, r1, meta)
    meta = jnp.where(lane == 3, r2, meta)
    meta = jnp.where(lane == 4, g1, meta)
    meta = jnp.where(lane == 5, g2, meta)
    meta_ref[...] = meta


def _router(x, g, router_w, tm=512):
    t, d = x.shape
    w_pad = jnp.pad(router_w, ((0, 0), (0, LANES - router_w.shape[1])))
    return pl.pallas_call(
        _router_kernel,
        grid=(t // tm,),
        in_specs=[
            pl.BlockSpec((tm, d), lambda i: (i, 0)),
            pl.BlockSpec((1, d), lambda i: (0, 0)),
            pl.BlockSpec((d, LANES), lambda i: (0, 0)),
        ],
        out_specs=[
            pl.BlockSpec((tm, d // 2), lambda i: (i, 0)),
            pl.BlockSpec((tm, LANES), lambda i: (i, 0)),
            pl.BlockSpec((1, LANES), lambda i: (0, 0)),
        ],
        out_shape=[
            jax.ShapeDtypeStruct((t, d // 2), jnp.uint32),
            jax.ShapeDtypeStruct((t, LANES), F32),
            jax.ShapeDtypeStruct((1, LANES), F32),
        ],
        scratch_shapes=[pltpu.VMEM((1, LANES), F32)],
        compiler_params=_cparams(("arbitrary",)),
        name="moe_router",
    )(x, g.reshape(1, d), w_pad)


def _row_copy(src, dst, s, d, sem):
    return pltpu.make_async_copy(src.at[pl.ds(s, 1)], dst.at[pl.ds(d, 1)], sem)


def _dispatch_kernel(dest_ref, h_ref, o_hbm, sem):
    tm = h_ref.shape[0]

    def start(t, carry):
        _row_copy(h_ref, o_hbm, t, dest_ref[0, 0, 2 * t], sem).start(priority=0)
        _row_copy(h_ref, o_hbm, t, dest_ref[0, 0, 2 * t + 1], sem).start(priority=1)
        return carry

    lax.fori_loop(0, tm, start, 0, unroll=8)

    def wait(t, carry):
        _row_copy(h_ref, o_hbm, 0, 0, sem).wait()
        _row_copy(h_ref, o_hbm, 0, 0, sem).wait()
        return carry

    lax.fori_loop(0, tm, wait, 0, unroll=8)


def _dispatch(h, dest, n_rows, tm=2048):
    t, d = h.shape
    dest3 = dest.reshape(t // tm, 1, 2 * tm)
    return pl.pallas_call(
        _dispatch_kernel,
        grid=(t // tm,),
        in_specs=[
            pl.BlockSpec((1, 1, 2 * tm), lambda i: (i, 0, 0), memory_space=pltpu.SMEM),
            pl.BlockSpec((tm, d), lambda i: (i, 0)),
        ],
        out_specs=pl.BlockSpec(memory_space=pl.ANY),
        out_shape=jax.ShapeDtypeStruct((n_rows, d), h.dtype),
        scratch_shapes=[pltpu.SemaphoreType.DMA(())],
        compiler_params=_cparams(("arbitrary",)),
        name="moe_dispatch",
    )(dest3, h)


def _moe_kernel(be_ref, bb_ref, nv_ref, na_ref, x_ref, wg_ref, wu_ref, wd_ref, o_ref,
                xb_scr, acc_scr):
    i = pl.program_id(0)
    j = pl.program_id(1)
    n_valid = nv_ref[i]

    @pl.when(i < na_ref[0])
    def _():
        @pl.when(j == 0)
        def _():
            row = lax.broadcasted_iota(jnp.int32, x_ref.shape, 0)
            lo, hi = _unpack_bf16_pair(jnp.where(row < n_valid, x_ref[...], jnp.uint32(0)))
            xb_scr[:, :lo.shape[1]] = lo
            xb_scr[:, lo.shape[1]:] = hi
            acc_scr[...] = jnp.zeros_like(acc_scr)

        def run(rows):
            xb = xb_scr[rows, :]
            a = _dot(xb, wg_ref[0].astype(BF16))
            b = _dot(xb, wu_ref[0].astype(BF16))
            hid = (_silu(a) * b).astype(BF16)
            acc_scr[rows, :] += _dot(hid, wd_ref[0].astype(BF16))

        part = x_ref.shape[0] // MOE_PARTS
        for nq in range(1, MOE_PARTS + 1):
            @pl.when((n_valid > (nq - 1) * part) & (n_valid <= nq * part))
            def _(nq=nq):
                run(slice(0, nq * part))

        @pl.when(j == pl.num_programs(1) - 1)
        def _():
            o_ref[...] = _pack_bf16_pair(acc_scr[...])


def _moe_experts(x_rows, wg, wu, wd, blk_e, blk_idx, blk_valid, n_live, n_blocks):
    dp = x_rows.shape[1]
    d = 2 * dp
    dff = wg.shape[2]
    n_ff = dff // MOE_TF
    r = MOE_ROWS

    def ffi(i, j, na):
        return jnp.where(i < na[0], j, n_ff - 1)

    grid_spec = pltpu.PrefetchScalarGridSpec(
        num_scalar_prefetch=4,
        grid=(n_blocks, n_ff),
        in_specs=[
            pl.BlockSpec((r, dp), lambda i, j, be, bb, nv, na: (bb[i], 0)),
            pl.BlockSpec((1, d, MOE_TF), lambda i, j, be, bb, nv, na: (be[i], 0, ffi(i, j, na))),
            pl.BlockSpec((1, d, MOE_TF), lambda i, j, be, bb, nv, na: (be[i], 0, ffi(i, j, na))),
            pl.BlockSpec((1, MOE_TF, d), lambda i, j, be, bb, nv, na: (be[i], ffi(i, j, na), 0)),
        ],
        out_specs=pl.BlockSpec((r, dp), lambda i, j, be, bb, nv, na: (bb[i], 0)),
        scratch_shapes=[pltpu.VMEM((r, d), BF16), pltpu.VMEM((r, d), F32)],
    )
    return pl.pallas_call(
        _moe_kernel,
        grid_spec=grid_spec,
        out_shape=jax.ShapeDtypeStruct(x_rows.shape, jnp.uint32),
        compiler_params=_cparams(("arbitrary", "arbitrary"), vmem=BIG_VMEM_LIMIT),
        name="moe_experts",
    )(blk_e, blk_idx, blk_valid, n_live, x_rows, wg, wu, wd)


def _combine_kernel(dest_ref, next_ref, x_ref, meta_ref, nf_ref, y_hbm, o_ref, buf, sem, xs):
    i = pl.program_id(0)
    tm = x_ref.shape[0]
    slot = i % 2

    def gather(idx_ref, s):
        def start(t, carry):
            _row_copy(y_hbm, buf.at[s, 0], idx_ref[0, 0, 2 * t], t, sem.at[s]).start(priority=0)
            _row_copy(y_hbm, buf.at[s, 1], idx_ref[0, 0, 2 * t + 1], t, sem.at[s]).start(priority=1)
            return carry

        lax.fori_loop(0, tm, start, 0, unroll=8)

    @pl.when(i == 0)
    def _():
        gather(dest_ref, slot)

    @pl.when(i + 1 < pl.num_programs(0))
    def _():
        gather(next_ref, 1 - slot)

    def wait(t, carry):
        _row_copy(y_hbm, buf.at[slot, 0], 0, 0, sem.at[slot]).wait()
        _row_copy(y_hbm, buf.at[slot, 1], 0, 0, sem.at[slot]).wait()
        return carry

    lax.fori_loop(0, tm, wait, 0, unroll=8)

    meta = meta_ref[...]
    g1 = meta[:, 4:5]
    g2 = meta[:, 5:6]
    half = buf.shape[3]
    for cols, part in zip((slice(0, half), slice(half, 2 * half)), (0, 1)):
        y1 = _unpack_bf16_pair(buf[slot, 0])[part].astype(F32)
        y2 = _unpack_bf16_pair(buf[slot, 1])[part].astype(F32)
        xs[:, cols] = x_ref[:, cols] + (g1 * y1 + g2 * y2)
    o_ref[...] = _rms(xs[...], nf_ref[...])


def _combine(x, meta, norm_final, y_rows, dest, tm=512):
    t, d = x.shape
    n_steps = t // tm
    dest3 = dest.reshape(n_steps, 1, 2 * tm)
    return pl.pallas_call(
        _combine_kernel,
        grid=(n_steps,),
        in_specs=[
            pl.BlockSpec((1, 1, 2 * tm), lambda i: (i, 0, 0), memory_space=pltpu.SMEM),
            pl.BlockSpec((1, 1, 2 * tm), lambda i: (jnp.minimum(i + 1, n_steps - 1), 0, 0),
                         memory_space=pltpu.SMEM),
            pl.BlockSpec((tm, d), lambda i: (i, 0)),
            pl.BlockSpec((tm, LANES), lambda i: (i, 0)),
            pl.BlockSpec((1, d), lambda i: (0, 0)),
            pl.BlockSpec(memory_space=pl.ANY),
        ],
        out_specs=pl.BlockSpec((tm, d), lambda i: (i, 0)),
        out_shape=jax.ShapeDtypeStruct((t, d), F32),
        scratch_shapes=[pltpu.VMEM((2, 2, tm, d // 2), jnp.uint32),
                        pltpu.SemaphoreType.DMA((2,)), pltpu.VMEM((tm, d), F32)],
        compiler_params=_cparams(("arbitrary",)),
        name="moe_combine",
    )(dest3, dest3, x, meta, norm_final.reshape(1, d), y_rows)


def _moe_block_table(counts, t):
    r = MOE_ROWS
    n_blocks = (2 * t) // r + N_EXPERTS
    nblk = (counts + r - 1) // r
    ends = jnp.cumsum(nblk)
    starts = ends - nblk
    n_live = ends[-1]
    step = jnp.minimum(jnp.arange(n_blocks, dtype=jnp.int32), n_live - 1)
    blk_e = jnp.sum(step[:, None] >= ends[None, :], axis=1).astype(jnp.int32)
    local = step - starts[blk_e]
    blk_valid = jnp.clip(counts[blk_e] - local * r, 0, r)
    return (starts * r, blk_e, step, blk_valid.astype(jnp.int32),
            n_live.reshape(1).astype(jnp.int32), n_blocks)


def kernel(x, norm_mix, norm_ffn, w_in_even, pool_w, pool_scale, w_out_even, ffn_w_gate,
           ffn_w_up, ffn_w_down, w_in_odd, lower_bound_logits, hgrn_out_norm, w_out_odd,
           router_w, moe_w_gate, moe_w_up, moe_w_down, norm_final):
    bsz, seq, d = x.shape
    t = bsz * seq

    lb_all = jnp.cumsum(jax.nn.softmax(lower_bound_logits.astype(F32), axis=0), axis=0)
    lb_all = lb_all - lb_all[:1]

    n_in = w_in_even.shape[2]
    q_scale = math.log2(math.e) / math.sqrt(HEAD_DIM)
    col_scale = jnp.where(jnp.arange(n_in) < A_WIDTH, q_scale, 1.0).astype(F32)
    proj = _norm_matmul_res(x, norm_mix[0], w_in_even[0].astype(BF16), col_scale)
    mix = _attn_pool(proj.reshape(bsz, seq, -1), pool_w[0], pool_scale[0])
    xt = _matmul_unres(mix.reshape(t, -1), w_out_even[0], x)
    xt = _swiglu_res(xt, norm_ffn[0], ffn_w_gate[0].astype(BF16), ffn_w_up[0].astype(BF16),
                     ffn_w_down[0].astype(BF16))

    wid = HGRN_HEADS * HEAD_DIM
    col = jnp.arange(4 * wid) // wid
    proj = _norm_matmul(xt, norm_mix[1], w_in_odd[0].astype(BF16),
                        jnp.where(col == 2, 1.0, 0.5).astype(F32))
    o = _hgrn(proj.reshape(bsz, seq, -1), lb_all[1], hgrn_out_norm[0])
    xt = _matmul_res(o.reshape(t, -1), w_out_odd[0], xt)

    h, meta, cnt = _router(xt, norm_ffn[1], router_w[0])
    counts = cnt[0, :N_EXPERTS].astype(jnp.int32)
    row0, blk_e, blk_idx, blk_valid, n_live, n_blocks = _moe_block_table(counts, t)
    dest = (row0[meta[:, 0:2].astype(jnp.int32)] + meta[:, 2:4].astype(jnp.int32)).reshape(-1)
    x_rows = _dispatch(h, dest, n_blocks * MOE_ROWS)
    y_rows = _moe_experts(x_rows, moe_w_gate[0], moe_w_up[0], moe_w_down[0],
                          blk_e, blk_idx, blk_valid, n_live, n_blocks)
    out = _combine(xt, meta, norm_final, y_rows, dest)
    return out.reshape(bsz, seq, d)
```

```python
import math

import jax
import jax.numpy as jnp
import numpy as np
from jax import lax
from jax.experimental import pallas as pl
from jax.experimental.pallas import tpu as pltpu

F32 = jnp.float32
BF16 = jnp.bfloat16

HEAD_DIM = 128
A_HEADS = 12
A_WIDTH = A_HEADS * HEAD_DIM
N_POOL = 4
POOL_WINDOWS = (2, 4, 8, 16)
HGRN_HEADS = 16
HGRN_CHUNK = 64
N_EXPERTS = 8
EPS = 1e-6
NEG = -1e30

LANES = 128
VMEM_LIMIT = 56 * 1024 * 1024

MOE_ROWS = 1024
MOE_PARTS = 4
MOE_TF = 512
BIG_VMEM_LIMIT = 62 * 1024 * 1024
HI16 = 0xFFFF0000


def _pack_bf16_pair(x):
    n = x.shape[1] // 2
    bits = lax.bitcast_convert_type(x.astype(BF16).astype(F32), jnp.uint32)
    return (bits[:, n:] & jnp.uint32(HI16)) | (bits[:, :n] >> 16)


def _unpack_bf16_pair(p):
    lo = lax.bitcast_convert_type(p << 16, F32).astype(BF16)
    hi = lax.bitcast_convert_type(p & jnp.uint32(HI16), F32).astype(BF16)
    return lo, hi

N_RES = 16


def _cparams(sem, vmem=VMEM_LIMIT, flags=None):
    return pltpu.CompilerParams(dimension_semantics=sem, vmem_limit_bytes=vmem, flags=flags)


def _rms(x, g):
    ms = jnp.mean(x * x, axis=-1, keepdims=True)
    return x * lax.rsqrt(ms + EPS) * g


def _half_silu(h):
    return h + h * jnp.tanh(h)


def _silu(x):
    return _half_silu(0.5 * x)


def _dot(a, b):
    return jnp.dot(a, b, preferred_element_type=F32)


def _dot_nt(a, b):
    return lax.dot_general(a, b, (((1,), (1,)), ((), ())), preferred_element_type=F32)


def _dot_tn(a, b):
    return lax.dot_general(a, b, (((0,), (0,)), ((), ())), preferred_element_type=F32)


def _norm_matmul_kernel(x_ref, g_ref, w_ref, cs_ref, o_ref, h_scr):
    @pl.when(pl.program_id(1) == 0)
    def _():
        h_scr[...] = _rms(x_ref[...], g_ref[...]).astype(BF16)

    o_ref[...] = (_dot(h_scr[...], w_ref[...]) * cs_ref[...]).astype(o_ref.dtype)


def _norm_matmul(x, g, w, col_scale, tm=1024, tn=2048):
    m, d = x.shape
    n = w.shape[1]
    return pl.pallas_call(
        _norm_matmul_kernel,
        grid=(m // tm, n // tn),
        in_specs=[
            pl.BlockSpec((tm, d), lambda i, j: (i, 0)),
            pl.BlockSpec((1, d), lambda i, j: (0, 0)),
            pl.BlockSpec((d, tn), lambda i, j: (0, j)),
            pl.BlockSpec((1, tn), lambda i, j: (0, j)),
        ],
        out_specs=pl.BlockSpec((tm, tn), lambda i, j: (i, j)),
        out_shape=jax.ShapeDtypeStruct((m, n), BF16),
        scratch_shapes=[pltpu.VMEM((tm, d), BF16)],
        compiler_params=_cparams(("parallel", "arbitrary")),
        name="norm_matmul",
    )(x, g.reshape(1, d), w, col_scale.reshape(1, n))


def _norm_matmul_res_kernel(x_hbm, g_ref, w_ref, cs_ref, o_ref, xbuf, h_scr, sem):
    i = pl.program_id(0)
    n_res_tile, rows, _ = xbuf.shape[1:]
    halves = N_RES // n_res_tile

    def fetch(step, slot):
        b = step // halves
        r0 = (step % halves) * n_res_tile
        return [pltpu.make_async_copy(x_hbm.at[b, :, r0 + r, :], xbuf.at[slot, r], sem.at[slot, r])
                for r in range(n_res_tile)]

    @pl.when(pl.program_id(1) == 0)
    def _():
        slot = i % 2

        @pl.when(i == 0)
        def _():
            for c in fetch(i, slot):
                c.start()

        @pl.when(i + 1 < pl.num_programs(0))
        def _():
            for c in fetch(i + 1, 1 - slot):
                c.start()

        for r, c in enumerate(fetch(i, slot)):
            c.wait()
            h_scr[r * rows:(r + 1) * rows, :] = _rms(xbuf[slot, r], g_ref[...]).astype(BF16)

    o_ref[...] = (_dot(h_scr[...], w_ref[...]) * cs_ref[...]).astype(o_ref.dtype)


def _norm_matmul_res(x, g, w, col_scale, n_res_tile=8, tn=2560):
    bsz, seq, d = x.shape
    n = w.shape[1]
    rows = seq // N_RES
    tm = n_res_tile * rows
    halves = N_RES // n_res_tile
    return pl.pallas_call(
        _norm_matmul_res_kernel,
        grid=(bsz * halves, n // tn),
        in_specs=[
            pl.BlockSpec(memory_space=pl.ANY),
            pl.BlockSpec((1, d), lambda i, j: (0, 0)),
            pl.BlockSpec((d, tn), lambda i, j: (0, j)),
            pl.BlockSpec((1, tn), lambda i, j: (0, j)),
        ],
        out_specs=pl.BlockSpec((tm, tn), lambda i, j: (i, j)),
        out_shape=jax.ShapeDtypeStruct((bsz * seq, n), BF16),
        scratch_shapes=[pltpu.VMEM((2, n_res_tile, rows, d), F32), pltpu.VMEM((tm, d), BF16),
                        pltpu.SemaphoreType.DMA((2, n_res_tile))],
        compiler_params=_cparams(("arbitrary", "arbitrary")),
        name="norm_matmul_res",
    )(x.reshape(bsz, rows, N_RES, d), g.reshape(1, d), w, col_scale.reshape(1, n))


OUT_CHUNK = 512


def _matmul_res_kernel(a_ref, w_ref, r_ref, o_ref):
    for c0 in range(0, o_ref.shape[1], OUT_CHUNK):
        cols = slice(c0, c0 + OUT_CHUNK)
        o_ref[:, cols] = r_ref[:, cols] + _dot(a_ref[...], w_ref[:, cols].astype(BF16))


def _matmul_res(a, w, res, tm=512, tn=2048):
    m, k = a.shape
    n = w.shape[1]
    return pl.pallas_call(
        _matmul_res_kernel,
        grid=(m // tm, n // tn),
        in_specs=[
            pl.BlockSpec((tm, k), lambda i, j: (i, 0)),
            pl.BlockSpec((k, tn), lambda i, j: (0, j)),
            pl.BlockSpec((tm, tn), lambda i, j: (i, j)),
        ],
        out_specs=pl.BlockSpec((tm, tn), lambda i, j: (i, j)),
        out_shape=jax.ShapeDtypeStruct((m, n), F32),
        compiler_params=_cparams(("parallel", "arbitrary")),
        name="matmul_res",
    )(a, w, res)


def _matmul_unres_kernel(a_ref, w_ref, r_hbm, o_hbm, rbuf, obuf, rsem, osem):
    i = pl.program_id(0)
    n_steps = pl.num_programs(0)
    n_res_tile, rows, _ = rbuf.shape[1:]
    parts = N_RES // n_res_tile

    def res_fetch(step, slot):
        b, r0 = step // parts, (step % parts) * n_res_tile
        return [pltpu.make_async_copy(r_hbm.at[b, :, r0 + r, :], rbuf.at[slot, r], rsem.at[slot, r])
                for r in range(n_res_tile)]

    def out_store(step, slot):
        b, r0 = step // parts, (step % parts) * n_res_tile
        return [pltpu.make_async_copy(obuf.at[slot, r], o_hbm.at[b, :, r0 + r, :], osem.at[slot, r])
                for r in range(n_res_tile)]

    slot = i % 2

    @pl.when(i == 0)
    def _():
        for c in res_fetch(i, slot):
            c.start()

    @pl.when(i + 1 < n_steps)
    def _():
        for c in res_fetch(i + 1, 1 - slot):
            c.start()

    @pl.when(i >= 2)
    def _():
        for c in out_store(i - 2, slot):
            c.wait()

    for c in res_fetch(i, slot):
        c.wait()
    for c0 in range(0, w_ref.shape[1], OUT_CHUNK):
        cols = slice(c0, c0 + OUT_CHUNK)
        y = _dot(a_ref[...], w_ref[:, cols].astype(BF16))
        for r in range(n_res_tile):
            obuf[slot, r, :, cols] = rbuf[slot, r, :, cols] + y[r * rows:(r + 1) * rows, :]
    for c in out_store(i, slot):
        c.start()

    @pl.when(i == n_steps - 1)
    def _():
        for c in out_store(i - 1, 1 - slot) + out_store(i, slot):
            c.wait()


def _matmul_unres(a, w, res, n_res_tile=4):
    bsz, seq, d = res.shape
    k = a.shape[1]
    rows = seq // N_RES
    tm = n_res_tile * rows
    parts = N_RES // n_res_tile
    assert bsz * parts >= 2
    buf = pltpu.VMEM((2, n_res_tile, rows, d), F32)
    out = pl.pallas_call(
        _matmul_unres_kernel,
        grid=(bsz * parts,),
        in_specs=[
            pl.BlockSpec((tm, k), lambda i: (i, 0)),
            pl.BlockSpec((k, d), lambda i: (0, 0)),
            pl.BlockSpec(memory_space=pl.ANY),
        ],
        out_specs=pl.BlockSpec(memory_space=pl.ANY),
        out_shape=jax.ShapeDtypeStruct((bsz, rows, N_RES, d), F32),
        scratch_shapes=[buf, buf, pltpu.SemaphoreType.DMA((2, n_res_tile)),
                        pltpu.SemaphoreType.DMA((2, n_res_tile))],
        compiler_params=_cparams(("arbitrary",)),
        name="matmul_unres",
    )(a, w, res.reshape(bsz, rows, N_RES, d))
    return out.reshape(bsz * seq, d)


def _swiglu_kernel(x_ref, g_ref, wg_ref, wu_ref, wd_ref, o_ref, h_scr):
    def step(rows):
        h = h_scr[rows, :]
        a = _dot(h, wg_ref[...])
        b = _dot(h, wu_ref[...])
        hid = (_silu(a) * b).astype(BF16)
        o_ref[rows, :] += _dot(hid, wd_ref[...])

    @pl.when(pl.program_id(1) == 0)
    def _():
        half = x_ref.shape[0] // 2
        for rows in (slice(0, half), slice(half, 2 * half)):
            h_scr[rows, :] = _rms(x_ref[rows, :], g_ref[...]).astype(BF16)
            o_ref[rows, :] = x_ref[rows, :]
        for rows in (slice(0, half), slice(half, 2 * half)):
            step(rows)

    @pl.when(pl.program_id(1) > 0)
    def _():
        step(slice(None))


def _swiglu_res(x, g, wg, wu, wd, tm=1024, tf=512):
    m, d = x.shape
    ff = wg.shape[1]
    return pl.pallas_call(
        _swiglu_kernel,
        grid=(m // tm, ff // tf),
        in_specs=[
            pl.BlockSpec((tm, d), lambda i, j: (i, 0)),
            pl.BlockSpec((1, d), lambda i, j: (0, 0)),
            pl.BlockSpec((d, tf), lambda i, j: (0, j)),
            pl.BlockSpec((d, tf), lambda i, j: (0, j)),
            pl.BlockSpec((tf, d), lambda i, j: (j, 0)),
        ],
        out_specs=pl.BlockSpec((tm, d), lambda i, j: (i, 0)),
        out_shape=jax.ShapeDtypeStruct((m, d), F32),
        scratch_shapes=[pltpu.VMEM((tm, d), BF16)],
        compiler_params=_cparams(("parallel", "arbitrary"), vmem=BIG_VMEM_LIMIT),
        name="swiglu_res",
    )(x, g.reshape(1, d), wg, wu, wd)


def _attn_bias_table():
    blk = HEAD_DIM
    iq = np.arange(blk)[:, None]
    jk = np.arange(2 * blk)[None, :]

    def offs(idx, n_chunk, size):
        return n_chunk * (idx % size) + idx // size

    masks = []
    for n_chunk in (1, 4, 16):
        size = blk // n_chunk
        lq = offs(iq, n_chunk, size)
        masks.append((offs(jk, n_chunk, size) <= lq) & (jk < blk))
        dist = lq - (offs(jk, n_chunk, 2 * size) - blk)
        masks.append((dist >= 0) & (dist <= blk))
    return np.where(np.stack(masks), 0.0, NEG).astype(np.float32)


def _attn_pool_kernel(q_ref, k_ref, v_ref, bias, pw_ref, ps_ref, o_ref,
                      qf, kf, vf, ms, ls, accs):
    h = pl.program_id(1)
    per = q_ref.shape[1] // HEAD_DIM
    lanes = [pl.ds(u * HEAD_DIM, HEAD_DIM) for u in range(per)]

    @pl.when(h < A_HEADS // per)
    def _():
        for ln in lanes:
            _attend(q_ref.at[:, ln], k_ref.at[:, ln], v_ref.at[:, ln], bias, o_ref.at[:, ln],
                    qf, kf, vf, ms, ls, accs)

    for pp in range(N_POOL // per):
        @pl.when(h == A_HEADS // per + pp)
        def _(pp=pp):
            for u, ln in enumerate(lanes):
                gi = pp * per + u
                _pool(q_ref.at[:, ln], pw_ref.at[gi], ps_ref.at[:, ln], o_ref.at[:, ln],
                      POOL_WINDOWS[gi])


def _attend(q_ref, k_ref, v_ref, bias, o_ref, qf, kf, vf, ms, ls, accs):
    seq = q_ref.shape[0]
    blk = HEAD_DIM
    bf16_rows = 16

    qf[...] = q_ref[...].astype(F32)
    kf[...] = k_ref[...].astype(F32)
    vf[...] = v_ref[...].astype(F32)

    def gather(ref, ref_f32, starts, size):
        if all(v % bf16_rows == 0 for v in [size] + starts):
            return jnp.concatenate([ref[s0:s0 + size, :] for s0 in starts], axis=0)
        return jnp.concatenate([ref_f32[s0:s0 + size, :] for s0 in starts], axis=0).astype(BF16)

    def scatter(br, starts, size, m, l, acc):
        mb = jnp.broadcast_to(m, acc.shape)
        for c, s0 in enumerate(starts):
            src = slice(c * size, (c + 1) * size)
            ms[br, s0:s0 + size, :] = mb[src]
            ls[br, s0:s0 + size, :] = l[src]
            accs[br, s0:s0 + size, :] = acc[src]

    kinds = {}
    for br, n_chunk in enumerate((1, 4, 16)):
        size = blk // n_chunk
        for grp in range(N_RES // n_chunk):
            bases = [(grp + (N_RES // n_chunk) * c) * blk for c in range(n_chunk)]
            for n in range(n_chunk):
                qs = [b0 + n * size for b0 in bases]
                if n == 0:
                    kinds.setdefault((2 * br, blk), []).append((br, size, qs, qs, size))
                else:
                    ks = [s0 - size for s0 in qs]
                    kinds.setdefault((2 * br + 1, 2 * blk), []).append(
                        (br, size, qs, ks, 2 * size))

    batch = 16
    for (bias_idx, kw), blocks in kinds.items():
        bi = bias[bias_idx][:, :kw]
        for g0 in range(0, len(blocks), batch):
            group = blocks[g0:g0 + batch]
            s = jnp.concatenate(
                [_dot_nt(gather(q_ref, qf, qs, size), gather(k_ref, kf, ks, ksize))
                 for _, size, qs, ks, ksize in group], axis=0)
            s = (s.reshape(len(group), blk, kw) + bi[None]).reshape(len(group) * blk, kw)
            m = jnp.max(s, axis=-1, keepdims=True)
            pb = jnp.exp2(s - m).astype(BF16)
            ones = jnp.ones((kw, blk), BF16)
            for g, (br, size, qs, ks, ksize) in enumerate(group):
                rows = slice(g * blk, (g + 1) * blk)
                v_one = jnp.concatenate([gather(v_ref, vf, ks, ksize), ones], axis=1)
                acc = _dot(pb[rows], v_one)
                scatter(br, qs, size, m[rows], acc[:, blk:], acc[:, :blk])

    slab = 2 * blk
    for s0 in range(0, seq, slab):
        rows = slice(s0, s0 + slab)
        m_all = jnp.maximum(jnp.maximum(ms[0, rows, :], ms[1, rows, :]), ms[2, rows, :])
        num = jnp.zeros((slab, blk), F32)
        den = jnp.zeros((slab, blk), F32)
        for br in range(3):
            w = jnp.exp2(ms[br, rows, :] - m_all)
            num = num + accs[br, rows, :] * w
            den = den + ls[br, rows, :] * w
        o_ref[rows, :] = (num / den).astype(o_ref.dtype)


def _pool(p_ref, pw_ref, ps_ref, o_ref, w):
    blk = HEAD_DIM
    pf = p_ref[...].astype(F32)
    row = lax.broadcasted_iota(jnp.int32, pf.shape, 0)
    l_idx = row % blk
    pos = N_RES * l_idx + row // blk
    s = pf
    k = 1
    while k < w:
        y = pltpu.roll(s, k * blk, axis=0)
        y1 = jnp.where(l_idx == 0, 0.0, pltpu.roll(y, 1, axis=0))
        s = s + jnp.where(row < k * blk, y1, y)
        k *= 2
    cnt = jnp.minimum(pos + 1, w).astype(F32)
    pooled = s / cnt - pf
    y = _dot(pooled.astype(BF16), pw_ref[...].astype(BF16)) * ps_ref[...]
    o_ref[...] = y.astype(o_ref.dtype)


def _attn_pool(proj, pool_w, pool_scale, per=2):
    bsz, seq, _ = proj.shape
    na = A_HEADS // per
    n_steps = na + N_POOL // per
    wide = per * HEAD_DIM
    qmap = lambda b, h: (b, 0, jnp.where(h < na, h, h + 2 * na))
    kmap = lambda b, h: (b, 0, na + jnp.minimum(h, na - 1))
    vmap = lambda b, h: (b, 0, 2 * na + jnp.minimum(h, na - 1))
    f32s = pltpu.VMEM((seq, HEAD_DIM), F32)
    br3 = pltpu.VMEM((3, seq, HEAD_DIM), F32)
    bias = jnp.asarray(_attn_bias_table())
    return pl.pallas_call(
        _attn_pool_kernel,
        grid=(bsz, n_steps),
        in_specs=[
            pl.BlockSpec((None, seq, wide), qmap),
            pl.BlockSpec((None, seq, wide), kmap),
            pl.BlockSpec((None, seq, wide), vmap),
            pl.BlockSpec(bias.shape, lambda b, h: (0, 0, 0)),
            pl.BlockSpec((N_POOL, HEAD_DIM, HEAD_DIM), lambda b, h: (0, 0, 0)),
            pl.BlockSpec((1, wide), lambda b, h: (0, jnp.maximum(h - na, 0))),
        ],
        out_specs=pl.BlockSpec((None, seq, wide), lambda b, h: (b, 0, h)),
        out_shape=jax.ShapeDtypeStruct((bsz, seq, n_steps * wide), BF16),
        scratch_shapes=[f32s, f32s, f32s, br3, br3, br3],
        compiler_params=_cparams(("parallel", "arbitrary")),
        name="attn_pool",
    )(proj, proj, proj, bias, pool_w, pool_scale.reshape(1, N_POOL * HEAD_DIM))


def _hgrn_kernel(q_ref, f_ref, i_ref, gt_ref, lb_ref, on_ref, o_ref, *scratch):
    for u in range(q_ref.shape[1] // HEAD_DIM):
        ln = pl.ds(u * HEAD_DIM, HEAD_DIM)
        _hgrn_head(*(r.at[:, ln] for r in (q_ref, f_ref, i_ref, gt_ref, lb_ref, on_ref, o_ref)),
                   *scratch)


def _hgrn_head(q_ref, f_ref, i_ref, gt_ref, lb_ref, on_ref, o_ref,
               qe_scr, o_scr, u_scr, d_scr, s_scr):
    seq = q_ref.shape[0]
    ck = HGRN_CHUNK
    grp = 4 * ck
    n_grp = seq // grp
    n_chunk = seq // ck
    lb = lb_ref[...]

    ri = lax.broadcasted_iota(jnp.int32, (grp, grp), 0)
    ci = lax.broadcasted_iota(jnp.int32, (grp, grp), 1)
    tri_mask = (ri // ck == ci // ck) & (ci <= ri)
    tri = tri_mask.astype(BF16)

    groups = [slice(gi * grp, (gi + 1) * grp) for gi in range(n_grp)]
    q = _half_silu(q_ref[...].astype(F32))
    f = (0.5 + 0.5 * lb) + (0.5 - 0.5 * lb) * jnp.tanh(f_ref[...].astype(F32))
    k = 1.0 - f
    g = jnp.log2(f)
    g_hi = g.astype(BF16)
    g_lo = (g - g_hi.astype(F32)).astype(BF16)
    b = jnp.concatenate([_dot(tri, g_hi[r]) + _dot(tri, g_lo[r]) for r in groups], axis=0)
    b3 = b.reshape(n_chunk, ck, HEAD_DIM)
    b_mid = b3[:, ck // 2 - 1:ck // 2, :]
    b_last = b3[:, ck - 1:ck, :]
    q3 = q.reshape(n_chunk, ck, HEAD_DIM)
    k3 = k.reshape(n_chunk, ck, HEAD_DIM)
    q_mid = q3 * jnp.exp2(b3 - b_mid)
    k_mid = k3 * jnp.exp2(b_mid - b3)
    qs = q_mid.reshape(seq, HEAD_DIM).astype(BF16)
    ks = k_mid.reshape(seq, HEAD_DIM).astype(BF16)
    qe_scr[...] = (q_mid * jnp.exp2(b_mid)).reshape(seq, HEAD_DIM).astype(BF16)
    kd = (k_mid * jnp.exp2(b_last - b_mid)).reshape(seq, HEAD_DIM).astype(BF16)
    d_scr[...] = jnp.exp2(b_last).reshape(n_chunk, HEAD_DIM)
    for r in groups:
        att = jnp.where(tri_mask, _dot_nt(qs[r], ks[r]), 0.0)
        o_scr[r, :] = _dot(att.astype(BF16), i_ref[r, :])
    for c in range(n_chunk):
        cr = slice(c * ck, (c + 1) * ck)
        u_scr[c] = _dot_tn(i_ref[cr, :], kd[cr])

    st = jnp.zeros((HEAD_DIM, HEAD_DIM), F32)
    for c in range(n_chunk):
        s_scr[c] = st.astype(BF16)
        st = st * d_scr[c:c + 1, :] + u_scr[c]

    for gi in range(n_grp):
        rows = slice(gi * grp, (gi + 1) * grp)
        inter = jnp.concatenate(
            [_dot_nt(qe_scr[gi * grp + c * ck:gi * grp + (c + 1) * ck, :], s_scr[gi * 4 + c])
             for c in range(4)], axis=0)
        o = _rms(o_scr[rows, :] + inter, on_ref[...])
        o_ref[rows, :] = (o * _half_silu(gt_ref[rows, :].astype(F32))).astype(o_ref.dtype)


def _hgrn(proj, lb, out_norm, per=2):
    bsz, seq, _ = proj.shape
    nh = HGRN_HEADS // per
    wide = per * HEAD_DIM
    n_chunk = seq // HGRN_CHUNK
    spec = lambda off: pl.BlockSpec((None, seq, wide), lambda b, h: (b, 0, off + h))
    vec = pl.BlockSpec((1, wide), lambda b, h: (0, h))
    return pl.pallas_call(
        _hgrn_kernel,
        grid=(bsz, nh),
        in_specs=[spec(0), spec(nh), spec(2 * nh), spec(3 * nh), vec, vec],
        out_specs=pl.BlockSpec((None, seq, wide), lambda b, h: (b, 0, h)),
        out_shape=jax.ShapeDtypeStruct((bsz, seq, nh * wide), BF16),
        scratch_shapes=[
            pltpu.VMEM((seq, HEAD_DIM), BF16),
            pltpu.VMEM((seq, HEAD_DIM), F32),
            pltpu.VMEM((n_chunk, HEAD_DIM, HEAD_DIM), F32),
            pltpu.VMEM((n_chunk, HEAD_DIM), F32),
            pltpu.VMEM((n_chunk, HEAD_DIM, HEAD_DIM), BF16),
        ],
        compiler_params=_cparams(("parallel", "arbitrary")),
        name="hgrn2",
    )(proj, proj, proj, proj, lb.reshape(1, -1), out_norm.reshape(1, -1))


def _router_kernel(x_ref, g_ref, w_ref, h_ref, meta_ref, cnt_ref, carry):
    i = pl.program_id(0)
    tm = x_ref.shape[0]

    @pl.when(i == 0)
    def _():
        carry[...] = jnp.zeros_like(carry)

    h = _rms(x_ref[...], g_ref[...])
    h_ref[...] = _pack_bf16_pair(h)
    h_hi = h.astype(BF16)
    h_lo = (h - h_hi.astype(F32)).astype(BF16)
    w = w_ref[...]
    w_hi = w.astype(BF16)
    w_lo = (w - w_hi.astype(F32)).astype(BF16)
    prod = _dot(jnp.concatenate([h_hi, h_lo], axis=0), jnp.concatenate([w_hi, w_lo], axis=1))
    logits = (prod[:tm, :LANES] + prod[:tm, LANES:]) + (prod[tm:, :LANES] + prod[tm:, LANES:])
    lane = lax.broadcasted_iota(jnp.int32, logits.shape, 1)
    logits = jnp.where(lane < N_EXPERTS, logits, NEG)
    m1 = jnp.max(logits, axis=-1, keepdims=True)
    i1 = jnp.min(jnp.where(logits == m1, lane, LANES), axis=-1, keepdims=True)
    rest = jnp.where(lane == i1, NEG, logits)
    m2 = jnp.max(rest, axis=-1, keepdims=True)
    i2 = jnp.min(jnp.where(rest == m2, lane, LANES), axis=-1, keepdims=True)
    e = jnp.exp(m2 - m1)
    g1 = 1.0 / (1.0 + e)
    g2 = e / (1.0 + e)
    oh1 = (lane == i1).astype(F32)
    oh2 = (lane == i2).astype(F32)
    cnt = oh1 + oh2
    ri = lax.broadcasted_iota(jnp.int32, (tm, tm), 0)
    ci = lax.broadcasted_iota(jnp.int32, (tm, tm), 1)
    before = (ci < ri).astype(BF16)
    prior = _dot(before, cnt.astype(BF16)) + carry[...]
    r1 = jnp.sum(oh1 * prior, axis=-1, keepdims=True)
    r2 = jnp.sum(oh2 * prior, axis=-1, keepdims=True)
    carry[...] += jnp.sum(cnt, axis=0, keepdims=True)
    cnt_ref[...] = carry[...]
    meta = jnp.where(lane == 0, i1.astype(F32), 0.0)
    meta = jnp.where(lane == 1, i2.astype(F32), meta)
    meta = jnp.where(lane == 2, r1, meta)
    meta = jnp.where(lane == 3, r2, meta)
    meta = jnp.where(lane == 4, g1, meta)
    meta = jnp.where(lane == 5, g2, meta)
    meta_ref[...] = meta


def _router(x, g, router_w, tm=512):
    t, d = x.shape
    w_pad = jnp.pad(router_w, ((0, 0), (0, LANES - router_w.shape[1])))
    return pl.pallas_call(
        _router_kernel,
        grid=(t // tm,),
        in_specs=[
            pl.BlockSpec((tm, d), lambda i: (i, 0)),
            pl.BlockSpec((1, d), lambda i: (0, 0)),
            pl.BlockSpec((d, LANES), lambda i: (0, 0)),
        ],
        out_specs=[
            pl.BlockSpec((tm, d // 2), lambda i: (i, 0)),
            pl.BlockSpec((tm, LANES), lambda i: (i, 0)),
            pl.BlockSpec((1, LANES), lambda i: (0, 0)),
        ],
        out_shape=[
            jax.ShapeDtypeStruct((t, d // 2), jnp.uint32),
            jax.ShapeDtypeStruct((t, LANES), F32),
            jax.ShapeDtypeStruct((1, LANES), F32),
        ],
        scratch_shapes=[pltpu.VMEM((1, LANES), F32)],
        compiler_params=_cparams(("arbitrary",)),
        name="moe_router",
    )(x, g.reshape(1, d), w_pad)


def _row_copy(src, dst, s, d, sem):
    return pltpu.make_async_copy(src.at[pl.ds(s, 1)], dst.at[pl.ds(d, 1)], sem)


def _dispatch_kernel(dest_ref, h_ref, o_hbm, sem):
    tm = h_ref.shape[0]

    def start(t, carry):
        _row_copy(h_ref, o_hbm, t, dest_ref[0, 0, 2 * t], sem).start(priority=0)
        _row_copy(h_ref, o_hbm, t, dest_ref[0, 0, 2 * t + 1], sem).start(priority=1)
        return carry

    lax.fori_loop(0, tm, start, 0, unroll=8)

    def wait(t, carry):
        _row_copy(h_ref, o_hbm, 0, 0, sem).wait()
        _row_copy(h_ref, o_hbm, 0, 0, sem).wait()
        return carry

    lax.fori_loop(0, tm, wait, 0, unroll=8)


def _dispatch(h, dest, n_rows, tm=2048):
    t, d = h.shape
    dest3 = dest.reshape(t // tm, 1, 2 * tm)
    return pl.pallas_call(
        _dispatch_kernel,
        grid=(t // tm,),
        in_specs=[
            pl.BlockSpec((1, 1, 2 * tm), lambda i: (i, 0, 0), memory_space=pltpu.SMEM),
            pl.BlockSpec((tm, d), lambda i: (i, 0)),
        ],
        out_specs=pl.BlockSpec(memory_space=pl.ANY),
        out_shape=jax.ShapeDtypeStruct((n_rows, d), h.dtype),
        scratch_shapes=[pltpu.SemaphoreType.DMA(())],
        compiler_params=_cparams(("arbitrary",)),
        name="moe_dispatch",
    )(dest3, h)


def _moe_kernel(be_ref, bb_ref, nv_ref, na_ref, x_ref, wg_ref, wu_ref, wd_ref, o_ref, xb_scr):
    i = pl.program_id(0)
    j = pl.program_id(1)
    n_valid = nv_ref[i]

    @pl.when(i < na_ref[0])
    def _():
        @pl.when(j == 0)
        def _():
            row = lax.broadcasted_iota(jnp.int32, x_ref.shape, 0)
            lo, hi = _unpack_bf16_pair(jnp.where(row < n_valid, x_ref[...], jnp.uint32(0)))
            xb_scr[:, :lo.shape[1]] = lo
            xb_scr[:, lo.shape[1]:] = hi
            o_ref[...] = jnp.zeros_like(o_ref)

        def run(rows):
            xb = xb_scr[rows, :]
            a = _dot(xb, wg_ref[0].astype(BF16))
            b = _dot(xb, wu_ref[0].astype(BF16))
            hid = (_silu(a) * b).astype(BF16)
            o_ref[rows, :] += _dot(hid, wd_ref[0].astype(BF16))

        part = x_ref.shape[0] // MOE_PARTS
        for nq in range(1, MOE_PARTS + 1):
            @pl.when((n_valid > (nq - 1) * part) & (n_valid <= nq * part))
            def _(nq=nq):
                run(slice(0, nq * part))


def _moe_experts(x_rows, wg, wu, wd, blk_e, blk_idx, blk_valid, n_live, n_blocks):
    dp = x_rows.shape[1]
    d = 2 * dp
    dff = wg.shape[2]
    n_ff = dff // MOE_TF
    r = MOE_ROWS

    def ffi(i, j, na):
        return jnp.where(i < na[0], j, n_ff - 1)

    grid_spec = pltpu.PrefetchScalarGridSpec(
        num_scalar_prefetch=4,
        grid=(n_blocks, n_ff),
        in_specs=[
            pl.BlockSpec((r, dp), lambda i, j, be, bb, nv, na: (bb[i], 0)),
            pl.BlockSpec((1, d, MOE_TF), lambda i, j, be, bb, nv, na: (be[i], 0, ffi(i, j, na))),
            pl.BlockSpec((1, d, MOE_TF), lambda i, j, be, bb, nv, na: (be[i], 0, ffi(i, j, na))),
            pl.BlockSpec((1, MOE_TF, d), lambda i, j, be, bb, nv, na: (be[i], ffi(i, j, na), 0)),
        ],
        out_specs=pl.BlockSpec((r, d), lambda i, j, be, bb, nv, na: (bb[i], 0)),
        scratch_shapes=[pltpu.VMEM((r, d), BF16)],
    )
    return pl.pallas_call(
        _moe_kernel,
        grid_spec=grid_spec,
        out_shape=jax.ShapeDtypeStruct((x_rows.shape[0], d), F32),
        compiler_params=_cparams(("arbitrary", "arbitrary"), vmem=BIG_VMEM_LIMIT),
        name="moe_experts",
    )(blk_e, blk_idx, blk_valid, n_live, x_rows, wg, wu, wd)


def _combine_kernel(dest_ref, next_ref, x_ref, meta_ref, nf_ref, y_hbm, o_ref, buf, sem):
    i = pl.program_id(0)
    tm = x_ref.shape[0]
    slot = i % 2

    def gather(idx_ref, s):
        def start(t, carry):
            _row_copy(y_hbm, buf.at[s, 0], idx_ref[0, 0, 2 * t], t, sem.at[s]).start(priority=0)
            _row_copy(y_hbm, buf.at[s, 1], idx_ref[0, 0, 2 * t + 1], t, sem.at[s]).start(priority=1)
            return carry

        lax.fori_loop(0, tm, start, 0, unroll=8)

    @pl.when(i == 0)
    def _():
        gather(dest_ref, slot)

    @pl.when(i + 1 < pl.num_programs(0))
    def _():
        gather(next_ref, 1 - slot)

    def wait(t, carry):
        _row_copy(y_hbm, buf.at[slot, 0], 0, 0, sem.at[slot]).wait()
        _row_copy(y_hbm, buf.at[slot, 1], 0, 0, sem.at[slot]).wait()
        return carry

    lax.fori_loop(0, tm, wait, 0, unroll=8)

    meta = meta_ref[...]
    g1 = meta[:, 4:5]
    g2 = meta[:, 5:6]
    x = x_ref[...] + (g1 * buf[slot, 0] + g2 * buf[slot, 1])
    o_ref[...] = _rms(x, nf_ref[...])


def _combine(x, meta, norm_final, y_rows, dest, tm=512):
    t, d = x.shape
    n_steps = t // tm
    dest3 = dest.reshape(n_steps, 1, 2 * tm)
    return pl.pallas_call(
        _combine_kernel,
        grid=(n_steps,),
        in_specs=[
            pl.BlockSpec((1, 1, 2 * tm), lambda i: (i, 0, 0), memory_space=pltpu.SMEM),
            pl.BlockSpec((1, 1, 2 * tm), lambda i: (jnp.minimum(i + 1, n_steps - 1), 0, 0),
                         memory_space=pltpu.SMEM),
            pl.BlockSpec((tm, d), lambda i: (i, 0)),
            pl.BlockSpec((tm, LANES), lambda i: (i, 0)),
            pl.BlockSpec((1, d), lambda i: (0, 0)),
            pl.BlockSpec(memory_space=pl.ANY),
        ],
        out_specs=pl.BlockSpec((tm, d), lambda i: (i, 0)),
        out_shape=jax.ShapeDtypeStruct((t, d), F32),
        scratch_shapes=[pltpu.VMEM((2, 2, tm, d), F32), pltpu.SemaphoreType.DMA((2,))],
        compiler_params=_cparams(("arbitrary",)),
        name="moe_combine",
    )(dest3, dest3, x, meta, norm_final.reshape(1, d), y_rows)


def _moe_block_table(counts, t):
    r = MOE_ROWS
    n_blocks = (2 * t) // r + N_EXPERTS
    nblk = (counts + r - 1) // r
    ends = jnp.cumsum(nblk)
    starts = ends - nblk
    n_live = ends[-1]
    step = jnp.minimum(jnp.arange(n_blocks, dtype=jnp.int32), n_live - 1)
    blk_e = jnp.sum(step[:, None] >= ends[None, :], axis=1).astype(jnp.int32)
    local = step - starts[blk_e]
    blk_valid = jnp.clip(counts[blk_e] - local * r, 0, r)
    return (starts * r, blk_e, step, blk_valid.astype(jnp.int32),
            n_live.reshape(1).astype(jnp.int32), n_blocks)


def kernel(x, norm_mix, norm_ffn, w_in_even, pool_w, pool_scale, w_out_even, ffn_w_gate,
           ffn_w_up, ffn_w_down, w_in_odd, lower_bound_logits, hgrn_out_norm, w_out_odd,
           router_w, moe_w_gate, moe_w_up, moe_w_down, norm_final):
    bsz, seq, d = x.shape
    t = bsz * seq

    lb_all = jnp.cumsum(jax.nn.softmax(lower_bound_logits.astype(F32), axis=0), axis=0)
    lb_all = lb_all - lb_all[:1]

    n_in = w_in_even.shape[2]
    q_scale = math.log2(math.e) / math.sqrt(HEAD_DIM)
    col_scale = jnp.where(jnp.arange(n_in) < A_WIDTH, q_scale, 1.0).astype(F32)
    proj = _norm_matmul_res(x, norm_mix[0], w_in_even[0].astype(BF16), col_scale)
    mix = _attn_pool(proj.reshape(bsz, seq, -1), pool_w[0], pool_scale[0])
    xt = _matmul_unres(mix.reshape(t, -1), w_out_even[0], x)
    xt = _swiglu_res(xt, norm_ffn[0], ffn_w_gate[0].astype(BF16), ffn_w_up[0].astype(BF16),
                     ffn_w_down[0].astype(BF16))

    wid = HGRN_HEADS * HEAD_DIM
    col = jnp.arange(4 * wid) // wid
    proj = _norm_matmul(xt, norm_mix[1], w_in_odd[0].astype(BF16),
                        jnp.where(col == 2, 1.0, 0.5).astype(F32))
    o = _hgrn(proj.reshape(bsz, seq, -1), lb_all[1], hgrn_out_norm[0])
    xt = _matmul_res(o.reshape(t, -1), w_out_odd[0], xt)

    h, meta, cnt = _router(xt, norm_ffn[1], router_w[0])
    counts = cnt[0, :N_EXPERTS].astype(jnp.int32)
    row0, blk_e, blk_idx, blk_valid, n_live, n_blocks = _moe_block_table(counts, t)
    dest = (row0[meta[:, 0:2].astype(jnp.int32)] + meta[:, 2:4].astype(jnp.int32)).reshape(-1)
    x_rows = _dispatch(h, dest, n_blocks * MOE_ROWS)
    y_rows = _moe_experts(x_rows, moe_w_gate[0], moe_w_up[0], moe_w_down[0],
                          blk_e, blk_idx, blk_valid, n_live, n_blocks)
    out = _combine(xt, meta, norm_final, y_rows, dest)
    return out.reshape(bsz, seq, d)
```
